```python
import math
import jax, jax.numpy as jnp
from jax import lax
import numpy as np

D_MODEL = 2048
BATCH = 4
SEQ = 2048
DEPTH = 2
DEC_BATCH = 128
DEC_SEQ = 1
PAST_LEN = 2048
PAGE_SIZE = 128

HEAD_DIM = 128
FOX_HEADS = 8
NSA_HEADS = 8
NSA_GROUPS = 2
NSA_HPG = NSA_HEADS // NSA_GROUPS
NSA_BLOCK = 64
NSA_TOPK = 8
NSA_WINDOW = 512
REL_BUCKETS = 32
REL_MAX_DIST = 1024
GLA_HEADS = 4
GLA_DK = D_MODEL // 2 // GLA_HEADS
GLA_DV = D_MODEL // GLA_HEADS
GLA_GATE_RANK = 16
GLA_TAU = 16.0
GLA_CHUNK = 64
N_EXPERTS = 32
TOP_K = 4
D_EXPERT = D_MODEL
SWIGLU_LIMIT = 7.0
SWIGLU_ALPHA = 1.702
Q_BLOCK = 128
LN_EPS = 1e-5
FORGET_BIAS_INIT = 3.0
N_ATTN_LAYERS = (DEPTH + 1) // 2
N_GLA_LAYERS = DEPTH // 2
DEEPNORM_ALPHA = (2 * DEPTH) ** 0.25
DEEPNORM_BETA = (8 * DEPTH) ** -0.25
FOX_W = FOX_HEADS * HEAD_DIM
NSA_W = NSA_HEADS * HEAD_DIM
NSA_KV_W = NSA_GROUPS * HEAD_DIM
IN_A_SIZES = (FOX_W, FOX_W, FOX_W, FOX_HEADS, NSA_W, NSA_KV_W, NSA_KV_W, NSA_KV_W, NSA_KV_W, NSA_KV_W, NSA_KV_W, 3 * NSA_HEADS)
IN_C_SIZES = (GLA_HEADS * GLA_DK, GLA_HEADS * GLA_DK, GLA_HEADS * GLA_DV, GLA_GATE_RANK, GLA_HEADS * GLA_DV)

kernel_name = 'fox_nsa_gla_moe_decoder_step'


def _split(h, sizes):
    return jnp.split(h, np.cumsum(sizes)[:-1].tolist(), axis=-1)


def _layernorm(x, g, b):
    xf = x.astype(jnp.float32)
    mu = xf.mean(-1, keepdims=True)
    var = jnp.square(xf - mu).mean(-1, keepdims=True)
    return ((xf - mu) * lax.rsqrt(var + LN_EPS) * g + b).astype(x.dtype)


def _masked_softmax(s, mask):
    s = jnp.where(mask, s, -jnp.inf)
    m = jnp.max(s, -1, keepdims=True)
    m = jnp.where(jnp.isfinite(m), m, 0.0)
    e = jnp.exp(s - m)
    return e / jnp.maximum(e.sum(-1, keepdims=True), 1e-30)


def _q_block(s):
    return Q_BLOCK if s % Q_BLOCK == 0 else s


def _to_blocks(a, axis, nq):
    shp = a.shape
    a = a.reshape(shp[:axis] + (nq, shp[axis] // nq) + shp[axis + 1:])
    return jnp.moveaxis(a, axis, 0)


def _from_blocks(o, axis):
    o = jnp.moveaxis(o, 0, axis)
    shp = o.shape
    return o.reshape(shp[:axis] + (shp[axis] * shp[axis + 1],) + shp[axis + 2:])


def _rel_bucket(dist):
    n = jnp.maximum(dist, 0)
    exact = REL_BUCKETS // 2
    nf = jnp.maximum(n, 1).astype(jnp.float32)
    large = exact + (jnp.log(nf / exact) / math.log(REL_MAX_DIST / exact) * (REL_BUCKETS - exact)).astype(jnp.int32)
    return jnp.where(n < exact, n, jnp.minimum(large, REL_BUCKETS - 1))


def _rel_bias_shared(table, dist):
    b = table[_rel_bucket(dist)].astype(jnp.float32).reshape(dist.shape + (NSA_GROUPS, NSA_HPG))
    return jnp.moveaxis(b, (-2, -1), (0, 1))


def _gather_pages(pool, layer, page_table):
    g = pool[layer, page_table]
    return g.reshape((g.shape[0], g.shape[1] * g.shape[2]) + g.shape[3:])


def _fox_attention(q, k, v, c_q, c_k, qpos0):
    N, S = q.shape[:2]
    L = k.shape[1]
    scale = HEAD_DIM ** -0.5
    qb = _q_block(S)
    nq = S // qb
    ck_t = jnp.swapaxes(c_k, 1, 2)
    kpos = jnp.arange(L)

    def blk(args):
        qi, cqi, qp = args
        s = jnp.einsum('nqhd,nkhd->nhqk', qi, k, preferred_element_type=jnp.float32) * scale
        s = s + jnp.swapaxes(cqi, 1, 2)[..., None] - ck_t[:, :, None, :]
        p = _masked_softmax(s, kpos[None, :] <= qp[:, None])
        return jnp.einsum('nhqk,nkhd->nqhd', p.astype(v.dtype), v)

    qpos = (qpos0 + jnp.arange(S)).reshape(nq, qb)
    return _from_blocks(lax.map(blk, (_to_blocks(q, 1, nq), _to_blocks(c_q, 1, nq), qpos)), 1)


def _nsa_attention(q, qpos0, kc, vc, ks, vs, kw, vw, w0, gates, rel_table):
    N, S = q.shape[:2]
    L = kc.shape[1]
    f32 = jnp.float32
    scale = HEAD_DIM ** -0.5
    qg = q.reshape(N, S, NSA_GROUPS, NSA_HPG, HEAD_DIM)
    qpos = qpos0 + jnp.arange(S)
    nb = -(-L // NSA_BLOCK)
    pad = nb * NSA_BLOCK - L

    def blocks(a):
        a = jnp.pad(a, ((0, 0), (0, pad), (0, 0), (0, 0)))
        return a.reshape(N, nb, NSA_BLOCK, NSA_GROUPS, HEAD_DIM)

    kcb = blocks(kc).mean(axis=2)
    vcb = blocks(vc).mean(axis=2)
    blk = jnp.arange(nb)
    blk_end = blk * NSA_BLOCK + NSA_BLOCK - 1
    dist_c = qpos[:, None] - blk_end[None, :]
    s_c = jnp.einsum('nsghd,nbgd->nghsb', qg, kcb, preferred_element_type=f32) * scale + _rel_bias_shared(rel_table, dist_c)
    p_c = _masked_softmax(s_c, dist_c >= 0)
    o_c = jnp.einsum('nghsb,nbgd->nsghd', p_c.astype(vcb.dtype), vcb)

    cur = (qpos // NSA_BLOCK)[:, None]
    valid = blk[None, :] <= cur
    forced = (blk[None, :] == 0) | (blk[None, :] == cur) | (blk[None, :] == cur - 1)
    score = jnp.where(forced, jnp.inf, jnp.where(valid, p_c.sum(axis=2), -jnp.inf))
    n_sel = min(NSA_TOPK, nb)
    top_v, top_i = lax.top_k(score, n_sel)
    sel_ok = top_v > -jnp.inf

    qb = _q_block(S)
    nq = S // qb
    ksb = jnp.moveaxis(blocks(ks), 3, 1)
    vsb = jnp.moveaxis(blocks(vs), 3, 1)
    n_idx = jnp.arange(N)[:, None, None, None]
    g_idx = jnp.arange(NSA_GROUPS)[None, :, None, None]
    table_g = jnp.moveaxis(rel_table.reshape(REL_BUCKETS, NSA_GROUPS, NSA_HPG), 1, 0)
    k_len = n_sel * NSA_BLOCK

    def sel_block(args):
        qi, ti, oki, qp = args
        kk = ksb[n_idx, g_idx, ti]
        vv = vsb[n_idx, g_idx, ti]
        dist = qp[None, None, :, None, None] - (ti[..., None] * NSA_BLOCK + jnp.arange(NSA_BLOCK))
        bias = jnp.moveaxis(table_g[g_idx[..., None], _rel_bucket(dist)], -1, 2).astype(f32)
        s = jnp.einsum('nqghd,ngqjbd->nghqjb', qi, kk, preferred_element_type=f32) * scale + bias
        mask = (oki[..., None] & (dist >= 0))[:, :, None]
        p = _masked_softmax(s.reshape(s.shape[:4] + (k_len,)), mask.reshape(mask.shape[:4] + (k_len,)))
        return jnp.einsum('nghqk,ngqkd->nqghd', p.astype(vv.dtype), vv.reshape(vv.shape[:3] + (k_len, HEAD_DIM)))

    o_s = _from_blocks(lax.map(sel_block, (_to_blocks(qg, 1, nq), _to_blocks(top_i, 2, nq), _to_blocks(sel_ok, 2, nq), qpos.reshape(nq, qb))), 1)

    pad_w = ((0, 0), (NSA_WINDOW, 0), (0, 0), (0, 0))
    kwp = jnp.pad(kw, pad_w)
    vwp = jnp.pad(vw, pad_w)
    span = NSA_WINDOW + qb

    def win_block(args):
        qi, qp, i = args
        start = qpos0 - w0 + i * qb
        kk = lax.dynamic_slice_in_dim(kwp, start, span, axis=1)
        vv = lax.dynamic_slice_in_dim(vwp, start, span, axis=1)
        pidx = start + jnp.arange(span)
        dist = qp[:, None] - (pidx - NSA_WINDOW + w0)[None, :]
        mask = (pidx >= NSA_WINDOW)[None, :] & (dist >= 0) & (dist <= NSA_WINDOW)
        s = jnp.einsum('nqghd,nkgd->nghqk', qi, kk, preferred_element_type=f32) * scale + _rel_bias_shared(rel_table, dist)
        p = _masked_softmax(s, mask)
        return jnp.einsum('nghqk,nkgd->nqghd', p.astype(vv.dtype), vv)

    o_w = _from_blocks(lax.map(win_block, (_to_blocks(qg, 1, nq), qpos.reshape(nq, qb), jnp.arange(nq))), 1)

    g = jax.nn.sigmoid(gates.astype(f32)).reshape(N, S, NSA_GROUPS, NSA_HPG, 3)
    o = g[..., 0:1] * o_c + g[..., 1:2] * o_s + g[..., 2:3] * o_w
    return o.reshape(N, S, NSA_W).astype(q.dtype)


def _proj_a(x, w_in, b_f):
    N, S, _ = x.shape
    qf, kf, vf, fl, qn, kc, vc, ks, vs, kw, vw, gt = _split(x @ w_in, IN_A_SIZES)
    heads = lambda a: a.reshape(N, S, -1, HEAD_DIM)
    logf = jax.nn.log_sigmoid((fl + b_f).astype(jnp.float32))
    return (heads(qf), heads(kf), heads(vf), logf, heads(qn), heads(kc), heads(vc),
            heads(ks), heads(vs), heads(kw), heads(vw), gt.reshape(N, S, NSA_HEADS, 3))


def _attn_prompt(x, w_in, b_f, w_out, rel_table):
    N, S, _ = x.shape
    qf, kf, vf, logf, qn, kc, vc, ks, vs, kw, vw, gt = _proj_a(x, w_in, b_f)
    c = jnp.cumsum(logf, axis=1)
    o_fox = _fox_attention(qf, kf, vf, c, c, 0)
    o_nsa = _nsa_attention(qn, 0, kc, vc, ks, vs, kw, vw, 0, gt, rel_table)
    y = jnp.concatenate([o_fox.reshape(N, S, FOX_W), o_nsa], axis=-1) @ w_out
    wb = min(NSA_WINDOW, S)
    return y, (kf, vf, logf, kc, vc, ks, vs, kw[:, S - wb:], vw[:, S - wb:])


def _attn_sample(x, layer, fox_k, fox_v, fox_logf, cmp_k, cmp_v, sel_k, sel_v, win_k, win_v,
                 page_table, w_in, b_f, w_out, rel_table):
    N, S, _ = x.shape
    past = page_table.shape[1] * PAGE_SIZE
    qf, kf, vf, logf, qn, kc, vc, ks, vs, kw, vw, gt = _proj_a(x, w_in, b_f)
    full = lambda pool, new: jnp.concatenate([_gather_pages(pool, layer, page_table), new], axis=1)
    c = jnp.cumsum(full(fox_logf, logf).astype(jnp.float32), axis=1)
    o_fox = _fox_attention(qf, full(fox_k, kf), full(fox_v, vf), c[:, past:], c, past)
    wb = win_k.shape[2]
    kw_all = jnp.concatenate([win_k[layer], kw], axis=1)
    vw_all = jnp.concatenate([win_v[layer], vw], axis=1)
    o_nsa = _nsa_attention(qn, past, full(cmp_k, kc), full(cmp_v, vc), full(sel_k, ks), full(sel_v, vs),
                           kw_all, vw_all, past - wb, gt, rel_table)
    y = jnp.concatenate([o_fox.reshape(N, S, FOX_W), o_nsa], axis=-1) @ w_out
    return y, (kf, vf, logf, kc, vc, ks, vs, kw, vw)


def _gla_recurrence(q, k, v, log_a, s0):
    N, S, H, DK = q.shape
    C = GLA_CHUNK if S % GLA_CHUNK == 0 else S
    nc = S // C

    def chunks(a):
        a = a.astype(jnp.float32).reshape(N, nc, C, H, a.shape[-1])
        return jnp.swapaxes(jnp.moveaxis(a, 1, 0), 2, 3)

    causal = jnp.tril(jnp.ones((C, C), bool))

    def step(state, inp):
        qc, kc, vc, ac = inp
        b = jnp.cumsum(ac, axis=2)
        o_inter = jnp.einsum('nhcd,nhde->nhce', qc * jnp.exp(b), state)
        diff = jnp.where(causal[:, :, None], b[:, :, :, None, :] - b[:, :, None, :, :], -jnp.inf)
        att = jnp.einsum('nhid,nhjd,nhijd->nhij', qc, kc, jnp.exp(diff))
        o_intra = jnp.einsum('nhij,nhje->nhie', att, vc)
        b_last = b[:, :, -1:, :]
        k_dec = kc * jnp.exp(b_last - b)
        state = jnp.exp(b_last[:, :, 0, :, None]) * state + jnp.einsum('nhcd,nhce->nhde', k_dec, vc)
        return state, o_inter + o_intra

    s_fin, o = lax.scan(step, s0.astype(jnp.float32), (chunks(q), chunks(k), chunks(v), chunks(log_a)))
    o = jnp.moveaxis(jnp.swapaxes(o, 2, 3), 0, 1).reshape(N, S, H, v.shape[-1])
    return o, s_fin


def _gla_mixer(x, s0, w_in, w_g2, b_g, g_norm, w_out):
    N, S, _ = x.shape
    q, k, v, g_lr, r = _split(x @ w_in, IN_C_SIZES)
    log_a = jax.nn.log_sigmoid((g_lr @ w_g2 + b_g).astype(jnp.float32)) / GLA_TAU
    hk = lambda a: a.reshape(N, S, GLA_HEADS, -1)
    o, s_new = _gla_recurrence(hk(q) * GLA_DK ** -0.5, hk(k), hk(v), hk(log_a), s0)
    o = o * lax.rsqrt(jnp.mean(o * o, axis=-1, keepdims=True) + LN_EPS) * g_norm
    y = (o.reshape(N, S, GLA_HEADS * GLA_DV) * jax.nn.silu(r.astype(jnp.float32))).astype(x.dtype) @ w_out
    return y, s_new


def _moe(x, w_r, b_r, w_gu, b_gu, w_d, b_d):
    N, S, D = x.shape
    t = x.reshape(N * S, D)
    logits = (t @ w_r + b_r).astype(jnp.float32)
    top_v, top_i = lax.top_k(logits, TOP_K)
    gate = jax.nn.softmax(top_v, axis=-1)
    gate_dense = jnp.einsum('tk,tke->te', gate, jax.nn.one_hot(top_i, N_EXPERTS, dtype=jnp.float32))
    out = jnp.zeros((N * S, D), jnp.float32)
    for e in range(N_EXPERTS):
        h = t @ w_gu[e] + b_gu[e]
        g = jnp.minimum(h[:, :D_EXPERT], SWIGLU_LIMIT)
        u = jnp.clip(h[:, D_EXPERT:], -SWIGLU_LIMIT, SWIGLU_LIMIT)
        a = (u + 1.0) * g * jax.nn.sigmoid(SWIGLU_ALPHA * g)
        out = out + gate_dense[:, e:e + 1] * (a @ w_d[e] + b_d[e])
    return out.astype(x.dtype).reshape(N, S, D)


def setup_inputs(seed: int = 0) -> dict:
    key = jax.random.key(seed)
    ks = jax.random.split(key, 32)
    f32 = jnp.float32

    def nrm(k, shape, s=1.0):
        return jax.random.normal(k, shape, f32) * s

    n_pages = PAST_LEN // PAGE_SIZE
    n_used = DEC_BATCH * n_pages
    n_pool = n_used + max(n_used // 4, 1)
    wb = min(NSA_WINDOW, PAST_LEN)
    na, nc, beta = N_ATTN_LAYERS, N_GLA_LAYERS, DEEPNORM_BETA
    d_in = D_MODEL ** -0.5
    scales_a = (1.0, 1.0, beta, 0.1, 1.0, 1.0, beta, 1.0, beta, 1.0, beta, 1.0)
    scales_c = (1.0, 1.0, beta, 1.0, 1.0)
    col = lambda sizes, scales: jnp.asarray(np.concatenate([np.full(n, s, np.float32) for n, s in zip(sizes, scales)]))
    page_table = jax.random.permutation(ks[0], n_pool)[:n_used].reshape(DEC_BATCH, n_pages).astype(jnp.int32)
    pool_shape = (na, n_pool, PAGE_SIZE)
    fox_kv = pool_shape + (FOX_HEADS, HEAD_DIM)
    nsa_kv = pool_shape + (NSA_GROUPS, HEAD_DIM)
    win = (na, DEC_BATCH, wb, NSA_GROUPS, HEAD_DIM)
    return {
        'x_prompt': nrm(ks[1], (BATCH, SEQ, D_MODEL)),
        'x_sample': nrm(ks[2], (DEC_BATCH, DEC_SEQ, D_MODEL)),
        'cache_fox_k': nrm(ks[3], fox_kv),
        'cache_fox_v': nrm(ks[4], fox_kv),
        'cache_fox_logf': jax.nn.log_sigmoid(FORGET_BIAS_INIT + nrm(ks[5], pool_shape + (FOX_HEADS,), 0.5)),
        'cache_nsa_cmp_k': nrm(ks[6], nsa_kv),
        'cache_nsa_cmp_v': nrm(ks[7], nsa_kv),
        'cache_nsa_sel_k': nrm(ks[8], nsa_kv),
        'cache_nsa_sel_v': nrm(ks[9], nsa_kv),
        'cache_nsa_win_k': nrm(ks[10], win),
        'cache_nsa_win_v': nrm(ks[11], win),
        'state_gla': nrm(ks[12], (nc, DEC_BATCH, GLA_HEADS, GLA_DK, GLA_DV), 0.5),
        'page_table': page_table,
        'w_in_a': nrm(ks[13], (na, D_MODEL, sum(IN_A_SIZES)), d_in) * col(IN_A_SIZES, scales_a),
        'b_forget': FORGET_BIAS_INIT + nrm(ks[14], (na, FOX_HEADS), 0.5),
        'w_out_a': nrm(ks[15], (na, D_MODEL, D_MODEL), d_in * beta),
        'rel_bias_table': nrm(ks[16], (REL_BUCKETS, NSA_HEADS), 0.5),
        'w_in_c': nrm(ks[17], (nc, D_MODEL, sum(IN_C_SIZES)), d_in) * col(IN_C_SIZES, scales_c),
        'w_gla_gate2': nrm(ks[18], (nc, GLA_GATE_RANK, GLA_HEADS * GLA_DK), GLA_GATE_RANK ** -0.5),
        'b_gla_gate': nrm(ks[19], (nc, GLA_HEADS * GLA_DK), 0.1),
        'gla_norm_g': 1.0 + nrm(ks[20], (nc, GLA_DV), 0.02),
        'w_out_c': nrm(ks[21], (nc, D_MODEL, D_MODEL), d_in * beta),
        'ln_g': 1.0 + nrm(ks[22], (DEPTH, 2, D_MODEL), 0.02),
        'ln_b': nrm(ks[23], (DEPTH, 2, D_MODEL), 0.02),
        'router_w': nrm(ks[24], (DEPTH, D_MODEL, N_EXPERTS), d_in),
        'router_b': nrm(ks[25], (DEPTH, N_EXPERTS), 0.01),
        'w_gate_up': nrm(ks[26], (DEPTH, N_EXPERTS, D_MODEL, 2 * D_EXPERT), d_in * beta),
        'b_gate_up': nrm(ks[27], (DEPTH, N_EXPERTS, 2 * D_EXPERT), 0.01),
        'w_down': nrm(ks[28], (DEPTH, N_EXPERTS, D_EXPERT, D_MODEL), D_EXPERT ** -0.5 * beta),
        'b_down': nrm(ks[29], (DEPTH, N_EXPERTS, D_MODEL), 0.01),
    }


def reference(x_prompt, x_sample, cache_fox_k, cache_fox_v, cache_fox_logf, cache_nsa_cmp_k, cache_nsa_cmp_v,
              cache_nsa_sel_k, cache_nsa_sel_v, cache_nsa_win_k, cache_nsa_win_v, state_gla, page_table,
              w_in_a, b_forget, w_out_a, rel_bias_table, w_in_c, w_gla_gate2, b_gla_gate, gla_norm_g, w_out_c,
              ln_g, ln_b, router_w, router_b, w_gate_up, b_gate_up, w_down, b_down):
    xp, xs = x_prompt, x_sample
    attn_p, attn_s, gla_p, gla_s = [], [], [], []
    for layer in range(DEPTH):
        i = layer // 2
        if layer % 2 == 0:
            mp, st_p = _attn_prompt(xp, w_in_a[i], b_forget[i], w_out_a[i], rel_bias_table)
            ms, st_s = _attn_sample(xs, i, cache_fox_k, cache_fox_v, cache_fox_logf, cache_nsa_cmp_k, cache_nsa_cmp_v,
                                    cache_nsa_sel_k, cache_nsa_sel_v, cache_nsa_win_k, cache_nsa_win_v, page_table,
                                    w_in_a[i], b_forget[i], w_out_a[i], rel_bias_table)
            attn_p.append(st_p)
            attn_s.append(st_s)
        else:
            s0 = jnp.zeros((xp.shape[0], GLA_HEADS, GLA_DK, GLA_DV), jnp.float32)
            mp, g_p = _gla_mixer(xp, s0, w_in_c[i], w_gla_gate2[i], b_gla_gate[i], gla_norm_g[i], w_out_c[i])
            ms, g_s = _gla_mixer(xs, state_gla[i], w_in_c[i], w_gla_gate2[i], b_gla_gate[i], gla_norm_g[i], w_out_c[i])
            gla_p.append(g_p)
            gla_s.append(g_s)
        xp = _layernorm(DEEPNORM_ALPHA * xp + mp, ln_g[layer, 0], ln_b[layer, 0])
        xs = _layernorm(DEEPNORM_ALPHA * xs + ms, ln_g[layer, 0], ln_b[layer, 0])
        moe_w = (router_w[layer], router_b[layer], w_gate_up[layer], b_gate_up[layer], w_down[layer], b_down[layer])
        xp = _layernorm(DEEPNORM_ALPHA * xp + _moe(xp, *moe_w), ln_g[layer, 1], ln_b[layer, 1])
        xs = _layernorm(DEEPNORM_ALPHA * xs + _moe(xs, *moe_w), ln_g[layer, 1], ln_b[layer, 1])
    sp = lambda j: jnp.stack([t[j] for t in attn_p])
    ss = lambda j: jnp.stack([t[j] for t in attn_s])
    return (xp, xs, sp(0), ss(0), sp(1), ss(1), sp(2), ss(2), sp(3), ss(3), sp(4), ss(4), sp(5), ss(5),
            sp(6), ss(6), sp(7), ss(7), sp(8), ss(8), jnp.stack(gla_p), jnp.stack(gla_s))
```

```python
import functools
import math

import jax
import jax.numpy as jnp
import numpy as np
from jax import lax
from jax.experimental import pallas as pl
from jax.experimental.pallas import tpu as pltpu

F32 = jnp.float32
BF16 = jnp.bfloat16
HIGHEST = lax.Precision.HIGHEST
NEG_INF = float("-inf")
POS_INF = float("inf")

D_MODEL = 2048
PAGE_SIZE = 128
HEAD_DIM = 128
FOX_HEADS = 8
NSA_HEADS = 8
NSA_GROUPS = 2
NSA_HPG = NSA_HEADS // NSA_GROUPS
NSA_BLOCK = 64
NSA_TOPK = 8
NSA_WINDOW = 512
REL_BUCKETS = 32
REL_MAX_DIST = 1024
GLA_HEADS = 4
GLA_DK = D_MODEL // 2 // GLA_HEADS
GLA_DV = D_MODEL // GLA_HEADS
GLA_GATE_RANK = 16
GLA_TAU = 16.0
GLA_CHUNK = 64
GLA_SUB = 16
N_EXPERTS = 32
TOP_K = 4
SWIGLU_LIMIT = 7.0
SWIGLU_ALPHA = 1.702
LN_EPS = 1e-5
DEPTH = 2
DEEPNORM_ALPHA = (2 * DEPTH) ** 0.25
FOX_W = FOX_HEADS * HEAD_DIM
NSA_W = NSA_HEADS * HEAD_DIM
NSA_KV_W = NSA_GROUPS * HEAD_DIM
ATTN_SCALE = HEAD_DIM ** -0.5

VMEM_LIMIT_BYTES = 56 * 1024 * 1024
LANES = 128

_REL_EXACT = REL_BUCKETS // 2
_REL_THRESHOLDS = tuple(
    (_REL_EXACT * 8) if 2 * j == (REL_BUCKETS - _REL_EXACT) else
    math.ceil(_REL_EXACT * (REL_MAX_DIST / _REL_EXACT) ** (j / (REL_BUCKETS - _REL_EXACT)))
    for j in range(1, REL_BUCKETS - _REL_EXACT))


def _params(n_axes):
    return pltpu.CompilerParams(dimension_semantics=("arbitrary",) * n_axes,
                                vmem_limit_bytes=VMEM_LIMIT_BYTES)


def _iota(shape, dim, dtype=jnp.int32):
    return lax.broadcasted_iota(dtype, shape, dim)


def _shift_div(x, d):
    k = d.bit_length() - 1
    assert d == 1 << k
    return lax.shift_right_logical(x, jnp.full(x.shape, k, jnp.int32))


def _sigmoid(x):
    return 1.0 / (1.0 + jnp.exp(-x))


def _log_sigmoid(x):
    return jnp.minimum(x, 0.0) - jnp.log(1.0 + jnp.exp(-jnp.abs(x)))


def _dot_nt(a, b, precision=None):
    return lax.dot_general(a, b, (((1,), (1,)), ((), ())), precision=precision,
                           preferred_element_type=F32)


def _dot(a, b, precision=None):
    return jnp.dot(a, b, precision=precision, preferred_element_type=F32)


def _masked_softmax_parts(s, mask):
    s = jnp.where(mask, s, NEG_INF)
    m = jnp.max(s, axis=-1, keepdims=True)
    m = jnp.where(m == NEG_INF, 0.0, m)
    e = jnp.exp(s - m)
    return e, jnp.maximum(jnp.sum(e, axis=-1, keepdims=True), 1e-30)


def _online_update(carry, s, v_bf16):
    m, l, acc = carry
    m_new = jnp.maximum(m, jnp.max(s, axis=-1, keepdims=True))
    m_safe = jnp.where(m_new == NEG_INF, 0.0, m_new)
    p = jnp.exp(s - m_safe)
    alpha = jnp.exp(m - m_safe)
    l = alpha * l + jnp.sum(p, axis=-1, keepdims=True)
    acc = alpha * acc + _dot(p.astype(BF16), v_bf16)
    return m_new, l, acc


def _rel_bucket(dist):
    n = jnp.maximum(dist, 0)
    large = jnp.full(n.shape, _REL_EXACT, jnp.int32)
    for t in _REL_THRESHOLDS:
        large = large + jnp.where(n >= t, 1, 0)
    return jnp.where(n < _REL_EXACT, n, large)


def _rel_bias(bucket, table_ref, head):
    out = jnp.zeros(bucket.shape, F32)
    for b in range(REL_BUCKETS):
        out = jnp.where(bucket == b, table_ref[b, head], out)
    return out


def _top_blocks(score, n_sel):
    lane = _iota(score.shape, 1).astype(F32)
    sel = jnp.zeros(score.shape, F32)
    for _ in range(n_sel):
        mx = jnp.max(score, axis=-1, keepdims=True)
        first = jnp.min(jnp.where(score == mx, lane, 1e9), axis=-1, keepdims=True)
        hit = lane == first
        sel = jnp.where(hit & (mx > NEG_INF), 1.0, sel)
        score = jnp.where(hit, NEG_INF, score)
    return sel


def _mm_kernel(x_ref, w_ref, o_ref):
    o_ref[...] = _dot(x_ref[...], w_ref[...])


def _mm(x, w, tm, tn, name):
    m, k = x.shape
    n = w.shape[1]
    assert m % tm == 0 and n % tn == 0
    return pl.pallas_call(
        _mm_kernel, grid=(n // tn, m // tm),
        in_specs=[pl.BlockSpec((tm, k), lambda j, i: (i, 0)),
                  pl.BlockSpec((k, tn), lambda j, i: (0, j))],
        out_specs=pl.BlockSpec((tm, tn), lambda j, i: (i, j)),
        out_shape=jax.ShapeDtypeStruct((m, n), F32),
        compiler_params=_params(2), name=name)(x, w)


def _layernorm_rows(z, g, b):
    mu = jnp.mean(z, axis=-1, keepdims=True)
    zc = z - mu
    var = jnp.mean(zc * zc, axis=-1, keepdims=True)
    return zc * lax.rsqrt(var + LN_EPS) * g + b


def _mm_res_ln_kernel(*refs, n_in):
    a_refs, w_refs = refs[:n_in], refs[n_in:2 * n_in]
    x_ref, g_ref, b_ref, o_ref, o16_ref = refs[2 * n_in:]
    y = _dot(a_refs[0][...].astype(BF16), w_refs[0][...])
    for a_ref, w_ref in zip(a_refs[1:], w_refs[1:]):
        y = y + _dot(a_ref[...].astype(BF16), w_ref[...])
    out = _layernorm_rows(DEEPNORM_ALPHA * x_ref[...] + y, g_ref[...], b_ref[...])
    o_ref[...] = out
    o16_ref[...] = out.astype(BF16)


def _mm_res_ln(acts, weights, x, g, b, tm, name):
    t, d = x.shape
    n_in = len(acts)
    in_specs = [pl.BlockSpec((tm, a.shape[1]), lambda i: (i, 0)) for a in acts]
    in_specs += [pl.BlockSpec(w.shape, lambda i: (0, 0)) for w in weights]
    in_specs += [pl.BlockSpec((tm, d), lambda i: (i, 0)),
                 pl.BlockSpec((1, d), lambda i: (0, 0)), pl.BlockSpec((1, d), lambda i: (0, 0))]
    return pl.pallas_call(
        functools.partial(_mm_res_ln_kernel, n_in=n_in), grid=(t // tm,),
        in_specs=in_specs,
        out_specs=[pl.BlockSpec((tm, d), lambda i: (i, 0))] * 2,
        out_shape=[jax.ShapeDtypeStruct((t, d), F32), jax.ShapeDtypeStruct((t, d), BF16)],
        compiler_params=_params(1), name=name)(*acts, *weights, x, g.reshape(1, d), b.reshape(1, d))


def _gates_a_kernel(x_ref, w_ref, wt_ref, b_ref, bt_ref, logf_ref, logft_ref, sg_ref):
    x = x_ref[...]
    z = _dot(x, w_ref[...], HIGHEST)
    logf_ref[...] = _log_sigmoid(z + b_ref[...])
    sg_ref[...] = _sigmoid(z)
    zt = _dot_nt(wt_ref[...], x, HIGHEST)
    logft_ref[...] = _log_sigmoid(zt + bt_ref[...])[:FOX_HEADS]


def _gates_a(x, w_small, b_small, tm):
    t, d = x.shape
    return pl.pallas_call(
        _gates_a_kernel, grid=(t // tm,),
        in_specs=[pl.BlockSpec((tm, d), lambda i: (i, 0)),
                  pl.BlockSpec((d, LANES), lambda i: (0, 0)),
                  pl.BlockSpec((LANES, d), lambda i: (0, 0)),
                  pl.BlockSpec((1, LANES), lambda i: (0, 0)),
                  pl.BlockSpec((LANES, 1), lambda i: (0, 0))],
        out_specs=[pl.BlockSpec((tm, LANES), lambda i: (i, 0)),
                   pl.BlockSpec((FOX_HEADS, tm), lambda i: (0, i)),
                   pl.BlockSpec((tm, LANES), lambda i: (i, 0))],
        out_shape=[jax.ShapeDtypeStruct((t, LANES), F32), jax.ShapeDtypeStruct((FOX_HEADS, t), F32),
                   jax.ShapeDtypeStruct((t, LANES), F32)],
        compiler_params=_params(1), name="gates_a")(
            x, w_small, w_small.T, b_small.reshape(1, LANES), b_small.reshape(LANES, 1))


def _cumsum_kernel(x_ref, o_ref, *, blk):
    s = x_ref.shape[1]
    upper = jnp.where(_iota((blk, blk), 0) <= _iota((blk, blk), 1), 1.0, 0.0)
    carry = jnp.zeros((x_ref.shape[0], 1), F32)
    for j in range(s // blk):
        c = _dot(x_ref[:, j * blk:(j + 1) * blk], upper, HIGHEST) + carry
        o_ref[:, j * blk:(j + 1) * blk] = c
        carry = c[:, blk - 1:blk]


def _cumsum_seq(xt, n, s):
    h = xt.shape[0]
    return pl.pallas_call(
        functools.partial(_cumsum_kernel, blk=min(256, s)), grid=(n,),
        in_specs=[pl.BlockSpec((h, s), lambda i: (0, i))],
        out_specs=pl.BlockSpec((None, h, s), lambda i: (i, 0, 0)),
        out_shape=jax.ShapeDtypeStruct((n, h, s), F32),
        compiler_params=_params(1), name="fox_cumsum")(xt)


def _fox_prompt_kernel(q_ref, k_ref, v_ref, cq_ref, ck_ref, o_ref, *, tq):
    qi = pl.program_id(2)
    q = q_ref[...].astype(BF16)
    cq = cq_ref[...]
    qpos = qi * tq + _iota((tq, tq), 0)

    def body(kb, carry):
        start = pl.multiple_of(kb * tq, tq)
        k = k_ref[pl.ds(start, tq), :].astype(BF16)
        v = v_ref[pl.ds(start, tq), :].astype(BF16)
        s = _dot_nt(q, k) * ATTN_SCALE + cq - ck_ref[:, pl.ds(start, tq)]
        s = jnp.where(start + _iota((tq, tq), 1) <= qpos, s, NEG_INF)
        return _online_update(carry, s, v)

    init = (jnp.full((tq, 1), NEG_INF, F32), jnp.zeros((tq, 1), F32), jnp.zeros((tq, HEAD_DIM), F32))
    _, l, acc = lax.fori_loop(0, qi + 1, body, init)
    o_ref[...] = acc / jnp.maximum(l, 1e-30)


def _fox_prompt(h, c_col, c_row, n, s, col_q, col_k, col_v, tq):
    nq = s // tq
    kv_spec = lambda col: pl.BlockSpec((s, HEAD_DIM), lambda b, hd, qi: (b, col + hd))
    return pl.pallas_call(
        functools.partial(_fox_prompt_kernel, tq=tq), grid=(n, FOX_HEADS, nq),
        in_specs=[pl.BlockSpec((tq, HEAD_DIM), lambda b, hd, qi: (b * nq + qi, col_q + hd)),
                  kv_spec(col_k), kv_spec(col_v),
                  pl.BlockSpec((None, None, tq, 1), lambda b, hd, qi: (b, hd, qi, 0)),
                  pl.BlockSpec((None, None, 1, s), lambda b, hd, qi: (b, hd, 0, 0))],
        out_specs=pl.BlockSpec((tq, HEAD_DIM), lambda b, hd, qi: (b * nq + qi, hd)),
        out_shape=jax.ShapeDtypeStruct((n * s, FOX_W), F32),
        compiler_params=_params(3), name="fox_prompt")(h, h, h, c_col, c_row)


def _nsa_bias_kernel(table_ref, tiles_ref, cmpb_ref, bk_ref, bw_ref, bc_ref, *, tq, past):
    d = pl.program_id(0)
    dist = d * tq + _iota((tq, tq), 0) - _iota((tq, tq), 1)
    bucket = _rel_bucket(dist)
    nb = cmpb_ref.shape[-1]
    dist_c = jnp.minimum(d, pl.num_programs(0) - 2) * tq + _iota((tq, nb), 0) \
        - (_iota((tq, nb), 1) * NSA_BLOCK + NSA_BLOCK - 1)
    bucket_c = _rel_bucket(dist_c)
    for h in range(NSA_HEADS):
        tiles_ref[h] = _rel_bias(bucket, table_ref, h)
        cmpb_ref[h] = _rel_bias(bucket_c, table_ref, h)

    @pl.when(d == 0)
    def _():
        bucket_k = _rel_bucket(past - _iota((1, bk_ref.shape[1]), 1))
        bucket_w = _rel_bucket(bw_ref.shape[1] - _iota((1, bw_ref.shape[1]), 1))
        bucket_b = _rel_bucket(past - (_iota((1, LANES), 1) * NSA_BLOCK + NSA_BLOCK - 1))
        for h in range(NSA_HEADS):
            bk_ref[h:h + 1, :] = _rel_bias(bucket_k, table_ref, h)
            bw_ref[h:h + 1, :] = _rel_bias(bucket_w, table_ref, h)
            bc_ref[h:h + 1, :] = _rel_bias(bucket_b, table_ref, h)


def _nsa_bias_tables(rel_table, s, tq, past, win):
    nq = s // tq
    nb = LANES
    assert s // NSA_BLOCK <= LANES
    return pl.pallas_call(
        functools.partial(_nsa_bias_kernel, tq=tq, past=past),
        grid=(nq + 1,),
        in_specs=[pl.BlockSpec(memory_space=pltpu.SMEM)],
        out_specs=[pl.BlockSpec((None, NSA_HEADS, tq, tq), lambda d: (d, 0, 0, 0)),
                   pl.BlockSpec((None, NSA_HEADS, tq, nb), lambda d: (jnp.minimum(d, nq - 1), 0, 0, 0)),
                   pl.BlockSpec((NSA_HEADS, past), lambda d: (0, 0)),
                   pl.BlockSpec((NSA_HEADS, win), lambda d: (0, 0)),
                   pl.BlockSpec((NSA_HEADS, LANES), lambda d: (0, 0))],
        out_shape=[jax.ShapeDtypeStruct((nq + 1, NSA_HEADS, tq, tq), F32),
                   jax.ShapeDtypeStruct((nq, NSA_HEADS, tq, nb), F32),
                   jax.ShapeDtypeStruct((NSA_HEADS, past), F32),
                   jax.ShapeDtypeStruct((NSA_HEADS, win), F32),
                   jax.ShapeDtypeStruct((NSA_HEADS, LANES), F32)],
        compiler_params=_params(1), name="nsa_bias")(rel_table)


def _nsa_prompt_kernel(q_ref, kc_ref, vc_ref, ks_ref, vs_ref, kw_ref, vw_ref, gate_ref, tiles_ref,
                       cmpb_ref, o_ref, *, tq):
    qi = pl.program_id(2)
    s_len = kc_ref.shape[0]
    n_blk = s_len // NSA_BLOCK
    nb = LANES
    rows = NSA_HPG * tq
    q = jnp.concatenate([q_ref[:, h * HEAD_DIM:(h + 1) * HEAD_DIM] for h in range(NSA_HPG)],
                        axis=0).astype(BF16)
    stack = lambda ref, idx: jnp.concatenate([ref[idx, h] for h in range(NSA_HPG)], axis=0)

    def block_means(ref):
        m = jnp.mean(ref[...].reshape(n_blk, NSA_BLOCK, HEAD_DIM), axis=1)
        return jnp.concatenate([m, jnp.zeros((nb - n_blk, HEAD_DIM), F32)], axis=0).astype(BF16)

    qpos_r = qi * tq + (_iota((rows, nb), 0) & (tq - 1))
    blk_r = _iota((rows, nb), 1)
    bias_c = jnp.concatenate([cmpb_ref[h] for h in range(NSA_HPG)], axis=0)
    s_c = _dot_nt(q, block_means(kc_ref)) * ATTN_SCALE + bias_c
    e_c, den_c = _masked_softmax_parts(
        s_c, (blk_r < n_blk) & (qpos_r >= blk_r * NSA_BLOCK + NSA_BLOCK - 1))
    p_c = e_c / den_c
    o_c = _dot(p_c.astype(BF16), block_means(vc_ref))

    p_sum = p_c[0:tq]
    for h in range(1, NSA_HPG):
        p_sum = p_sum + p_c[h * tq:(h + 1) * tq]
    blk = _iota((tq, nb), 1)
    cur = _shift_div(qi * tq + _iota((tq, nb), 0), NSA_BLOCK)
    forced = (blk == 0) | (blk == cur) | (blk == cur - 1)
    score = jnp.where(forced, POS_INF, jnp.where(blk <= cur, p_sum, NEG_INF))
    sel = _top_blocks(score, min(NSA_TOPK, n_blk)).astype(BF16)

    row_q = _iota((rows, tq), 0) & (tq - 1)
    col_k = _iota((rows, tq), 1)
    init = (jnp.full((rows, 1), NEG_INF, F32), jnp.zeros((rows, 1), F32), jnp.zeros((rows, HEAD_DIM), F32))

    def sel_body(kt, carry):
        start = pl.multiple_of(kt * tq, tq)
        delta = qi - kt
        k = ks_ref[pl.ds(start, tq), :].astype(BF16)
        v = vs_ref[pl.ds(start, tq), :].astype(BF16)
        s = _dot_nt(q, k) * ATTN_SCALE + stack(tiles_ref, delta)
        expand = jnp.where(_iota((nb, tq), 0) == _shift_div(start + _iota((nb, tq), 1), NSA_BLOCK), 1.0, 0.0)
        picked = _dot(sel, expand.astype(BF16))
        picked = jnp.concatenate([picked] * NSA_HPG, axis=0)
        dist = delta * tq + row_q - col_k
        s = jnp.where((picked > 0.5) & (dist >= 0), s, NEG_INF)
        return _online_update(carry, s, v)

    _, l_s, acc_s = lax.fori_loop(0, qi + 1, sel_body, init)
    o_s = acc_s / jnp.maximum(l_s, 1e-30)

    def win_body(kt, carry):
        start = pl.multiple_of(kt * tq, tq)
        delta = qi - kt
        k = kw_ref[pl.ds(start, tq), :].astype(BF16)
        v = vw_ref[pl.ds(start, tq), :].astype(BF16)
        s = _dot_nt(q, k) * ATTN_SCALE + stack(tiles_ref, delta)
        dist = delta * tq + row_q - col_k
        s = jnp.where((dist >= 0) & (dist <= NSA_WINDOW), s, NEG_INF)
        return _online_update(carry, s, v)

    first = jnp.maximum(qi - (NSA_WINDOW + tq - 1) // tq, 0)
    _, l_w, acc_w = lax.fori_loop(first, qi + 1, win_body, init)
    o_w = acc_w / jnp.maximum(l_w, 1e-30)

    gate = gate_ref[...]
    for h in range(NSA_HPG):
        r = slice(h * tq, (h + 1) * tq)
        o_ref[:, h * HEAD_DIM:(h + 1) * HEAD_DIM] = (
            gate[:, 3 * h:3 * h + 1] * o_c[r] + gate[:, 3 * h + 1:3 * h + 2] * o_s[r]
            + gate[:, 3 * h + 2:3 * h + 3] * o_w[r])


def _nsa_prompt(h, gates, tiles, cmpb, n, s, col_q, col_kv, tq):
    nq = s // tq
    grp_w = NSA_HPG * HEAD_DIM
    kv_spec = lambda seg: pl.BlockSpec(
        (s, HEAD_DIM), lambda b, g, qi: (b, col_kv + seg * NSA_GROUPS + g))
    return pl.pallas_call(
        functools.partial(_nsa_prompt_kernel, tq=tq), grid=(n, NSA_GROUPS, nq),
        in_specs=[pl.BlockSpec((tq, grp_w), lambda b, g, qi: (b * nq + qi, col_q + g))]
        + [kv_spec(seg) for seg in range(6)]
        + [pl.BlockSpec((None, tq, 3 * NSA_HPG), lambda b, g, qi: (g, b * nq + qi, 0)),
           pl.BlockSpec((nq + 1, NSA_HPG, tq, tq), lambda b, g, qi: (0, g, 0, 0)),
           pl.BlockSpec((None, NSA_HPG, tq, LANES), lambda b, g, qi: (qi, g, 0, 0))],
        out_specs=pl.BlockSpec((tq, grp_w), lambda b, g, qi: (b * nq + qi, g)),
        out_shape=jax.ShapeDtypeStruct((n * s, NSA_W), F32),
        compiler_params=_params(3), name="nsa_prompt")(h, h, h, h, h, h, h, gates, tiles, cmpb)


def _head_mask(width):
    return _shift_div(_iota((FOX_HEADS, width), 1), HEAD_DIM) == _iota((FOX_HEADS, width), 0)


def _group_mask(width):
    return (_shift_div(_iota((NSA_HEADS, width), 1), HEAD_DIM)
            == _shift_div(_iota((NSA_HEADS, width), 0), NSA_HPG))


def _fox_sample_kernel(pt_ref, q_ref, kn_ref, vn_ref, lfn_ref, k_ref, v_ref, lf_ref, o_ref,
                       m_scr, l_scr, acc_scr, suf_scr):
    p = pl.program_id(1)
    width = q_ref.shape[-1]
    own = _head_mask(width)
    qbd = jnp.where(own, q_ref[...], 0.0)

    @pl.when(p == 0)
    def _():
        m_scr[...] = jnp.sum(qbd * kn_ref[...], axis=-1, keepdims=True) * ATTN_SCALE
        l_scr[...] = jnp.ones(l_scr.shape, F32)
        acc_scr[...] = jnp.broadcast_to(vn_ref[...].astype(BF16).astype(F32), acc_scr.shape)
        suf_scr[...] = lfn_ref[...]

    lf = lf_ref[...]
    later = jnp.where(_iota((PAGE_SIZE, PAGE_SIZE), 0) > _iota((PAGE_SIZE, PAGE_SIZE), 1), 1.0, 0.0)
    decay = _dot(lf, later, HIGHEST) + suf_scr[...]
    s = _dot_nt(qbd.astype(BF16), k_ref[...].astype(BF16)) * ATTN_SCALE + decay
    m, l, acc = _online_update((m_scr[...], l_scr[...], acc_scr[...]), s, v_ref[...].astype(BF16))
    m_scr[...] = m
    l_scr[...] = l
    acc_scr[...] = acc
    suf_scr[...] = suf_scr[...] + jnp.sum(lf, axis=-1, keepdims=True)

    @pl.when(p == pl.num_programs(1) - 1)
    def _():
        o = jnp.where(own, acc_scr[...] / jnp.maximum(l_scr[...], 1e-30), 0.0)
        o_ref[...] = jnp.sum(o, axis=0, keepdims=True)


def _fox_sample(page_table, layer, q, k_new, v_new, logf_new, cache_k, cache_v, cache_logf_t):
    nb, n_pages = page_table.shape
    row = pl.BlockSpec((None, 1, FOX_W), lambda b, p, pt: (b, 0, 0))
    page = lambda b, p, pt: (layer, pt[b, n_pages - 1 - p], 0, 0)
    grid_spec = pltpu.PrefetchScalarGridSpec(
        num_scalar_prefetch=1, grid=(nb, n_pages),
        in_specs=[row, row, row,
                  pl.BlockSpec((None, FOX_HEADS, 1), lambda b, p, pt: (b, 0, 0)),
                  pl.BlockSpec((None, None, PAGE_SIZE, FOX_W), page),
                  pl.BlockSpec((None, None, PAGE_SIZE, FOX_W), page),
                  pl.BlockSpec((None, None, FOX_HEADS, PAGE_SIZE), page)],
        out_specs=row,
        scratch_shapes=[pltpu.VMEM((FOX_HEADS, 1), F32), pltpu.VMEM((FOX_HEADS, 1), F32),
                        pltpu.VMEM((FOX_HEADS, FOX_W), F32), pltpu.VMEM((FOX_HEADS, 1), F32)])
    return pl.pallas_call(
        _fox_sample_kernel, grid_spec=grid_spec,
        out_shape=jax.ShapeDtypeStruct((nb, 1, FOX_W), F32),
        compiler_params=_params(2), name="fox_sample")(
            page_table, q, k_new, v_new, logf_new, cache_k, cache_v, cache_logf_t)


def _nsa_sample_cmp_kernel(pt_ref, q_ref, k_ref, v_ref, bc_ref, oc_ref, sel_ref, kcb_scr, vcb_scr,
                           *, past):
    p = pl.program_id(1)
    n_blk = past // NSA_BLOCK
    per_page = PAGE_SIZE // NSA_BLOCK

    @pl.when(p == 0)
    def _():
        kcb_scr[...] = jnp.zeros(kcb_scr.shape, F32)
        vcb_scr[...] = jnp.zeros(vcb_scr.shape, F32)

    pick = jnp.where(
        _iota((LANES, PAGE_SIZE), 0) == p * per_page + _shift_div(_iota((LANES, PAGE_SIZE), 1), NSA_BLOCK),
        1.0 / NSA_BLOCK, 0.0)
    kcb_scr[...] += _dot(pick, k_ref[...], HIGHEST)
    vcb_scr[...] += _dot(pick, v_ref[...], HIGHEST)

    @pl.when(p == pl.num_programs(1) - 1)
    def _():
        grp = _group_mask(NSA_KV_W)
        qbd = jnp.where(grp, jnp.concatenate([q_ref[...]] * NSA_GROUPS, axis=1), 0.0)
        blk = _iota((NSA_HEADS, LANES), 1)
        s_c = _dot_nt(qbd.astype(BF16), kcb_scr[...].astype(BF16)) * ATTN_SCALE + bc_ref[...]
        e_c, den_c = _masked_softmax_parts(s_c, (blk < n_blk) & (past >= blk * NSA_BLOCK + NSA_BLOCK - 1))
        p_c = e_c / den_c
        o_all = _dot(p_c.astype(BF16), vcb_scr[...].astype(BF16))
        o_grp = jnp.where(grp, o_all, 0.0)
        o_c = o_grp[:, :HEAD_DIM]
        for g in range(1, NSA_GROUPS):
            o_c = o_c + o_grp[:, g * HEAD_DIM:(g + 1) * HEAD_DIM]
        oc_ref[...] = o_c
        head_grp = _shift_div(_iota((NSA_HEADS, LANES), 0), NSA_HPG)
        p_sum = jnp.zeros((NSA_HEADS, LANES), F32)
        for g in range(NSA_GROUPS):
            tot = jnp.sum(p_c[g * NSA_HPG:(g + 1) * NSA_HPG], axis=0, keepdims=True)
            p_sum = jnp.where(head_grp == g, tot, p_sum)
        cur = past // NSA_BLOCK
        forced = (blk == 0) | (blk == cur) | (blk == cur - 1)
        score = jnp.where(forced, POS_INF, jnp.where(blk <= cur, p_sum, NEG_INF))
        sel_ref[...] = _top_blocks(score, min(NSA_TOPK, cur + 1))


def _nsa_sample_cmp(page_table, layer, q, cache_k, cache_v, bc, past):
    assert past % NSA_BLOCK == 0 and past // NSA_BLOCK < LANES
    nb, n_pages = page_table.shape
    page = lambda b, p, pt: (layer, pt[b, p], 0, 0)
    grid_spec = pltpu.PrefetchScalarGridSpec(
        num_scalar_prefetch=1, grid=(nb, n_pages),
        in_specs=[pl.BlockSpec((None, NSA_HEADS, HEAD_DIM), lambda b, p, pt: (b, 0, 0)),
                  pl.BlockSpec((None, None, PAGE_SIZE, NSA_KV_W), page),
                  pl.BlockSpec((None, None, PAGE_SIZE, NSA_KV_W), page),
                  pl.BlockSpec((NSA_HEADS, LANES), lambda b, p, pt: (0, 0))],
        out_specs=[pl.BlockSpec((None, NSA_HEADS, HEAD_DIM), lambda b, p, pt: (b, 0, 0)),
                   pl.BlockSpec((None, NSA_HEADS, LANES), lambda b, p, pt: (b, 0, 0))],
        scratch_shapes=[pltpu.VMEM((LANES, NSA_KV_W), F32), pltpu.VMEM((LANES, NSA_KV_W), F32)])
    return pl.pallas_call(
        functools.partial(_nsa_sample_cmp_kernel, past=past), grid_spec=grid_spec,
        out_shape=[jax.ShapeDtypeStruct((nb, NSA_HEADS, HEAD_DIM), F32),
                   jax.ShapeDtypeStruct((nb, NSA_HEADS, LANES), F32)],
        compiler_params=_params(2), name="nsa_sample_cmp")(page_table, q, cache_k, cache_v, bc)


def _nsa_sample_sel_kernel(pt_ref, q_ref, ksn_ref, vsn_ref, kwn_ref, vwn_ref, sel_ref, oc_ref, gate_ref,
                           ks_ref, vs_ref, kw_ref, vw_ref, bk_ref, bw_ref, b0_ref, o_ref,
                           m_scr, l_scr, acc_scr):
    p = pl.program_id(1)
    per_page = PAGE_SIZE // NSA_BLOCK
    grp = _group_mask(NSA_KV_W)
    qbd = jnp.where(grp, jnp.concatenate([q_ref[...]] * NSA_GROUPS, axis=1), 0.0)
    qb16 = qbd.astype(BF16)
    b0 = b0_ref[...]

    @pl.when(p == 0)
    def _():
        m_scr[...] = jnp.sum(qbd * ksn_ref[...], axis=-1, keepdims=True) * ATTN_SCALE + b0
        l_scr[...] = jnp.ones(l_scr.shape, F32)
        acc_scr[...] = jnp.broadcast_to(vsn_ref[...].astype(BF16).astype(F32), acc_scr.shape)

    expand = jnp.where(
        _iota((LANES, PAGE_SIZE), 0) == p * per_page + _shift_div(_iota((LANES, PAGE_SIZE), 1), NSA_BLOCK),
        1.0, 0.0)
    picked = _dot(sel_ref[...].astype(BF16), expand.astype(BF16))
    start = pl.multiple_of(p * PAGE_SIZE, PAGE_SIZE)
    s = _dot_nt(qb16, ks_ref[...].astype(BF16)) * ATTN_SCALE + bk_ref[:, pl.ds(start, PAGE_SIZE)]
    s = jnp.where(picked > 0.5, s, NEG_INF)
    m, l, acc = _online_update((m_scr[...], l_scr[...], acc_scr[...]), s, vs_ref[...].astype(BF16))
    m_scr[...] = m
    l_scr[...] = l
    acc_scr[...] = acc

    @pl.when(p == pl.num_programs(1) - 1)
    def _():
        fold = lambda a: sum(jnp.where(grp, a, 0.0)[:, g * HEAD_DIM:(g + 1) * HEAD_DIM] for g in range(NSA_GROUPS))
        o_s = fold(acc_scr[...] / jnp.maximum(l_scr[...], 1e-30))
        s_w = _dot_nt(qb16, kw_ref[...].astype(BF16)) * ATTN_SCALE + bw_ref[...]
        s_n = jnp.sum(qbd * kwn_ref[...], axis=-1, keepdims=True) * ATTN_SCALE + b0
        m_w = jnp.maximum(jnp.max(s_w, axis=-1, keepdims=True), s_n)
        e_w = jnp.exp(s_w - m_w)
        e_n = jnp.exp(s_n - m_w)
        den = jnp.maximum(jnp.sum(e_w, axis=-1, keepdims=True) + e_n, 1e-30)
        p_w = (e_w / den).astype(BF16)
        p_n = (e_n / den).astype(BF16).astype(F32)
        o_w = fold(_dot(p_w, vw_ref[...].astype(BF16)) + p_n * vwn_ref[...].astype(BF16).astype(F32))
        gate = gate_ref[...]
        o_ref[...] = gate[:, 0:1] * oc_ref[...] + gate[:, 1:2] * o_s + gate[:, 2:3] * o_w


def _nsa_sample_sel(page_table, layer, q, ks_new, vs_new, kw_new, vw_new, sel, o_c, gates,
                    cache_ks, cache_vs, win_k, win_v, bk, bw, b0):
    nb, n_pages = page_table.shape
    win = win_k.shape[2]
    past = n_pages * PAGE_SIZE
    per_b = lambda shape: pl.BlockSpec((None,) + shape, lambda b, p, pt: (b, 0, 0))
    page = lambda b, p, pt: (layer, pt[b, p], 0, 0)
    const = lambda shape: pl.BlockSpec(shape, lambda b, p, pt: (0, 0))
    grid_spec = pltpu.PrefetchScalarGridSpec(
        num_scalar_prefetch=1, grid=(nb, n_pages),
        in_specs=[per_b((NSA_HEADS, HEAD_DIM)),
                  per_b((1, NSA_KV_W)), per_b((1, NSA_KV_W)), per_b((1, NSA_KV_W)), per_b((1, NSA_KV_W)),
                  per_b((NSA_HEADS, LANES)), per_b((NSA_HEADS, HEAD_DIM)), per_b((NSA_HEADS, 3)),
                  pl.BlockSpec((None, None, PAGE_SIZE, NSA_KV_W), page),
                  pl.BlockSpec((None, None, PAGE_SIZE, NSA_KV_W), page),
                  pl.BlockSpec((None, None, win, NSA_KV_W), lambda b, p, pt: (layer, b, 0, 0)),
                  pl.BlockSpec((None, None, win, NSA_KV_W), lambda b, p, pt: (layer, b, 0, 0)),
                  const((NSA_HEADS, past)), const((NSA_HEADS, win)), const((NSA_HEADS, 1))],
        out_specs=per_b((NSA_HEADS, HEAD_DIM)),
        scratch_shapes=[pltpu.VMEM((NSA_HEADS, 1), F32), pltpu.VMEM((NSA_HEADS, 1), F32),
                        pltpu.VMEM((NSA_HEADS, NSA_KV_W), F32)])
    return pl.pallas_call(
        _nsa_sample_sel_kernel, grid_spec=grid_spec,
        out_shape=jax.ShapeDtypeStruct((nb, NSA_HEADS, HEAD_DIM), F32),
        compiler_params=_params(2), name="nsa_sample_sel")(
            page_table, q, ks_new, vs_new, kw_new, vw_new, sel, o_c, gates,
            cache_ks, cache_vs, win_k, win_v, bk, bw, b0)


def _gla_gate_kernel(g_ref, w_ref, b_ref, o_ref):
    z = _dot(g_ref[...].astype(BF16), w_ref[...]) + b_ref[...]
    o_ref[...] = _log_sigmoid(z) * (1.0 / GLA_TAU)


def _gla_gate(hc, col_g, w_g2_pad, b_g, tm):
    t = hc.shape[0]
    n = w_g2_pad.shape[1]
    return pl.pallas_call(
        _gla_gate_kernel, grid=(t // tm,),
        in_specs=[pl.BlockSpec((tm, LANES), lambda i: (i, col_g)),
                  pl.BlockSpec((LANES, n), lambda i: (0, 0)),
                  pl.BlockSpec((1, n), lambda i: (0, 0))],
        out_specs=pl.BlockSpec((tm, n), lambda i: (i, 0)),
        out_shape=jax.ShapeDtypeStruct((t, n), F32),
        compiler_params=_params(1), name="gla_gate")(hc, w_g2_pad, b_g.reshape(1, n))


def _gla_out_gate(o, r, g_norm):
    o = o * lax.rsqrt(jnp.mean(o * o, axis=-1, keepdims=True) + LN_EPS) * g_norm
    return o * (r * _sigmoid(r))


def _gla_prompt_kernel(q_ref, k_ref, v_ref, r_ref, la_ref, gn_ref, o_ref, st_ref, stt_scr):
    c = pl.program_id(2)
    ch = q_ref.shape[0]

    @pl.when(c == 0)
    def _():
        stt_scr[...] = jnp.zeros(stt_scr.shape, F32)

    la = la_ref[...]
    row = _iota((ch, ch), 0)
    col = _iota((ch, ch), 1)
    b = _dot(jnp.where(row >= col, 1.0, 0.0), la, HIGHEST)
    sub_start = (row >> int(math.log2(GLA_SUB))) << int(math.log2(GLA_SUB))
    base = _dot(jnp.where(col < sub_start, 1.0, 0.0), la, HIGHEST)
    q = q_ref[...] * (GLA_DK ** -0.5)
    k = k_ref[...]
    v16 = v_ref[...].astype(BF16)
    stt = stt_scr[...]

    o = _dot_nt((q * jnp.exp(b)).astype(BF16), stt.astype(BF16))
    q_in = (q * jnp.exp(b - base)).astype(BF16)
    krow = _iota((ch, 1), 0)
    att_rows = []
    for i in range(ch // GLA_SUB):
        lo, hi = i * GLA_SUB, (i + 1) * GLA_SUB
        expo = jnp.where(krow < hi, base[lo:lo + 1] - b, NEG_INF)
        k_in = (k * jnp.exp(expo)).astype(BF16)
        att_rows.append(_dot_nt(q_in[lo:hi], k_in))
    att = jnp.where(row >= col, jnp.concatenate(att_rows, axis=0), 0.0)
    o = o + _dot(att.astype(BF16), v16)

    b_last = b[ch - 1:ch]
    k_dec = (k * jnp.exp(b_last - b)).astype(BF16)
    stt = stt * jnp.exp(b_last) + lax.dot_general(v16, k_dec, (((0,), (0,)), ((), ())),
                                                   preferred_element_type=F32)
    stt_scr[...] = stt
    o_ref[...] = _gla_out_gate(o, r_ref[...], gn_ref[...])

    @pl.when(c == pl.num_programs(2) - 1)
    def _():
        st_ref[...] = stt.T


def _gla_prompt(hc, la, g_norm, n, s):
    ch = GLA_CHUNK if s % GLA_CHUNK == 0 else s
    nc = s // ch
    rows = lambda b, h, c: b * nc + c
    qk_blocks = GLA_HEADS
    return pl.pallas_call(
        _gla_prompt_kernel, grid=(n, GLA_HEADS, nc),
        in_specs=[pl.BlockSpec((ch, GLA_DK), lambda b, h, c: (rows(b, h, c), h)),
                  pl.BlockSpec((ch, GLA_DK), lambda b, h, c: (rows(b, h, c), qk_blocks + h)),
                  pl.BlockSpec((ch, GLA_DV), lambda b, h, c: (rows(b, h, c), GLA_HEADS + h)),
                  pl.BlockSpec((ch, GLA_DV), lambda b, h, c: (rows(b, h, c), 2 * GLA_HEADS + h)),
                  pl.BlockSpec((ch, GLA_DK), lambda b, h, c: (rows(b, h, c), h)),
                  pl.BlockSpec((1, GLA_DV), lambda b, h, c: (0, 0))],
        out_specs=[pl.BlockSpec((ch, GLA_DV), lambda b, h, c: (rows(b, h, c), h)),
                   pl.BlockSpec((None, None, GLA_DK, GLA_DV), lambda b, h, c: (b, h, 0, 0))],
        out_shape=[jax.ShapeDtypeStruct((n * s, GLA_HEADS * GLA_DV), F32),
                   jax.ShapeDtypeStruct((n, GLA_HEADS, GLA_DK, GLA_DV), F32)],
        scratch_shapes=[pltpu.VMEM((GLA_DV, GLA_DK), F32)],
        compiler_params=_params(3), name="gla_prompt")(hc, hc, hc, hc, la, g_norm.reshape(1, GLA_DV))


def _gla_sample_kernel(q_ref, k_ref, la_ref, v_ref, r_ref, gn_ref, s0_ref, o_ref, st_ref):
    for h in range(GLA_HEADS):
        st = jnp.exp(la_ref[h]) * s0_ref[h] + k_ref[h] * v_ref[h]
        st_ref[h] = st
        o = jnp.sum((q_ref[h] * (GLA_DK ** -0.5)) * st, axis=0, keepdims=True)
        o_ref[h] = _gla_out_gate(o, r_ref[h], gn_ref[...])


def _gla_sample(q, k, la, v, r, g_norm, s0):
    nb = q.shape[0]
    col = pl.BlockSpec((None, GLA_HEADS, GLA_DK, 1), lambda b: (b, 0, 0, 0))
    rowv = pl.BlockSpec((None, GLA_HEADS, 1, GLA_DV), lambda b: (b, 0, 0, 0))
    full = pl.BlockSpec((None, GLA_HEADS, GLA_DK, GLA_DV), lambda b: (b, 0, 0, 0))
    return pl.pallas_call(
        _gla_sample_kernel, grid=(nb,),
        in_specs=[col, col, col, rowv, rowv, pl.BlockSpec((1, GLA_DV), lambda b: (0, 0)), full],
        out_specs=[rowv, full],
        out_shape=[jax.ShapeDtypeStruct((nb, GLA_HEADS, 1, GLA_DV), F32),
                   jax.ShapeDtypeStruct((nb, GLA_HEADS, GLA_DK, GLA_DV), F32)],
        compiler_params=_params(1), name="gla_sample")(q, k, la, v, r, g_norm.reshape(1, GLA_DV), s0)


def _router_kernel(x_ref, w_ref, b_ref, idx_ref, gate_ref):
    logits = _dot(x_ref[...], w_ref[...], HIGHEST) + b_ref[...]
    lane = _iota(logits.shape, 1).astype(F32)
    logits = jnp.where(lane < N_EXPERTS, logits, NEG_INF)
    top_v = jnp.full(logits.shape, NEG_INF, F32)
    top_i = jnp.zeros(logits.shape, F32)
    for kk in range(TOP_K):
        mx = jnp.max(logits, axis=-1, keepdims=True)
        first = jnp.min(jnp.where(logits == mx, lane, 1e9), axis=-1, keepdims=True)
        top_v = jnp.where(lane == kk, mx, top_v)
        top_i = jnp.where(lane == kk, first, top_i)
        logits = jnp.where(lane == first, NEG_INF, logits)
    e = jnp.exp(top_v - jnp.max(top_v, axis=-1, keepdims=True))
    gate_ref[...] = e / jnp.sum(e, axis=-1, keepdims=True)
    idx_ref[...] = top_i.astype(jnp.int32)


def _router(x, w_pad, b_pad, tm):
    t, d = x.shape
    return pl.pallas_call(
        _router_kernel, grid=(t // tm,),
        in_specs=[pl.BlockSpec((tm, d), lambda i: (i, 0)),
                  pl.BlockSpec((d, LANES), lambda i: (0, 0)),
                  pl.BlockSpec((1, LANES), lambda i: (0, 0))],
        out_specs=[pl.BlockSpec((tm, LANES), lambda i: (i, 0))] * 2,
        out_shape=[jax.ShapeDtypeStruct((t, LANES), jnp.int32), jax.ShapeDtypeStruct((t, LANES), F32)],
        compiler_params=_params(1), name="router")(x, w_pad, b_pad)


def _moe_gate_up_kernel(te_ref, tf_ref, tv_ref, x_ref, wg_ref, wu_ref, bg_ref, bu_ref, o_ref, wg16, wu16):
    i = pl.program_id(1)

    @pl.when(tf_ref[i] == 1)
    def _():
        wg16[...] = wg_ref[...].astype(BF16)
        wu16[...] = wu_ref[...].astype(BF16)

    @pl.when(tv_ref[i] == 1)
    def _():
        x = x_ref[...]
        g = jnp.minimum(_dot(x, wg16[...]) + bg_ref[...], SWIGLU_LIMIT)
        u = jnp.clip(_dot(x, wu16[...]) + bu_ref[...], -SWIGLU_LIMIT, SWIGLU_LIMIT)
        o_ref[...] = ((u + 1.0) * g * _sigmoid(SWIGLU_ALPHA * g)).astype(BF16)

    @pl.when(tv_ref[i] == 0)
    def _():
        o_ref[...] = jnp.zeros(o_ref.shape, BF16)


def _moe_gate_up(te, tf, tv, xs, w_gu, b_gu, layer, tmm, tn):
    p_pad, d = xs.shape
    de = w_gu.shape[-1] // 2
    nj = de // tn
    grid_spec = pltpu.PrefetchScalarGridSpec(
        num_scalar_prefetch=3, grid=(nj, p_pad // tmm),
        in_specs=[pl.BlockSpec((tmm, d), lambda j, i, te, tf, tv: (i, 0)),
                  pl.BlockSpec((None, None, d, tn), lambda j, i, te, tf, tv: (layer, te[i], 0, j)),
                  pl.BlockSpec((None, None, d, tn), lambda j, i, te, tf, tv: (layer, te[i], 0, nj + j)),
                  pl.BlockSpec((None, None, 1, tn), lambda j, i, te, tf, tv: (layer, te[i], 0, j)),
                  pl.BlockSpec((None, None, 1, tn), lambda j, i, te, tf, tv: (layer, te[i], 0, nj + j))],
        out_specs=pl.BlockSpec((tmm, tn), lambda j, i, te, tf, tv: (i, j)),
        scratch_shapes=[pltpu.VMEM((d, tn), BF16), pltpu.VMEM((d, tn), BF16)])
    b4 = b_gu.reshape(b_gu.shape[0], b_gu.shape[1], 1, b_gu.shape[2])
    return pl.pallas_call(
        _moe_gate_up_kernel, grid_spec=grid_spec,
        out_shape=jax.ShapeDtypeStruct((p_pad, de), BF16),
        compiler_params=_params(2), name="moe_gate_up")(te, tf, tv, xs, w_gu, w_gu, b4, b4)


def _moe_down_kernel(te_ref, tf_ref, tv_ref, a_ref, w_ref, b_ref, o_ref, w16):
    i = pl.program_id(1)

    @pl.when(tf_ref[i] == 1)
    def _():
        w16[...] = w_ref[...].astype(BF16)

    @pl.when(tv_ref[i] == 1)
    def _():
        o_ref[...] = _dot(a_ref[...], w16[...]) + b_ref[...]

    @pl.when(tv_ref[i] == 0)
    def _():
        o_ref[...] = jnp.zeros(o_ref.shape, F32)


def _moe_down(te, tf, tv, a, w_d, b_d, layer, tmm, tn):
    p_pad, de = a.shape
    d = w_d.shape[-1]
    grid_spec = pltpu.PrefetchScalarGridSpec(
        num_scalar_prefetch=3, grid=(d // tn, p_pad // tmm),
        in_specs=[pl.BlockSpec((tmm, de), lambda j, i, te, tf, tv: (i, 0)),
                  pl.BlockSpec((None, None, de, tn), lambda j, i, te, tf, tv: (layer, te[i], 0, j)),
                  pl.BlockSpec((None, None, 1, tn), lambda j, i, te, tf, tv: (layer, te[i], 0, j))],
        out_specs=pl.BlockSpec((tmm, tn), lambda j, i, te, tf, tv: (i, j)),
        scratch_shapes=[pltpu.VMEM((de, tn), BF16)])
    b4 = b_d.reshape(b_d.shape[0], b_d.shape[1], 1, b_d.shape[2])
    return pl.pallas_call(
        _moe_down_kernel, grid_spec=grid_spec,
        out_shape=jax.ShapeDtypeStruct((p_pad, d), F32),
        compiler_params=_params(2), name="moe_down")(te, tf, tv, a, w_d, b4)


def _moe_combine_ln_kernel(y_ref, gate_ref, x_ref, g_ref, b_ref, o_ref, o16_ref):
    d = x_ref.shape[1]
    gate = gate_ref[...]
    out = gate[:, 0:1] * y_ref[:, 0:d]
    for kk in range(1, TOP_K):
        out = out + gate[:, kk:kk + 1] * y_ref[:, kk * d:(kk + 1) * d]
    res = _layernorm_rows(DEEPNORM_ALPHA * x_ref[...] + out, g_ref[...], b_ref[...])
    o_ref[...] = res
    o16_ref[...] = res.astype(BF16)


def _moe_combine_ln(yk, gate, x, g, b, tm):
    t, d = x.shape
    return pl.pallas_call(
        _moe_combine_ln_kernel, grid=(t // tm,),
        in_specs=[pl.BlockSpec((tm, TOP_K * d), lambda i: (i, 0)),
                  pl.BlockSpec((tm, LANES), lambda i: (i, 0)),
                  pl.BlockSpec((tm, d), lambda i: (i, 0)),
                  pl.BlockSpec((1, d), lambda i: (0, 0)), pl.BlockSpec((1, d), lambda i: (0, 0))],
        out_specs=[pl.BlockSpec((tm, d), lambda i: (i, 0))] * 2,
        out_shape=[jax.ShapeDtypeStruct((t, d), F32), jax.ShapeDtypeStruct((t, d), BF16)],
        compiler_params=_params(1), name="moe_combine_ln")(yk, gate, x, g.reshape(1, d), b.reshape(1, d))


MOE_TILE_ROWS = 256
MOE_TILE_COLS = 512
ROW_TILE = 128


def _moe(x, x16, layer, router_w, router_b, w_gu, b_gu, w_d, b_d, ln_g, ln_b):
    t, d = x.shape
    w_pad = jnp.pad(router_w[layer], ((0, 0), (0, LANES - N_EXPERTS)))
    b_pad = jnp.pad(router_b[layer], (0, LANES - N_EXPERTS)).reshape(1, LANES)
    top_i, gate = _router(x, w_pad, b_pad, ROW_TILE)

    n_pairs = t * TOP_K
    n_tiles = n_pairs // MOE_TILE_ROWS + N_EXPERTS
    e_flat = top_i[:, :TOP_K].reshape(n_pairs)
    onehot = (e_flat[:, None] == jnp.arange(N_EXPERTS, dtype=jnp.int32)[None, :]).astype(jnp.int32)
    csum = jnp.cumsum(onehot, axis=0)
    counts = csum[-1]
    rank = jnp.sum(onehot * csum, axis=1) - 1
    tiles_per = (counts + MOE_TILE_ROWS - 1) // MOE_TILE_ROWS
    tile_end = jnp.cumsum(tiles_per)
    dest = ((tile_end - tiles_per) * MOE_TILE_ROWS)[e_flat] + rank
    row_token = jnp.zeros((n_tiles * MOE_TILE_ROWS,), jnp.int32).at[dest].set(
        jnp.arange(n_pairs, dtype=jnp.int32) // TOP_K)
    tile_id = jnp.arange(n_tiles, dtype=jnp.int32)
    tv = (tile_id < tile_end[-1]).astype(jnp.int32)
    te = jnp.searchsorted(tile_end, jnp.minimum(tile_id, tile_end[-1] - 1), side="right").astype(jnp.int32)
    tf = jnp.concatenate([jnp.ones((1,), jnp.int32), (te[1:] != te[:-1]).astype(jnp.int32)])

    xs = x16[row_token]
    act = _moe_gate_up(te, tf, tv, xs, w_gu, b_gu, layer, MOE_TILE_ROWS, MOE_TILE_COLS)
    ys = _moe_down(te, tf, tv, act, w_d, b_d, layer, MOE_TILE_ROWS, MOE_TILE_COLS)
    yk = ys[dest].reshape(t, TOP_K * d)
    return _moe_combine_ln(yk, gate, x, ln_g, ln_b, ROW_TILE)


PROJ_ROW_TILE = 640
PROJ_A_COL_TILE = 512
PROJ_C_COL_TILE = 896
OUT_ROW_TILE = 320
FOX_Q_TILE = 256
NSA_Q_TILE = 128


def _attention_layer(x, x16, i, n, s, nb, caches, page_table, w_in, b_f, w_out, rel_table, ln_g, ln_b):
    (cache_fox_k, cache_fox_v, cache_fox_logf, cache_cmp_k, cache_cmp_v, cache_sel_k, cache_sel_v,
     cache_win_k, cache_win_v) = caches
    tp = n * s
    past = page_table.shape[1] * PAGE_SIZE
    win = cache_win_k.shape[2]
    o_f = 3 * FOX_W
    o_n = o_f + FOX_HEADS
    o_g = o_n + NSA_W + 6 * NSA_KV_W
    w_main = jnp.concatenate([w_in[:, :o_f], w_in[:, o_n:o_g]], axis=1).astype(BF16)
    n_small = FOX_HEADS + 3 * NSA_HEADS
    w_small = jnp.pad(jnp.concatenate([w_in[:, o_f:o_n], w_in[:, o_g:]], axis=1), ((0, 0), (0, LANES - n_small)))
    b_small = jnp.pad(b_f, (0, LANES - FOX_HEADS))

    h = _mm(x16, w_main, PROJ_ROW_TILE, PROJ_A_COL_TILE, "proj_a")
    logf_all, logf_t, sg_all = _gates_a(x, w_small, b_small, ROW_TILE)
    logf = logf_all[:, :FOX_HEADS]
    gates = sg_all[:, FOX_HEADS:n_small]

    c_row = _cumsum_seq(logf_t, n, s)
    o_fox_p = _fox_prompt(h, c_row.reshape(n, FOX_HEADS, s, 1), c_row.reshape(n, FOX_HEADS, 1, s), n, s,
                          0, FOX_HEADS, 2 * FOX_HEADS, FOX_Q_TILE)
    tiles, cmpb, bk, bw, bc = _nsa_bias_tables(rel_table, s, NSA_Q_TILE, past, win)
    gates_g = gates.reshape(-1, NSA_GROUPS, 3 * NSA_HPG).transpose(1, 0, 2)
    col_nsa = 3 * FOX_W
    o_nsa_p = _nsa_prompt(h, gates_g, tiles, cmpb, n, s, col_nsa // (NSA_HPG * HEAD_DIM),
                          (col_nsa + NSA_W) // HEAD_DIM, NSA_Q_TILE)

    hs = h[tp:]
    seg = lambda lo, w: hs[:, lo:lo + w]
    q_fox, k_fox, v_fox = seg(0, FOX_W), seg(FOX_W, FOX_W), seg(2 * FOX_W, FOX_W)
    q_nsa = seg(col_nsa, NSA_W).reshape(nb, NSA_HEADS, HEAD_DIM)
    kv = [seg(col_nsa + NSA_W + j * NSA_KV_W, NSA_KV_W) for j in range(6)]
    pool = cache_fox_k.shape[1]
    paged = lambda c, w: c.reshape(c.shape[0], pool, PAGE_SIZE, w)
    r3 = lambda a: a.reshape(nb, 1, a.shape[-1])
    o_fox_s = _fox_sample(page_table, i, r3(q_fox), r3(k_fox), r3(v_fox),
                          logf[tp:].reshape(nb, FOX_HEADS, 1),
                          paged(cache_fox_k, FOX_W), paged(cache_fox_v, FOX_W),
                          jnp.swapaxes(cache_fox_logf, 2, 3))
    o_c, sel = _nsa_sample_cmp(page_table, i, q_nsa, paged(cache_cmp_k, NSA_KV_W),
                               paged(cache_cmp_v, NSA_KV_W), bc, past)
    o_nsa_s = _nsa_sample_sel(
        page_table, i, q_nsa, r3(kv[2]), r3(kv[3]), r3(kv[4]), r3(kv[5]), sel, o_c,
        gates[tp:].reshape(nb, NSA_HEADS, 3), paged(cache_sel_k, NSA_KV_W), paged(cache_sel_v, NSA_KV_W),
        cache_win_k.reshape(cache_win_k.shape[0], nb, win, NSA_KV_W),
        cache_win_v.reshape(cache_win_v.shape[0], nb, win, NSA_KV_W), bk, bw, rel_table[0].reshape(NSA_HEADS, 1))

    o_fox = jnp.concatenate([o_fox_p, o_fox_s.reshape(nb, FOX_W)], axis=0)
    o_nsa = jnp.concatenate([o_nsa_p, o_nsa_s.reshape(nb, NSA_W)], axis=0)
    w_out16 = w_out.astype(BF16)
    x_new, x_new16 = _mm_res_ln([o_fox, o_nsa], [w_out16[:FOX_W], w_out16[FOX_W:]], x, ln_g, ln_b,
                                OUT_ROW_TILE, "attn_out_ln")

    hp = h[:tp].reshape(n, s, -1)
    wb = min(NSA_WINDOW, s)

    def state(lo, heads, keep=None):
        p = hp[:, :, lo:lo + heads * HEAD_DIM]
        if keep is not None:
            p = p[:, s - keep:]
        return (p.reshape(n, p.shape[1], heads, HEAD_DIM),
                hs[:, lo:lo + heads * HEAD_DIM].reshape(nb, 1, heads, HEAD_DIM))

    kv_lo = col_nsa + NSA_W
    states = [state(FOX_W, FOX_HEADS), state(2 * FOX_W, FOX_HEADS),
              (logf[:tp].reshape(n, s, FOX_HEADS), logf[tp:].reshape(nb, 1, FOX_HEADS))]
    states += [state(kv_lo + j * NSA_KV_W, NSA_GROUPS) for j in range(4)]
    states += [state(kv_lo + j * NSA_KV_W, NSA_GROUPS, wb) for j in (4, 5)]
    return x_new, x_new16, states


def _gla_layer(x, x16, i, n, s, nb, state_gla, w_in, w_g2, b_g, g_norm, w_out, ln_g, ln_b):
    tp = n * s
    qk_w = GLA_HEADS * GLA_DK
    v_w = GLA_HEADS * GLA_DV
    o_g = 2 * qk_w + v_w
    o_r = o_g + GLA_GATE_RANK
    w_main = jnp.concatenate([w_in[:, :o_g], w_in[:, o_r:], w_in[:, o_g:o_r],
                              jnp.zeros((w_in.shape[0], LANES - GLA_GATE_RANK), w_in.dtype)], axis=1).astype(BF16)
    hc = _mm(x16, w_main, PROJ_ROW_TILE, PROJ_C_COL_TILE, "proj_c")
    w_g2_pad = jnp.pad(w_g2, ((0, LANES - GLA_GATE_RANK), (0, 0))).astype(BF16)
    la = _gla_gate(hc, (o_g + v_w) // LANES, w_g2_pad, b_g, PROJ_ROW_TILE)

    og_p, st_p = _gla_prompt(hc, la, g_norm, n, s)
    hs = hc[tp:]
    col = lambda a: a.reshape(nb, GLA_HEADS, GLA_DK, 1)
    rowv = lambda a: a.reshape(nb, GLA_HEADS, 1, GLA_DV)
    og_s, st_s = _gla_sample(col(hs[:, :qk_w]), col(hs[:, qk_w:2 * qk_w]), col(la[tp:]),
                             rowv(hs[:, 2 * qk_w:o_g]), rowv(hs[:, o_g:o_g + v_w]), g_norm, state_gla[i])
    og = jnp.concatenate([og_p, og_s.reshape(nb, v_w)], axis=0)
    x_new, x_new16 = _mm_res_ln([og], [w_out.astype(BF16)], x, ln_g, ln_b, OUT_ROW_TILE, "gla_out_ln")
    return x_new, x_new16, st_p, st_s


def kernel(x_prompt, x_sample, cache_fox_k, cache_fox_v, cache_fox_logf, cache_nsa_cmp_k, cache_nsa_cmp_v,
           cache_nsa_sel_k, cache_nsa_sel_v, cache_nsa_win_k, cache_nsa_win_v, state_gla, page_table,
           w_in_a, b_forget, w_out_a, rel_bias_table, w_in_c, w_gla_gate2, b_gla_gate, gla_norm_g, w_out_c,
           ln_g, ln_b, router_w, router_b, w_gate_up, b_gate_up, w_down, b_down):
    n, s, d = x_prompt.shape
    nb = x_sample.shape[0]
    assert x_sample.shape[1] == 1
    tp = n * s
    x = jnp.concatenate([x_prompt.reshape(tp, d), x_sample.reshape(nb, d)], axis=0)
    x16 = x.astype(BF16)
    caches = (cache_fox_k, cache_fox_v, cache_fox_logf, cache_nsa_cmp_k, cache_nsa_cmp_v,
              cache_nsa_sel_k, cache_nsa_sel_v, cache_nsa_win_k, cache_nsa_win_v)
    attn_states, gla_p, gla_s = [], [], []
    for layer in range(DEPTH):
        i = layer // 2
        if layer % 2 == 0:
            x, x16, st = _attention_layer(x, x16, i, n, s, nb, caches, page_table, w_in_a[i], b_forget[i],
                                          w_out_a[i], rel_bias_table, ln_g[layer, 0], ln_b[layer, 0])
            attn_states.append(st)
        else:
            x, x16, st_p, st_s = _gla_layer(x, x16, i, n, s, nb, state_gla, w_in_c[i], w_gla_gate2[i],
                                            b_gla_gate[i], gla_norm_g[i], w_out_c[i],
                                            ln_g[layer, 0], ln_b[layer, 0])
            gla_p.append(st_p)
            gla_s.append(st_s)
        x, x16 = _moe(x, x16, layer, router_w, router_b, w_gate_up, b_gate_up, w_down, b_down,
                      ln_g[layer, 1], ln_b[layer, 1])
    outs = [x[:tp].reshape(n, s, d), x[tp:].reshape(nb, 1, d)]
    for j in range(9):
        outs.append(jnp.stack([st[j][0] for st in attn_states]))
        outs.append(jnp.stack([st[j][1] for st in attn_states]))
    outs.append(jnp.stack(gla_p))
    outs.append(jnp.stack(gla_s))
    return tuple(outs)
```

```python
import functools
import math

import jax
import jax.numpy as jnp
import numpy as np
from jax import lax
from jax.experimental import pallas as pl
from jax.experimental.pallas import tpu as pltpu

F32 = jnp.float32
BF16 = jnp.bfloat16
HIGHEST = lax.Precision.HIGHEST
NEG_INF = float("-inf")
POS_INF = float("inf")

D_MODEL = 2048
PAGE_SIZE = 128
HEAD_DIM = 128
FOX_HEADS = 8
NSA_HEADS = 8
NSA_GROUPS = 2
NSA_HPG = NSA_HEADS // NSA_GROUPS
NSA_BLOCK = 64
NSA_TOPK = 8
NSA_WINDOW = 512
REL_BUCKETS = 32
REL_MAX_DIST = 1024
GLA_HEADS = 4
GLA_DK = D_MODEL // 2 // GLA_HEADS
GLA_DV = D_MODEL // GLA_HEADS
GLA_GATE_RANK = 16
GLA_TAU = 16.0
GLA_CHUNK = 64
GLA_SUB = 16
N_EXPERTS = 32
TOP_K = 4
SWIGLU_LIMIT = 7.0
SWIGLU_ALPHA = 1.702
LN_EPS = 1e-5
DEPTH = 2
DEEPNORM_ALPHA = (2 * DEPTH) ** 0.25
FOX_W = FOX_HEADS * HEAD_DIM
NSA_W = NSA_HEADS * HEAD_DIM
NSA_KV_W = NSA_GROUPS * HEAD_DIM
ATTN_SCALE = HEAD_DIM ** -0.5

VMEM_LIMIT_BYTES = 56 * 1024 * 1024
LANES = 128

_REL_EXACT = REL_BUCKETS // 2
_REL_THRESHOLDS = tuple(
    (_REL_EXACT * 8) if 2 * j == (REL_BUCKETS - _REL_EXACT) else
    math.ceil(_REL_EXACT * (REL_MAX_DIST / _REL_EXACT) ** (j / (REL_BUCKETS - _REL_EXACT)))
    for j in range(1, REL_BUCKETS - _REL_EXACT))


def _params(n_axes):
    return pltpu.CompilerParams(dimension_semantics=("arbitrary",) * n_axes,
                                vmem_limit_bytes=VMEM_LIMIT_BYTES)


def _iota(shape, dim, dtype=jnp.int32):
    return lax.broadcasted_iota(dtype, shape, dim)


def _shift_div(x, d):
    k = d.bit_length() - 1
    assert d == 1 << k
    return lax.shift_right_logical(x, jnp.full(x.shape, k, jnp.int32))


def _sigmoid(x):
    return 1.0 / (1.0 + jnp.exp(-x))


def _log_sigmoid(x):
    return jnp.minimum(x, 0.0) - jnp.log(1.0 + jnp.exp(-jnp.abs(x)))


def _dot_nt(a, b, precision=None):
    return lax.dot_general(a, b, (((1,), (1,)), ((), ())), precision=precision,
                           preferred_element_type=F32)


def _dot(a, b, precision=None):
    return jnp.dot(a, b, precision=precision, preferred_element_type=F32)


def _masked_softmax_parts(s, mask):
    s = jnp.where(mask, s, NEG_INF)
    m = jnp.max(s, axis=-1, keepdims=True)
    m = jnp.where(m == NEG_INF, 0.0, m)
    e = jnp.exp(s - m)
    return e, jnp.maximum(jnp.sum(e, axis=-1, keepdims=True), 1e-30)


def _online_update(carry, s, v_bf16):
    m, l, acc = carry
    m_new = jnp.maximum(m, jnp.max(s, axis=-1, keepdims=True))
    m_safe = jnp.where(m_new == NEG_INF, 0.0, m_new)
    p = jnp.exp(s - m_safe)
    alpha = jnp.exp(m - m_safe)
    l = alpha * l + jnp.sum(p, axis=-1, keepdims=True)
    acc = alpha * acc + _dot(p.astype(BF16), v_bf16)
    return m_new, l, acc


def _rel_bucket(dist):
    n = jnp.maximum(dist, 0)
    large = jnp.full(n.shape, _REL_EXACT, jnp.int32)
    for t in _REL_THRESHOLDS:
        large = large + jnp.where(n >= t, 1, 0)
    return jnp.where(n < _REL_EXACT, n, large)


def _rel_bias(bucket, table_ref, head):
    out = jnp.zeros(bucket.shape, F32)
    for b in range(REL_BUCKETS):
        out = jnp.where(bucket == b, table_ref[b, head], out)
    return out


def _top_blocks(score, n_sel):
    lane = _iota(score.shape, 1).astype(F32)
    sel = jnp.zeros(score.shape, F32)
    for _ in range(n_sel):
        mx = jnp.max(score, axis=-1, keepdims=True)
        first = jnp.min(jnp.where(score == mx, lane, 1e9), axis=-1, keepdims=True)
        hit = lane == first
        sel = jnp.where(hit & (mx > NEG_INF), 1.0, sel)
        score = jnp.where(hit, NEG_INF, score)
    return sel


def _mm_kernel(x_ref, w_ref, o_ref):
    o_ref[...] = _dot(x_ref[...], w_ref[...])


def _mm(x, w, tm, tn, name):
    m, k = x.shape
    n = w.shape[1]
    assert m % tm == 0 and n % tn == 0
    return pl.pallas_call(
        _mm_kernel, grid=(n // tn, m // tm),
        in_specs=[pl.BlockSpec((tm, k), lambda j, i: (i, 0)),
                  pl.BlockSpec((k, tn), lambda j, i: (0, j))],
        out_specs=pl.BlockSpec((tm, tn), lambda j, i: (i, j)),
        out_shape=jax.ShapeDtypeStruct((m, n), F32),
        compiler_params=_params(2), name=name)(x, w)


def _layernorm_rows(z, g, b):
    mu = jnp.mean(z, axis=-1, keepdims=True)
    zc = z - mu
    var = jnp.mean(zc * zc, axis=-1, keepdims=True)
    return zc * lax.rsqrt(var + LN_EPS) * g + b


def _mm_res_ln_kernel(*refs, n_in):
    a_refs, w_refs = refs[:n_in], refs[n_in:2 * n_in]
    x_ref, g_ref, b_ref, o_ref, o16_ref = refs[2 * n_in:]
    y = _dot(a_refs[0][...].astype(BF16), w_refs[0][...])
    for a_ref, w_ref in zip(a_refs[1:], w_refs[1:]):
        y = y + _dot(a_ref[...].astype(BF16), w_ref[...])
    out = _layernorm_rows(DEEPNORM_ALPHA * x_ref[...] + y, g_ref[...], b_ref[...])
    o_ref[...] = out
    o16_ref[...] = out.astype(BF16)


def _mm_res_ln(acts, weights, x, g, b, tm, name):
    t, d = x.shape
    n_in = len(acts)
    in_specs = [pl.BlockSpec((tm, a.shape[1]), lambda i: (i, 0)) for a in acts]
    in_specs += [pl.BlockSpec(w.shape, lambda i: (0, 0)) for w in weights]
    in_specs += [pl.BlockSpec((tm, d), lambda i: (i, 0)),
                 pl.BlockSpec((1, d), lambda i: (0, 0)), pl.BlockSpec((1, d), lambda i: (0, 0))]
    return pl.pallas_call(
        functools.partial(_mm_res_ln_kernel, n_in=n_in), grid=(t // tm,),
        in_specs=in_specs,
        out_specs=[pl.BlockSpec((tm, d), lambda i: (i, 0))] * 2,
        out_shape=[jax.ShapeDtypeStruct((t, d), F32), jax.ShapeDtypeStruct((t, d), BF16)],
        compiler_params=_params(1), name=name)(*acts, *weights, x, g.reshape(1, d), b.reshape(1, d))


def _gates_a_kernel(x_ref, w_ref, wt_ref, b_ref, bt_ref, logf_ref, logft_ref, sg_ref):
    x = x_ref[...]
    z = _dot(x, w_ref[...], HIGHEST)
    logf_ref[...] = _log_sigmoid(z + b_ref[...])
    sg_ref[...] = _sigmoid(z)
    zt = _dot_nt(wt_ref[...], x, HIGHEST)
    logft_ref[...] = _log_sigmoid(zt + bt_ref[...])[:FOX_HEADS]


def _gates_a(x, w_small, b_small, tm):
    t, d = x.shape
    return pl.pallas_call(
        _gates_a_kernel, grid=(t // tm,),
        in_specs=[pl.BlockSpec((tm, d), lambda i: (i, 0)),
                  pl.BlockSpec((d, LANES), lambda i: (0, 0)),
                  pl.BlockSpec((LANES, d), lambda i: (0, 0)),
                  pl.BlockSpec((1, LANES), lambda i: (0, 0)),
                  pl.BlockSpec((LANES, 1), lambda i: (0, 0))],
        out_specs=[pl.BlockSpec((tm, LANES), lambda i: (i, 0)),
                   pl.BlockSpec((FOX_HEADS, tm), lambda i: (0, i)),
                   pl.BlockSpec((tm, LANES), lambda i: (i, 0))],
        out_shape=[jax.ShapeDtypeStruct((t, LANES), F32), jax.ShapeDtypeStruct((FOX_HEADS, t), F32),
                   jax.ShapeDtypeStruct((t, LANES), F32)],
        compiler_params=_params(1), name="gates_a")(
            x, w_small, w_small.T, b_small.reshape(1, LANES), b_small.reshape(LANES, 1))


def _cumsum_kernel(x_ref, o_ref, *, blk):
    s = x_ref.shape[1]
    upper = jnp.where(_iota((blk, blk), 0) <= _iota((blk, blk), 1), 1.0, 0.0)
    carry = jnp.zeros((x_ref.shape[0], 1), F32)
    for j in range(s // blk):
        c = _dot(x_ref[:, j * blk:(j + 1) * blk], upper, HIGHEST) + carry
        o_ref[:, j * blk:(j + 1) * blk] = c
        carry = c[:, blk - 1:blk]


def _cumsum_seq(xt, n, s):
    h = xt.shape[0]
    return pl.pallas_call(
        functools.partial(_cumsum_kernel, blk=min(256, s)), grid=(n,),
        in_specs=[pl.BlockSpec((h, s), lambda i: (0, i))],
        out_specs=pl.BlockSpec((None, h, s), lambda i: (i, 0, 0)),
        out_shape=jax.ShapeDtypeStruct((n, h, s), F32),
        compiler_params=_params(1), name="fox_cumsum")(xt)


def _fox_prompt_kernel(q_ref, k_ref, v_ref, cq_ref, ck_ref, o_ref, *, tq):
    qi = pl.program_id(2)
    q = q_ref[...].astype(BF16)
    cq = cq_ref[...]
    qpos = qi * tq + _iota((tq, tq), 0)

    def body(kb, carry):
        start = pl.multiple_of(kb * tq, tq)
        k = k_ref[pl.ds(start, tq), :].astype(BF16)
        v = v_ref[pl.ds(start, tq), :].astype(BF16)
        s = _dot_nt(q, k) * ATTN_SCALE + cq - ck_ref[:, pl.ds(start, tq)]
        s = jnp.where(start + _iota((tq, tq), 1) <= qpos, s, NEG_INF)
        return _online_update(carry, s, v)

    init = (jnp.full((tq, 1), NEG_INF, F32), jnp.zeros((tq, 1), F32), jnp.zeros((tq, HEAD_DIM), F32))
    _, l, acc = lax.fori_loop(0, qi + 1, body, init)
    o_ref[...] = acc / jnp.maximum(l, 1e-30)


def _fox_prompt(h, c_col, c_row, n, s, col_q, col_k, col_v, tq):
    nq = s // tq
    kv_spec = lambda col: pl.BlockSpec((s, HEAD_DIM), lambda b, hd, qi: (b, col + hd))
    return pl.pallas_call(
        functools.partial(_fox_prompt_kernel, tq=tq), grid=(n, FOX_HEADS, nq),
        in_specs=[pl.BlockSpec((tq, HEAD_DIM), lambda b, hd, qi: (b * nq + qi, col_q + hd)),
                  kv_spec(col_k), kv_spec(col_v),
                  pl.BlockSpec((None, None, tq, 1), lambda b, hd, qi: (b, hd, qi, 0)),
                  pl.BlockSpec((None, None, 1, s), lambda b, hd, qi: (b, hd, 0, 0))],
        out_specs=pl.BlockSpec((tq, HEAD_DIM), lambda b, hd, qi: (b * nq + qi, hd)),
        out_shape=jax.ShapeDtypeStruct((n * s, FOX_W), F32),
        compiler_params=_params(3), name="fox_prompt")(h, h, h, c_col, c_row)


def _nsa_bias_kernel(table_ref, tiles_ref, cmpb_ref, bk_ref, bw_ref, bc_ref, *, tq, past):
    d = pl.program_id(0)
    dist = d * tq + _iota((tq, tq), 0) - _iota((tq, tq), 1)
    bucket = _rel_bucket(dist)
    nb = cmpb_ref.shape[-1]
    dist_c = jnp.minimum(d, pl.num_programs(0) - 2) * tq + _iota((tq, nb), 0) \
        - (_iota((tq, nb), 1) * NSA_BLOCK + NSA_BLOCK - 1)
    bucket_c = _rel_bucket(dist_c)
    for h in range(NSA_HEADS):
        tiles_ref[h] = _rel_bias(bucket, table_ref, h)
        cmpb_ref[h] = _rel_bias(bucket_c, table_ref, h)

    @pl.when(d == 0)
    def _():
        win = bw_ref.shape[1] // NSA_GROUPS
        bucket_k = _rel_bucket(past - _shift_div(_iota((1, bk_ref.shape[1]), 1), NSA_GROUPS))
        bucket_w = _rel_bucket(win - _shift_div(_iota((1, bw_ref.shape[1]), 1), NSA_GROUPS))
        bucket_b = _rel_bucket(past - (_iota((1, LANES), 1) * NSA_BLOCK + NSA_BLOCK - 1))
        for h in range(NSA_HEADS):
            bk_ref[h:h + 1, :] = _rel_bias(bucket_k, table_ref, h)
            bw_ref[h:h + 1, :] = _rel_bias(bucket_w, table_ref, h)
            bc_ref[h:h + 1, :] = _rel_bias(bucket_b, table_ref, h)


def _nsa_bias_tables(rel_table, s, tq, past, win):
    nq = s // tq
    nb = LANES
    assert s // NSA_BLOCK <= LANES
    return pl.pallas_call(
        functools.partial(_nsa_bias_kernel, tq=tq, past=past),
        grid=(nq + 1,),
        in_specs=[pl.BlockSpec(memory_space=pltpu.SMEM)],
        out_specs=[pl.BlockSpec((None, NSA_HEADS, tq, tq), lambda d: (d, 0, 0, 0)),
                   pl.BlockSpec((None, NSA_HEADS, tq, nb), lambda d: (jnp.minimum(d, nq - 1), 0, 0, 0)),
                   pl.BlockSpec((NSA_HEADS, NSA_GROUPS * past), lambda d: (0, 0)),
                   pl.BlockSpec((NSA_HEADS, NSA_GROUPS * win), lambda d: (0, 0)),
                   pl.BlockSpec((NSA_HEADS, LANES), lambda d: (0, 0))],
        out_shape=[jax.ShapeDtypeStruct((nq + 1, NSA_HEADS, tq, tq), F32),
                   jax.ShapeDtypeStruct((nq, NSA_HEADS, tq, nb), F32),
                   jax.ShapeDtypeStruct((NSA_HEADS, NSA_GROUPS * past), F32),
                   jax.ShapeDtypeStruct((NSA_HEADS, NSA_GROUPS * win), F32),
                   jax.ShapeDtypeStruct((NSA_HEADS, LANES), F32)],
        compiler_params=_params(1), name="nsa_bias")(rel_table)


def _nsa_prompt_kernel(q_ref, kc_ref, vc_ref, ks_ref, vs_ref, kw_ref, vw_ref, gate_ref, tiles_ref,
                       cmpb_ref, o_ref, *, tq):
    qi = pl.program_id(2)
    s_len = kc_ref.shape[0]
    n_blk = s_len // NSA_BLOCK
    nb = LANES
    rows = NSA_HPG * tq
    q = jnp.concatenate([q_ref[:, h * HEAD_DIM:(h + 1) * HEAD_DIM] for h in range(NSA_HPG)],
                        axis=0).astype(BF16)
    stack = lambda ref, idx: jnp.concatenate([ref[idx, h] for h in range(NSA_HPG)], axis=0)

    def block_means(ref):
        m = jnp.mean(ref[...].reshape(n_blk, NSA_BLOCK, HEAD_DIM), axis=1)
        return jnp.concatenate([m, jnp.zeros((nb - n_blk, HEAD_DIM), F32)], axis=0).astype(BF16)

    qpos_r = qi * tq + (_iota((rows, nb), 0) & (tq - 1))
    blk_r = _iota((rows, nb), 1)
    bias_c = jnp.concatenate([cmpb_ref[h] for h in range(NSA_HPG)], axis=0)
    s_c = _dot_nt(q, block_means(kc_ref)) * ATTN_SCALE + bias_c
    e_c, den_c = _masked_softmax_parts(
        s_c, (blk_r < n_blk) & (qpos_r >= blk_r * NSA_BLOCK + NSA_BLOCK - 1))
    p_c = e_c / den_c
    o_c = _dot(p_c.astype(BF16), block_means(vc_ref))

    p_sum = p_c[0:tq]
    for h in range(1, NSA_HPG):
        p_sum = p_sum + p_c[h * tq:(h + 1) * tq]
    blk = _iota((tq, nb), 1)
    cur = _shift_div(qi * tq + _iota((tq, nb), 0), NSA_BLOCK)
    forced = (blk == 0) | (blk == cur) | (blk == cur - 1)
    score = jnp.where(forced, POS_INF, jnp.where(blk <= cur, p_sum, NEG_INF))
    sel = _top_blocks(score, min(NSA_TOPK, n_blk)).astype(BF16)

    row_q = _iota((rows, tq), 0) & (tq - 1)
    col_k = _iota((rows, tq), 1)
    init = (jnp.full((rows, 1), NEG_INF, F32), jnp.zeros((rows, 1), F32), jnp.zeros((rows, HEAD_DIM), F32))

    def sel_body(kt, carry):
        start = pl.multiple_of(kt * tq, tq)
        delta = qi - kt
        k = ks_ref[pl.ds(start, tq), :].astype(BF16)
        v = vs_ref[pl.ds(start, tq), :].astype(BF16)
        s = _dot_nt(q, k) * ATTN_SCALE + stack(tiles_ref, delta)
        expand = jnp.where(_iota((nb, tq), 0) == _shift_div(start + _iota((nb, tq), 1), NSA_BLOCK), 1.0, 0.0)
        picked = _dot(sel, expand.astype(BF16))
        picked = jnp.concatenate([picked] * NSA_HPG, axis=0)
        dist = delta * tq + row_q - col_k
        s = jnp.where((picked > 0.5) & (dist >= 0), s, NEG_INF)
        return _online_update(carry, s, v)

    _, l_s, acc_s = lax.fori_loop(0, qi + 1, sel_body, init)
    o_s = acc_s / jnp.maximum(l_s, 1e-30)

    def win_body(kt, carry):
        start = pl.multiple_of(kt * tq, tq)
        delta = qi - kt
        k = kw_ref[pl.ds(start, tq), :].astype(BF16)
        v = vw_ref[pl.ds(start, tq), :].astype(BF16)
        s = _dot_nt(q, k) * ATTN_SCALE + stack(tiles_ref, delta)
        dist = delta * tq + row_q - col_k
        s = jnp.where((dist >= 0) & (dist <= NSA_WINDOW), s, NEG_INF)
        return _online_update(carry, s, v)

    first = jnp.maximum(qi - (NSA_WINDOW + tq - 1) // tq, 0)
    _, l_w, acc_w = lax.fori_loop(first, qi + 1, win_body, init)
    o_w = acc_w / jnp.maximum(l_w, 1e-30)

    gate = gate_ref[...]
    for h in range(NSA_HPG):
        r = slice(h * tq, (h + 1) * tq)
        o_ref[:, h * HEAD_DIM:(h + 1) * HEAD_DIM] = (
            gate[:, 3 * h:3 * h + 1] * o_c[r] + gate[:, 3 * h + 1:3 * h + 2] * o_s[r]
            + gate[:, 3 * h + 2:3 * h + 3] * o_w[r])


def _nsa_prompt(h, gates, tiles, cmpb, n, s, col_q, col_kv, tq):
    nq = s // tq
    grp_w = NSA_HPG * HEAD_DIM
    kv_spec = lambda seg: pl.BlockSpec(
        (s, HEAD_DIM), lambda b, g, qi: (b, col_kv + seg * NSA_GROUPS + g))
    return pl.pallas_call(
        functools.partial(_nsa_prompt_kernel, tq=tq), grid=(n, NSA_GROUPS, nq),
        in_specs=[pl.BlockSpec((tq, grp_w), lambda b, g, qi: (b * nq + qi, col_q + g))]
        + [kv_spec(seg) for seg in range(6)]
        + [pl.BlockSpec((None, tq, 3 * NSA_HPG), lambda b, g, qi: (g, b * nq + qi, 0)),
           pl.BlockSpec((nq + 1, NSA_HPG, tq, tq), lambda b, g, qi: (0, g, 0, 0)),
           pl.BlockSpec((None, NSA_HPG, tq, LANES), lambda b, g, qi: (qi, g, 0, 0))],
        out_specs=pl.BlockSpec((tq, grp_w), lambda b, g, qi: (b * nq + qi, g)),
        out_shape=jax.ShapeDtypeStruct((n * s, NSA_W), F32),
        compiler_params=_params(3), name="nsa_prompt")(h, h, h, h, h, h, h, gates, tiles, cmpb)


def _softmax_step(scr, s_parts, v_refs):
    m_scr, l_scr, acc_scr = scr
    s = jnp.concatenate(s_parts, axis=1)
    m = m_scr[...]
    m_new = jnp.maximum(m, jnp.max(s, axis=-1, keepdims=True))
    m_safe = jnp.where(m_new == NEG_INF, 0.0, m_new)
    p = jnp.exp(s - m_safe)
    alpha = jnp.exp(m - m_safe)
    l_scr[...] = alpha * l_scr[...] + jnp.sum(p, axis=-1, keepdims=True)
    acc = alpha * acc_scr[...]
    width = s_parts[0].shape[1]
    for i, v_ref in enumerate(v_refs):
        acc = acc + _dot(p[:, i * width:(i + 1) * width].astype(BF16), v_ref[...].astype(BF16))
    acc_scr[...] = acc
    m_scr[...] = m_new


def _fox_sample_kernel(pt_ref, q_ref, kn_ref, vn_ref, lfn_ref, later_ref, *rest, pp):
    k_refs, v_refs, lf_refs = rest[:pp], rest[pp:2 * pp], rest[2 * pp:3 * pp]
    o_ref, m_scr, l_scr, acc_scr, suf_scr = rest[3 * pp:]
    step = pl.program_id(1)
    q = q_ref[...]

    @pl.when(step == 0)
    def _():
        m_scr[...] = jnp.sum(q * kn_ref[...], axis=-1, keepdims=True) * ATTN_SCALE
        l_scr[...] = jnp.ones(l_scr.shape, F32)
        acc_scr[...] = vn_ref[...].astype(BF16).astype(F32)
        suf_scr[...] = lfn_ref[...]

    cols = k_refs[0].shape[0]
    own = (_iota((FOX_HEADS, cols), 1) & (FOX_HEADS - 1)) == _iota((FOX_HEADS, cols), 0)
    q16 = q.astype(BF16)
    later = later_ref[...]
    suf = suf_scr[...]
    s_parts = []
    for k_ref, lf_ref in zip(k_refs, lf_refs):
        lf = lf_ref[...]
        lf_hi = lf.astype(BF16)
        lf_lo = (lf - lf_hi.astype(F32)).astype(BF16)
        decay = _dot(lf_hi, later) + _dot(lf_lo, later) + suf
        s = _dot_nt(q16, k_ref[...].astype(BF16)) * ATTN_SCALE + decay
        s_parts.append(jnp.where(own, s, NEG_INF))
        suf = suf + jnp.sum(lf, axis=-1, keepdims=True)
    suf_scr[...] = suf
    _softmax_step((m_scr, l_scr, acc_scr), s_parts, v_refs)

    @pl.when(step == pl.num_programs(1) - 1)
    def _():
        o_ref[...] = acc_scr[...] / jnp.maximum(l_scr[...], 1e-30)


FOX_PAGES_PER_STEP = 4
NSA_PAGES_PER_STEP = 8


def _fox_sample(page_table, layer, q, k_new, v_new, logf_new, cache_k, cache_v, cache_logf_t):
    nb, n_pages = page_table.shape
    pp = FOX_PAGES_PER_STEP
    assert n_pages % pp == 0
    rows = PAGE_SIZE * FOX_HEADS
    later = np.repeat(np.tril(np.ones((PAGE_SIZE, PAGE_SIZE), np.float32), -1), FOX_HEADS, axis=1)
    per_b = lambda shape: pl.BlockSpec((None,) + shape, lambda b, p, pt: (b, 0, 0))
    page = lambda i: (lambda b, p, pt: (layer, pt[b, n_pages - 1 - (p * pp + i)], 0, 0))
    grid_spec = pltpu.PrefetchScalarGridSpec(
        num_scalar_prefetch=1, grid=(nb, n_pages // pp),
        in_specs=[per_b((FOX_HEADS, HEAD_DIM)), per_b((FOX_HEADS, HEAD_DIM)), per_b((FOX_HEADS, HEAD_DIM)),
                  per_b((FOX_HEADS, 1)),
                  pl.BlockSpec((PAGE_SIZE, rows), lambda b, p, pt: (0, 0))]
        + [pl.BlockSpec((None, None, rows, HEAD_DIM), page(i)) for i in range(pp)]
        + [pl.BlockSpec((None, None, rows, HEAD_DIM), page(i)) for i in range(pp)]
        + [pl.BlockSpec((None, None, FOX_HEADS, PAGE_SIZE), page(i)) for i in range(pp)],
        out_specs=per_b((FOX_HEADS, HEAD_DIM)),
        scratch_shapes=[pltpu.VMEM((FOX_HEADS, 1), F32), pltpu.VMEM((FOX_HEADS, 1), F32),
                        pltpu.VMEM((FOX_HEADS, HEAD_DIM), F32), pltpu.VMEM((FOX_HEADS, 1), F32)])
    return pl.pallas_call(
        functools.partial(_fox_sample_kernel, pp=pp), grid_spec=grid_spec,
        out_shape=jax.ShapeDtypeStruct((nb, FOX_HEADS, HEAD_DIM), F32),
        compiler_params=_params(2), name="fox_sample")(
            page_table, q, k_new, v_new, logf_new, jnp.asarray(later, BF16),
            *([cache_k] * pp), *([cache_v] * pp), *([cache_logf_t] * pp))


def _group_block_sums(x):
    sub = 8
    y = jnp.sum(x.reshape(x.shape[0] // sub, sub, x.shape[1]), axis=0)
    shift = NSA_GROUPS
    while shift < sub:
        y = y + pltpu.roll(y, shift, 0)
        shift *= 2
    return y


def _nsa_sample_cmp_kernel(pt_ref, q_ref, bc_ref, *rest, past, pp):
    k_refs, v_refs = rest[:pp], rest[pp:2 * pp]
    oc_ref, sel_ref, kcb_scr, vcb_scr = rest[2 * pp:]
    step = pl.program_id(1)
    n_blk = past // NSA_BLOCK
    per_page = PAGE_SIZE // NSA_BLOCK
    blk_rows = NSA_BLOCK * NSA_GROUPS

    @pl.when(step == 0)
    def _():
        kcb_scr[...] = jnp.zeros(kcb_scr.shape, F32)
        vcb_scr[...] = jnp.zeros(vcb_scr.shape, F32)

    row_id = _iota((LANES, HEAD_DIM), 0)
    for refs, scr in ((k_refs, kcb_scr), (v_refs, vcb_scr)):
        tabs = [scr[g] for g in range(NSA_GROUPS)]
        for i, ref in enumerate(refs):
            for bl in range(per_page):
                sums = _group_block_sums(ref[bl * blk_rows:(bl + 1) * blk_rows, :]) * (1.0 / NSA_BLOCK)
                blk_id = (step * pp + i) * per_page + bl
                for g in range(NSA_GROUPS):
                    tabs[g] = jnp.where(row_id == blk_id, sums[g:g + 1], tabs[g])
        for g in range(NSA_GROUPS):
            scr[g] = tabs[g]

    @pl.when(step == pl.num_programs(1) - 1)
    def _():
        q16 = q_ref[...].astype(BF16)
        blk = _iota((NSA_HEADS, LANES), 1)
        head_grp = _shift_div(_iota((NSA_HEADS, LANES), 0), NSA_HPG)
        by_group = lambda parts: functools.reduce(
            lambda acc, gp: jnp.where(head_grp == gp[0], gp[1], acc), enumerate(parts), jnp.zeros_like(parts[0]))
        s_c = by_group([_dot_nt(q16, kcb_scr[g].astype(BF16)) for g in range(NSA_GROUPS)])
        s_c = s_c * ATTN_SCALE + bc_ref[...]
        e_c, den_c = _masked_softmax_parts(s_c, (blk < n_blk) & (past >= blk * NSA_BLOCK + NSA_BLOCK - 1))
        p_c = e_c / den_c
        p16 = p_c.astype(BF16)
        oc_ref[...] = by_group([_dot(p16, vcb_scr[g].astype(BF16)) for g in range(NSA_GROUPS)])
        p_sum = jnp.zeros((NSA_HEADS, LANES), F32)
        for g in range(NSA_GROUPS):
            tot = jnp.sum(p_c[g * NSA_HPG:(g + 1) * NSA_HPG], axis=0, keepdims=True)
            p_sum = jnp.where(head_grp == g, tot, p_sum)
        cur = past // NSA_BLOCK
        forced = (blk == 0) | (blk == cur) | (blk == cur - 1)
        score = jnp.where(forced, POS_INF, jnp.where(blk <= cur, p_sum, NEG_INF))
        sel_ref[...] = _top_blocks(score, min(NSA_TOPK, cur + 1))


def _nsa_sample_cmp(page_table, layer, q, cache_k, cache_v, bc, past):
    assert past % NSA_BLOCK == 0 and past // NSA_BLOCK < LANES
    nb, n_pages = page_table.shape
    pp = NSA_PAGES_PER_STEP
    assert n_pages % pp == 0
    rows = PAGE_SIZE * NSA_GROUPS
    page = lambda i: (lambda b, p, pt: (layer, pt[b, p * pp + i], 0, 0))
    grid_spec = pltpu.PrefetchScalarGridSpec(
        num_scalar_prefetch=1, grid=(nb, n_pages // pp),
        in_specs=[pl.BlockSpec((None, NSA_HEADS, HEAD_DIM), lambda b, p, pt: (b, 0, 0)),
                  pl.BlockSpec((NSA_HEADS, LANES), lambda b, p, pt: (0, 0))]
        + [pl.BlockSpec((None, None, rows, HEAD_DIM), page(i)) for i in range(pp)] * 2,
        out_specs=[pl.BlockSpec((None, NSA_HEADS, HEAD_DIM), lambda b, p, pt: (b, 0, 0)),
                   pl.BlockSpec((None, NSA_HEADS, LANES), lambda b, p, pt: (b, 0, 0))],
        scratch_shapes=[pltpu.VMEM((NSA_GROUPS, LANES, HEAD_DIM), F32),
                        pltpu.VMEM((NSA_GROUPS, LANES, HEAD_DIM), F32)])
    return pl.pallas_call(
        functools.partial(_nsa_sample_cmp_kernel, past=past, pp=pp), grid_spec=grid_spec,
        out_shape=[jax.ShapeDtypeStruct((nb, NSA_HEADS, HEAD_DIM), F32),
                   jax.ShapeDtypeStruct((nb, NSA_HEADS, LANES), F32)],
        compiler_params=_params(2), name="nsa_sample_cmp")(
            page_table, q, bc, *([cache_k] * pp), *([cache_v] * pp))


def _per_head(new_ref):
    head_grp = _shift_div(_iota((NSA_HEADS, HEAD_DIM), 0), NSA_HPG)
    out = jnp.zeros((NSA_HEADS, HEAD_DIM), F32)
    for g in range(NSA_GROUPS):
        out = jnp.where(head_grp == g, new_ref[g:g + 1, :], out)
    return out


def _nsa_sample_sel_kernel(pt_ref, q_ref, ksn_ref, vsn_ref, kwn_ref, vwn_ref, sel_ref, oc_ref, gate_ref,
                           kw_ref, vw_ref, bk_ref, bw_ref, b0_ref, *rest, pp):
    ks_refs, vs_refs = rest[:pp], rest[pp:2 * pp]
    o_ref, m_scr, l_scr, acc_scr = rest[2 * pp:]
    step = pl.program_id(1)
    per_page = PAGE_SIZE // NSA_BLOCK
    q = q_ref[...]
    q16 = q.astype(BF16)
    b0 = b0_ref[...]
    own = lambda cols: ((_iota((NSA_HEADS, cols), 1) & (NSA_GROUPS - 1))
                        == _shift_div(_iota((NSA_HEADS, cols), 0), NSA_HPG))

    @pl.when(step == 0)
    def _():
        m_scr[...] = jnp.sum(q * _per_head(ksn_ref), axis=-1, keepdims=True) * ATTN_SCALE + b0
        l_scr[...] = jnp.ones(l_scr.shape, F32)
        acc_scr[...] = _per_head(vsn_ref).astype(BF16).astype(F32)

    cols = ks_refs[0].shape[0]
    sel16 = sel_ref[...].astype(BF16)
    key_blk = _shift_div(_iota((LANES, cols), 1), NSA_GROUPS * NSA_BLOCK)
    s_parts = []
    for i, ks_ref in enumerate(ks_refs):
        pg = step * pp + i
        expand = jnp.where(_iota((LANES, cols), 0) == pg * per_page + key_blk, 1.0, 0.0)
        picked = _dot(sel16, expand.astype(BF16))
        start = pl.multiple_of(pg * cols, cols)
        s = _dot_nt(q16, ks_ref[...].astype(BF16)) * ATTN_SCALE + bk_ref[:, pl.ds(start, cols)]
        s_parts.append(jnp.where(own(cols) & (picked > 0.5), s, NEG_INF))
    _softmax_step((m_scr, l_scr, acc_scr), s_parts, vs_refs)

    @pl.when(step == pl.num_programs(1) - 1)
    def _():
        o_s = acc_scr[...] / jnp.maximum(l_scr[...], 1e-30)
        wcols = kw_ref.shape[0]
        s_w = _dot_nt(q16, kw_ref[...].astype(BF16)) * ATTN_SCALE + bw_ref[...]
        s_w = jnp.where(own(wcols), s_w, NEG_INF)
        s_n = jnp.sum(q * _per_head(kwn_ref), axis=-1, keepdims=True) * ATTN_SCALE + b0
        m_w = jnp.maximum(jnp.max(s_w, axis=-1, keepdims=True), s_n)
        e_w = jnp.exp(s_w - m_w)
        e_n = jnp.exp(s_n - m_w)
        den = jnp.maximum(jnp.sum(e_w, axis=-1, keepdims=True) + e_n, 1e-30)
        p_w = (e_w / den).astype(BF16)
        p_n = (e_n / den).astype(BF16).astype(F32)
        o_w = _dot(p_w, vw_ref[...].astype(BF16)) + p_n * _per_head(vwn_ref).astype(BF16).astype(F32)
        gate = gate_ref[...]
        o_ref[...] = gate[:, 0:1] * oc_ref[...] + gate[:, 1:2] * o_s + gate[:, 2:3] * o_w


def _nsa_sample_sel(page_table, layer, q, ks_new, vs_new, kw_new, vw_new, sel, o_c, gates,
                    cache_ks, cache_vs, win_k, win_v, bk, bw, b0):
    nb, n_pages = page_table.shape
    pp = NSA_PAGES_PER_STEP
    assert n_pages % pp == 0
    wrows = win_k.shape[2]
    rows = PAGE_SIZE * NSA_GROUPS
    per_b = lambda shape: pl.BlockSpec((None,) + shape, lambda b, p, pt: (b, 0, 0))
    page = lambda i: (lambda b, p, pt: (layer, pt[b, p * pp + i], 0, 0))
    const = lambda shape: pl.BlockSpec(shape, lambda b, p, pt: (0, 0))
    new = per_b((NSA_GROUPS, HEAD_DIM))
    grid_spec = pltpu.PrefetchScalarGridSpec(
        num_scalar_prefetch=1, grid=(nb, n_pages // pp),
        in_specs=[per_b((NSA_HEADS, HEAD_DIM)), new, new, new, new,
                  per_b((NSA_HEADS, LANES)), per_b((NSA_HEADS, HEAD_DIM)), per_b((NSA_HEADS, 3)),
                  pl.BlockSpec((None, None, wrows, HEAD_DIM), lambda b, p, pt: (layer, b, 0, 0)),
                  pl.BlockSpec((None, None, wrows, HEAD_DIM), lambda b, p, pt: (layer, b, 0, 0)),
                  const((NSA_HEADS, n_pages * rows)), const((NSA_HEADS, wrows)), const((NSA_HEADS, 1))]
        + [pl.BlockSpec((None, None, rows, HEAD_DIM), page(i)) for i in range(pp)] * 2,
        out_specs=per_b((NSA_HEADS, HEAD_DIM)),
        scratch_shapes=[pltpu.VMEM((NSA_HEADS, 1), F32), pltpu.VMEM((NSA_HEADS, 1), F32),
                        pltpu.VMEM((NSA_HEADS, HEAD_DIM), F32)])
    return pl.pallas_call(
        functools.partial(_nsa_sample_sel_kernel, pp=pp), grid_spec=grid_spec,
        out_shape=jax.ShapeDtypeStruct((nb, NSA_HEADS, HEAD_DIM), F32),
        compiler_params=_params(2), name="nsa_sample_sel")(
            page_table, q, ks_new, vs_new, kw_new, vw_new, sel, o_c, gates,
            win_k, win_v, bk, bw, b0, *([cache_ks] * pp), *([cache_vs] * pp))


def _gla_gate_kernel(g_ref, w_ref, b_ref, o_ref):
    z = _dot(g_ref[...].astype(BF16), w_ref[...]) + b_ref[...]
    o_ref[...] = _log_sigmoid(z) * (1.0 / GLA_TAU)


def _gla_gate(hc, col_g, w_g2_pad, b_g, tm):
    t = hc.shape[0]
    n = w_g2_pad.shape[1]
    return pl.pallas_call(
        _gla_gate_kernel, grid=(t // tm,),
        in_specs=[pl.BlockSpec((tm, LANES), lambda i: (i, col_g)),
                  pl.BlockSpec((LANES, n), lambda i: (0, 0)),
                  pl.BlockSpec((1, n), lambda i: (0, 0))],
        out_specs=pl.BlockSpec((tm, n), lambda i: (i, 0)),
        out_shape=jax.ShapeDtypeStruct((t, n), F32),
        compiler_params=_params(1), name="gla_gate")(hc, w_g2_pad, b_g.reshape(1, n))


def _gla_out_gate(o, r, g_norm):
    o = o * lax.rsqrt(jnp.mean(o * o, axis=-1, keepdims=True) + LN_EPS) * g_norm
    return o * (r * _sigmoid(r))


def _gla_prompt_kernel(q_ref, k_ref, v_ref, r_ref, la_ref, gn_ref, o_ref, st_ref, stt_scr):
    c = pl.program_id(2)
    ch = q_ref.shape[0]

    @pl.when(c == 0)
    def _():
        stt_scr[...] = jnp.zeros(stt_scr.shape, F32)

    la = la_ref[...]
    row = _iota((ch, ch), 0)
    col = _iota((ch, ch), 1)
    b = _dot(jnp.where(row >= col, 1.0, 0.0), la, HIGHEST)
    sub_start = (row >> int(math.log2(GLA_SUB))) << int(math.log2(GLA_SUB))
    base = _dot(jnp.where(col < sub_start, 1.0, 0.0), la, HIGHEST)
    q = q_ref[...] * (GLA_DK ** -0.5)
    k = k_ref[...]
    v16 = v_ref[...].astype(BF16)
    stt = stt_scr[...]

    o = _dot_nt((q * jnp.exp(b)).astype(BF16), stt.astype(BF16))
    q_in = (q * jnp.exp(b - base)).astype(BF16)
    krow = _iota((ch, 1), 0)
    att_rows = []
    for i in range(ch // GLA_SUB):
        lo, hi = i * GLA_SUB, (i + 1) * GLA_SUB
        expo = jnp.where(krow < hi, base[lo:lo + 1] - b, NEG_INF)
        k_in = (k * jnp.exp(expo)).astype(BF16)
        att_rows.append(_dot_nt(q_in[lo:hi], k_in))
    att = jnp.where(row >= col, jnp.concatenate(att_rows, axis=0), 0.0)
    o = o + _dot(att.astype(BF16), v16)

    b_last = b[ch - 1:ch]
    k_dec = (k * jnp.exp(b_last - b)).astype(BF16)
    stt = stt * jnp.exp(b_last) + lax.dot_general(v16, k_dec, (((0,), (0,)), ((), ())),
                                                   preferred_element_type=F32)
    stt_scr[...] = stt
    o_ref[...] = _gla_out_gate(o, r_ref[...], gn_ref[...])

    @pl.when(c == pl.num_programs(2) - 1)
    def _():
        st_ref[...] = stt.T


def _gla_prompt(hc, la, g_norm, n, s):
    ch = GLA_CHUNK if s % GLA_CHUNK == 0 else s
    nc = s // ch
    rows = lambda b, h, c: b * nc + c
    qk_blocks = GLA_HEADS
    return pl.pallas_call(
        _gla_prompt_kernel, grid=(n, GLA_HEADS, nc),
        in_specs=[pl.BlockSpec((ch, GLA_DK), lambda b, h, c: (rows(b, h, c), h)),
                  pl.BlockSpec((ch, GLA_DK), lambda b, h, c: (rows(b, h, c), qk_blocks + h)),
                  pl.BlockSpec((ch, GLA_DV), lambda b, h, c: (rows(b, h, c), GLA_HEADS + h)),
                  pl.BlockSpec((ch, GLA_DV), lambda b, h, c: (rows(b, h, c), 2 * GLA_HEADS + h)),
                  pl.BlockSpec((ch, GLA_DK), lambda b, h, c: (rows(b, h, c), h)),
                  pl.BlockSpec((1, GLA_DV), lambda b, h, c: (0, 0))],
        out_specs=[pl.BlockSpec((ch, GLA_DV), lambda b, h, c: (rows(b, h, c), h)),
                   pl.BlockSpec((None, None, GLA_DK, GLA_DV), lambda b, h, c: (b, h, 0, 0))],
        out_shape=[jax.ShapeDtypeStruct((n * s, GLA_HEADS * GLA_DV), F32),
                   jax.ShapeDtypeStruct((n, GLA_HEADS, GLA_DK, GLA_DV), F32)],
        scratch_shapes=[pltpu.VMEM((GLA_DV, GLA_DK), F32)],
        compiler_params=_params(3), name="gla_prompt")(hc, hc, hc, hc, la, g_norm.reshape(1, GLA_DV))


def _gla_sample_kernel(q_ref, k_ref, la_ref, v_ref, r_ref, gn_ref, s0_ref, o_ref, st_ref):
    for h in range(GLA_HEADS):
        st = jnp.exp(la_ref[h]) * s0_ref[h] + k_ref[h] * v_ref[h]
        st_ref[h] = st
        o = jnp.sum((q_ref[h] * (GLA_DK ** -0.5)) * st, axis=0, keepdims=True)
        o_ref[h] = _gla_out_gate(o, r_ref[h], gn_ref[...])


def _gla_sample(q, k, la, v, r, g_norm, s0):
    nb = q.shape[0]
    col = pl.BlockSpec((None, GLA_HEADS, GLA_DK, 1), lambda b: (b, 0, 0, 0))
    rowv = pl.BlockSpec((None, GLA_HEADS, 1, GLA_DV), lambda b: (b, 0, 0, 0))
    full = pl.BlockSpec((None, GLA_HEADS, GLA_DK, GLA_DV), lambda b: (b, 0, 0, 0))
    return pl.pallas_call(
        _gla_sample_kernel, grid=(nb,),
        in_specs=[col, col, col, rowv, rowv, pl.BlockSpec((1, GLA_DV), lambda b: (0, 0)), full],
        out_specs=[rowv, full],
        out_shape=[jax.ShapeDtypeStruct((nb, GLA_HEADS, 1, GLA_DV), F32),
                   jax.ShapeDtypeStruct((nb, GLA_HEADS, GLA_DK, GLA_DV), F32)],
        compiler_params=_params(1), name="gla_sample")(q, k, la, v, r, g_norm.reshape(1, GLA_DV), s0)


def _router_kernel(x_ref, w_ref, b_ref, idx_ref, gate_ref):
    logits = _dot(x_ref[...], w_ref[...], HIGHEST) + b_ref[...]
    lane = _iota(logits.shape, 1).astype(F32)
    logits = jnp.where(lane < N_EXPERTS, logits, NEG_INF)
    top_v = jnp.full(logits.shape, NEG_INF, F32)
    top_i = jnp.zeros(logits.shape, F32)
    for kk in range(TOP_K):
        mx = jnp.max(logits, axis=-1, keepdims=True)
        first = jnp.min(jnp.where(logits == mx, lane, 1e9), axis=-1, keepdims=True)
        top_v = jnp.where(lane == kk, mx, top_v)
        top_i = jnp.where(lane == kk, first, top_i)
        logits = jnp.where(lane == first, NEG_INF, logits)
    e = jnp.exp(top_v - jnp.max(top_v, axis=-1, keepdims=True))
    gate_ref[...] = e / jnp.sum(e, axis=-1, keepdims=True)
    idx_ref[...] = top_i.astype(jnp.int32)


def _router(x, w_pad, b_pad, tm):
    t, d = x.shape
    return pl.pallas_call(
        _router_kernel, grid=(t // tm,),
        in_specs=[pl.BlockSpec((tm, d), lambda i: (i, 0)),
                  pl.BlockSpec((d, LANES), lambda i: (0, 0)),
                  pl.BlockSpec((1, LANES), lambda i: (0, 0))],
        out_specs=[pl.BlockSpec((tm, LANES), lambda i: (i, 0))] * 2,
        out_shape=[jax.ShapeDtypeStruct((t, LANES), jnp.int32), jax.ShapeDtypeStruct((t, LANES), F32)],
        compiler_params=_params(1), name="router")(x, w_pad, b_pad)


def _moe_gate_up_kernel(te_ref, tf_ref, tv_ref, x_ref, wg_ref, wu_ref, bg_ref, bu_ref, o_ref, wg16, wu16):
    i = pl.program_id(1)

    @pl.when(tf_ref[i] == 1)
    def _():
        wg16[...] = wg_ref[...].astype(BF16)
        wu16[...] = wu_ref[...].astype(BF16)

    @pl.when(tv_ref[i] == 1)
    def _():
        x = x_ref[...].astype(BF16)
        g = jnp.minimum(_dot(x, wg16[...]) + bg_ref[...], SWIGLU_LIMIT)
        u = jnp.clip(_dot(x, wu16[...]) + bu_ref[...], -SWIGLU_LIMIT, SWIGLU_LIMIT)
        o_ref[...] = ((u + 1.0) * g * _sigmoid(SWIGLU_ALPHA * g)).astype(BF16)

    @pl.when(tv_ref[i] == 0)
    def _():
        o_ref[...] = jnp.zeros(o_ref.shape, BF16)


def _moe_gate_up(te, tf, tv, xs, w_gu, b_gu, layer, tmm, tn):
    p_pad, d = xs.shape
    de = w_gu.shape[-1] // 2
    nj = de // tn
    grid_spec = pltpu.PrefetchScalarGridSpec(
        num_scalar_prefetch=3, grid=(nj, p_pad // tmm),
        in_specs=[pl.BlockSpec((tmm, d), lambda j, i, te, tf, tv: (i, 0)),
                  pl.BlockSpec((None, None, d, tn), lambda j, i, te, tf, tv: (layer, te[i], 0, j)),
                  pl.BlockSpec((None, None, d, tn), lambda j, i, te, tf, tv: (layer, te[i], 0, nj + j)),
                  pl.BlockSpec((None, None, 1, tn), lambda j, i, te, tf, tv: (layer, te[i], 0, j)),
                  pl.BlockSpec((None, None, 1, tn), lambda j, i, te, tf, tv: (layer, te[i], 0, nj + j))],
        out_specs=pl.BlockSpec((tmm, tn), lambda j, i, te, tf, tv: (i, j)),
        scratch_shapes=[pltpu.VMEM((d, tn), BF16), pltpu.VMEM((d, tn), BF16)])
    b4 = b_gu.reshape(b_gu.shape[0], b_gu.shape[1], 1, b_gu.shape[2])
    return pl.pallas_call(
        _moe_gate_up_kernel, grid_spec=grid_spec,
        out_shape=jax.ShapeDtypeStruct((p_pad, de), BF16),
        compiler_params=_params(2), name="moe_gate_up")(te, tf, tv, xs, w_gu, w_gu, b4, b4)


def _moe_down_kernel(te_ref, tf_ref, tv_ref, a_ref, w_ref, b_ref, o_ref, w16):
    i = pl.program_id(1)

    @pl.when(tf_ref[i] == 1)
    def _():
        w16[...] = w_ref[...].astype(BF16)

    @pl.when(tv_ref[i] == 1)
    def _():
        o_ref[...] = _dot(a_ref[...], w16[...]) + b_ref[...]

    @pl.when(tv_ref[i] == 0)
    def _():
        o_ref[...] = jnp.zeros(o_ref.shape, F32)


def _moe_down(te, tf, tv, a, w_d, b_d, layer, tmm, tn):
    p_pad, de = a.shape
    d = w_d.shape[-1]
    grid_spec = pltpu.PrefetchScalarGridSpec(
        num_scalar_prefetch=3, grid=(d // tn, p_pad // tmm),
        in_specs=[pl.BlockSpec((tmm, de), lambda j, i, te, tf, tv: (i, 0)),
                  pl.BlockSpec((None, None, de, tn), lambda j, i, te, tf, tv: (layer, te[i], 0, j)),
                  pl.BlockSpec((None, None, 1, tn), lambda j, i, te, tf, tv: (layer, te[i], 0, j))],
        out_specs=pl.BlockSpec((tmm, tn), lambda j, i, te, tf, tv: (i, j)),
        scratch_shapes=[pltpu.VMEM((de, tn), BF16)])
    b4 = b_d.reshape(b_d.shape[0], b_d.shape[1], 1, b_d.shape[2])
    return pl.pallas_call(
        _moe_down_kernel, grid_spec=grid_spec,
        out_shape=jax.ShapeDtypeStruct((p_pad, d), F32),
        compiler_params=_params(2), name="moe_down")(te, tf, tv, a, w_d, b4)


def _moe_combine_ln_kernel(*refs):
    y_refs = refs[:TOP_K]
    gate_ref, x_ref, g_ref, b_ref, o_ref, o16_ref = refs[TOP_K:]
    gate = gate_ref[...]
    out = gate[:, 0:1] * y_refs[0][...]
    for kk in range(1, TOP_K):
        out = out + gate[:, kk:kk + 1] * y_refs[kk][...]
    res = _layernorm_rows(DEEPNORM_ALPHA * x_ref[...] + out, g_ref[...], b_ref[...])
    o_ref[...] = res
    o16_ref[...] = res.astype(BF16)


def _moe_combine_ln(yk, gate, x, g, b, tm):
    t, d = x.shape
    nt = t // tm
    slot = lambda kk: pl.BlockSpec((tm, d), lambda i: (kk * nt + i, 0))
    return pl.pallas_call(
        _moe_combine_ln_kernel, grid=(nt,),
        in_specs=[slot(kk) for kk in range(TOP_K)]
        + [pl.BlockSpec((tm, LANES), lambda i: (i, 0)),
           pl.BlockSpec((tm, d), lambda i: (i, 0)),
           pl.BlockSpec((1, d), lambda i: (0, 0)), pl.BlockSpec((1, d), lambda i: (0, 0))],
        out_specs=[pl.BlockSpec((tm, d), lambda i: (i, 0))] * 2,
        out_shape=[jax.ShapeDtypeStruct((t, d), F32), jax.ShapeDtypeStruct((t, d), BF16)],
        compiler_params=_params(1), name="moe_combine_ln")(
            *([yk] * TOP_K), gate, x, g.reshape(1, d), b.reshape(1, d))


MOE_TILE_ROWS = 256
MOE_TILE_COLS = 512
MOE_DOWN_TILE_COLS = 1024
ROW_TILE = 128


def _moe(x, x16, layer, router_w, router_b, w_gu, b_gu, w_d, b_d, ln_g, ln_b):
    t, d = x.shape
    w_pad = jnp.pad(router_w[layer], ((0, 0), (0, LANES - N_EXPERTS)))
    b_pad = jnp.pad(router_b[layer], (0, LANES - N_EXPERTS)).reshape(1, LANES)
    top_i, gate = _router(x, w_pad, b_pad, ROW_TILE)

    n_pairs = t * TOP_K
    n_tiles = n_pairs // MOE_TILE_ROWS + N_EXPERTS
    e_flat = top_i[:, :TOP_K].reshape(n_pairs)
    onehot = (e_flat[:, None] == jnp.arange(N_EXPERTS, dtype=jnp.int32)[None, :]).astype(jnp.int32)
    csum = jnp.cumsum(onehot, axis=0)
    counts = csum[-1]
    rank = jnp.sum(onehot * csum, axis=1) - 1
    tiles_per = (counts + MOE_TILE_ROWS - 1) // MOE_TILE_ROWS
    tile_end = jnp.cumsum(tiles_per)
    dest = ((tile_end - tiles_per) * MOE_TILE_ROWS)[e_flat] + rank
    row_token = jnp.zeros((n_tiles * MOE_TILE_ROWS,), jnp.int32).at[dest].set(
        jnp.arange(n_pairs, dtype=jnp.int32) // TOP_K)
    tile_id = jnp.arange(n_tiles, dtype=jnp.int32)
    tv = (tile_id < tile_end[-1]).astype(jnp.int32)
    te = jnp.searchsorted(tile_end, jnp.minimum(tile_id, tile_end[-1] - 1), side="right").astype(jnp.int32)
    tf = jnp.concatenate([jnp.ones((1,), jnp.int32), (te[1:] != te[:-1]).astype(jnp.int32)])

    xs = x[row_token]
    act = _moe_gate_up(te, tf, tv, xs, w_gu, b_gu, layer, MOE_TILE_ROWS, MOE_TILE_COLS)
    ys = _moe_down(te, tf, tv, act, w_d, b_d, layer, MOE_TILE_ROWS, MOE_DOWN_TILE_COLS)
    yk = ys[dest.reshape(t, TOP_K).T.reshape(n_pairs)]
    return _moe_combine_ln(yk, gate, x, ln_g, ln_b, ROW_TILE)


PROJ_ROW_TILE = 640
PROJ_A_COL_TILE = 512
PROJ_C_COL_TILE = 896
OUT_ROW_TILE = 320
FOX_Q_TILE = 256
NSA_Q_TILE = 128


def _attention_layer(x, x16, i, n, s, nb, caches, page_table, w_in, b_f, w_out, rel_table, ln_g, ln_b):
    (cache_fox_k, cache_fox_v, cache_fox_logf, cache_cmp_k, cache_cmp_v, cache_sel_k, cache_sel_v,
     cache_win_k, cache_win_v) = caches
    tp = n * s
    past = page_table.shape[1] * PAGE_SIZE
    win = cache_win_k.shape[2]
    o_f = 3 * FOX_W
    o_n = o_f + FOX_HEADS
    o_g = o_n + NSA_W + 6 * NSA_KV_W
    w_main = jnp.concatenate([w_in[:, :o_f], w_in[:, o_n:o_g]], axis=1).astype(BF16)
    n_small = FOX_HEADS + 3 * NSA_HEADS
    w_small = jnp.pad(jnp.concatenate([w_in[:, o_f:o_n], w_in[:, o_g:]], axis=1), ((0, 0), (0, LANES - n_small)))
    b_small = jnp.pad(b_f, (0, LANES - FOX_HEADS))

    h = _mm(x16, w_main, PROJ_ROW_TILE, PROJ_A_COL_TILE, "proj_a")
    logf_all, logf_t, sg_all = _gates_a(x, w_small, b_small, ROW_TILE)
    logf = logf_all[:, :FOX_HEADS]
    gates = sg_all[:, FOX_HEADS:n_small]

    c_row = _cumsum_seq(logf_t, n, s)
    o_fox_p = _fox_prompt(h, c_row.reshape(n, FOX_HEADS, s, 1), c_row.reshape(n, FOX_HEADS, 1, s), n, s,
                          0, FOX_HEADS, 2 * FOX_HEADS, FOX_Q_TILE)
    tiles, cmpb, bk, bw, bc = _nsa_bias_tables(rel_table, s, NSA_Q_TILE, past, win)
    gates_g = gates.reshape(-1, NSA_GROUPS, 3 * NSA_HPG).transpose(1, 0, 2)
    col_nsa = 3 * FOX_W
    o_nsa_p = _nsa_prompt(h, gates_g, tiles, cmpb, n, s, col_nsa // (NSA_HPG * HEAD_DIM),
                          (col_nsa + NSA_W) // HEAD_DIM, NSA_Q_TILE)

    hs = h[tp:]
    seg = lambda lo, w: hs[:, lo:lo + w]
    q_fox, k_fox, v_fox = seg(0, FOX_W), seg(FOX_W, FOX_W), seg(2 * FOX_W, FOX_W)
    q_nsa = seg(col_nsa, NSA_W).reshape(nb, NSA_HEADS, HEAD_DIM)
    kv = [seg(col_nsa + NSA_W + j * NSA_KV_W, NSA_KV_W).reshape(nb, NSA_GROUPS, HEAD_DIM) for j in range(6)]
    rows_view = lambda c: c.reshape(c.shape[0], c.shape[1], c.shape[2] * c.shape[3], HEAD_DIM)
    heads = lambda a: a.reshape(nb, FOX_HEADS, HEAD_DIM)
    o_fox_s = _fox_sample(page_table, i, heads(q_fox), heads(k_fox), heads(v_fox),
                          logf[tp:].reshape(nb, FOX_HEADS, 1),
                          rows_view(cache_fox_k), rows_view(cache_fox_v),
                          jnp.swapaxes(cache_fox_logf, 2, 3))
    o_c, sel = _nsa_sample_cmp(page_table, i, q_nsa, rows_view(cache_cmp_k), rows_view(cache_cmp_v), bc, past)
    o_nsa_s = _nsa_sample_sel(
        page_table, i, q_nsa, kv[2], kv[3], kv[4], kv[5], sel, o_c,
        gates[tp:].reshape(nb, NSA_HEADS, 3), rows_view(cache_sel_k), rows_view(cache_sel_v),
        rows_view(cache_win_k), rows_view(cache_win_v), bk, bw, rel_table[0].reshape(NSA_HEADS, 1))

    o_fox = jnp.concatenate([o_fox_p, o_fox_s.reshape(nb, FOX_W)], axis=0)
    o_nsa = jnp.concatenate([o_nsa_p, o_nsa_s.reshape(nb, NSA_W)], axis=0)
    w_out16 = w_out.astype(BF16)
    x_new, x_new16 = _mm_res_ln([o_fox, o_nsa], [w_out16[:FOX_W], w_out16[FOX_W:]], x, ln_g, ln_b,
                                OUT_ROW_TILE, "attn_out_ln")

    hp = h[:tp].reshape(n, s, -1)
    wb = min(NSA_WINDOW, s)

    def state(lo, heads, keep=None):
        p = hp[:, :, lo:lo + heads * HEAD_DIM]
        if keep is not None:
            p = p[:, s - keep:]
        return (p.reshape(n, p.shape[1], heads, HEAD_DIM),
                hs[:, lo:lo + heads * HEAD_DIM].reshape(nb, 1, heads, HEAD_DIM))

    kv_lo = col_nsa + NSA_W
    states = [state(FOX_W, FOX_HEADS), state(2 * FOX_W, FOX_HEADS),
              (logf[:tp].reshape(n, s, FOX_HEADS), logf[tp:].reshape(nb, 1, FOX_HEADS))]
    states += [state(kv_lo + j * NSA_KV_W, NSA_GROUPS) for j in range(4)]
    states += [state(kv_lo + j * NSA_KV_W, NSA_GROUPS, wb) for j in (4, 5)]
    return x_new, x_new16, states


def _gla_layer(x, x16, i, n, s, nb, state_gla, w_in, w_g2, b_g, g_norm, w_out, ln_g, ln_b):
    tp = n * s
    qk_w = GLA_HEADS * GLA_DK
    v_w = GLA_HEADS * GLA_DV
    o_g = 2 * qk_w + v_w
    o_r = o_g + GLA_GATE_RANK
    w_main = jnp.concatenate([w_in[:, :o_g], w_in[:, o_r:], w_in[:, o_g:o_r],
                              jnp.zeros((w_in.shape[0], LANES - GLA_GATE_RANK), w_in.dtype)], axis=1).astype(BF16)
    hc = _mm(x16, w_main, PROJ_ROW_TILE, PROJ_C_COL_TILE, "proj_c")
    w_g2_pad = jnp.pad(w_g2, ((0, LANES - GLA_GATE_RANK), (0, 0))).astype(BF16)
    la = _gla_gate(hc, (o_g + v_w) // LANES, w_g2_pad, b_g, PROJ_ROW_TILE)

    og_p, st_p = _gla_prompt(hc, la, g_norm, n, s)
    hs = hc[tp:]
    col = lambda a: a.reshape(nb, GLA_HEADS, GLA_DK, 1)
    rowv = lambda a: a.reshape(nb, GLA_HEADS, 1, GLA_DV)
    og_s, st_s = _gla_sample(col(hs[:, :qk_w]), col(hs[:, qk_w:2 * qk_w]), col(la[tp:]),
                             rowv(hs[:, 2 * qk_w:o_g]), rowv(hs[:, o_g:o_g + v_w]), g_norm, state_gla[i])
    og = jnp.concatenate([og_p, og_s.reshape(nb, v_w)], axis=0)
    x_new, x_new16 = _mm_res_ln([og], [w_out.astype(BF16)], x, ln_g, ln_b, OUT_ROW_TILE, "gla_out_ln")
    return x_new, x_new16, st_p, st_s


def kernel(x_prompt, x_sample, cache_fox_k, cache_fox_v, cache_fox_logf, cache_nsa_cmp_k, cache_nsa_cmp_v,
           cache_nsa_sel_k, cache_nsa_sel_v, cache_nsa_win_k, cache_nsa_win_v, state_gla, page_table,
           w_in_a, b_forget, w_out_a, rel_bias_table, w_in_c, w_gla_gate2, b_gla_gate, gla_norm_g, w_out_c,
           ln_g, ln_b, router_w, router_b, w_gate_up, b_gate_up, w_down, b_down):
    n, s, d = x_prompt.shape
    nb = x_sample.shape[0]
    assert x_sample.shape[1] == 1
    tp = n * s
    x = jnp.concatenate([x_prompt.reshape(tp, d), x_sample.reshape(nb, d)], axis=0)
    x16 = x.astype(BF16)
    caches = (cache_fox_k, cache_fox_v, cache_fox_logf, cache_nsa_cmp_k, cache_nsa_cmp_v,
              cache_nsa_sel_k, cache_nsa_sel_v, cache_nsa_win_k, cache_nsa_win_v)
    attn_states, gla_p, gla_s = [], [], []
    for layer in range(DEPTH):
        i = layer // 2
        if layer % 2 == 0:
            x, x16, st = _attention_layer(x, x16, i, n, s, nb, caches, page_table, w_in_a[i], b_forget[i],
                                          w_out_a[i], rel_bias_table, ln_g[layer, 0], ln_b[layer, 0])
            attn_states.append(st)
        else:
            x, x16, st_p, st_s = _gla_layer(x, x16, i, n, s, nb, state_gla, w_in_c[i], w_gla_gate2[i],
                                            b_gla_gate[i], gla_norm_g[i], w_out_c[i],
                                            ln_g[layer, 0], ln_b[layer, 0])
            gla_p.append(st_p)
            gla_s.append(st_s)
        x, x16 = _moe(x, x16, layer, router_w, router_b, w_gate_up, b_gate_up, w_down, b_down,
                      ln_g[layer, 1], ln_b[layer, 1])
    outs = [x[:tp].reshape(n, s, d), x[tp:].reshape(nb, 1, d)]
    for j in range(9):
        outs.append(jnp.stack([st[j][0] for st in attn_states]))
        outs.append(jnp.stack([st[j][1] for st in attn_states]))
    outs.append(jnp.stack(gla_p))
    outs.append(jnp.stack(gla_s))
    return tuple(outs)
```

```python
import functools
import math

import jax
import jax.numpy as jnp
import numpy as np
from jax import lax
from jax.experimental import pallas as pl
from jax.experimental.pallas import tpu as pltpu

F32 = jnp.float32
BF16 = jnp.bfloat16
HIGHEST = lax.Precision.HIGHEST
NEG_INF = float("-inf")
POS_INF = float("inf")

D_MODEL = 2048
PAGE_SIZE = 128
HEAD_DIM = 128
FOX_HEADS = 8
NSA_HEADS = 8
NSA_GROUPS = 2
NSA_HPG = NSA_HEADS // NSA_GROUPS
NSA_BLOCK = 64
NSA_TOPK = 8
NSA_WINDOW = 512
REL_BUCKETS = 32
REL_MAX_DIST = 1024
GLA_HEADS = 4
GLA_DK = D_MODEL // 2 // GLA_HEADS
GLA_DV = D_MODEL // GLA_HEADS
GLA_GATE_RANK = 16
GLA_TAU = 16.0
GLA_CHUNK = 64
GLA_SUB = 16
N_EXPERTS = 32
TOP_K = 4
SWIGLU_LIMIT = 7.0
SWIGLU_ALPHA = 1.702
LN_EPS = 1e-5
DEPTH = 2
DEEPNORM_ALPHA = (2 * DEPTH) ** 0.25
FOX_W = FOX_HEADS * HEAD_DIM
NSA_W = NSA_HEADS * HEAD_DIM
NSA_KV_W = NSA_GROUPS * HEAD_DIM
ATTN_SCALE = HEAD_DIM ** -0.5

VMEM_LIMIT_BYTES = 56 * 1024 * 1024
LANES = 128

_REL_EXACT = REL_BUCKETS // 2
_REL_THRESHOLDS = tuple(
    (_REL_EXACT * 8) if 2 * j == (REL_BUCKETS - _REL_EXACT) else
    math.ceil(_REL_EXACT * (REL_MAX_DIST / _REL_EXACT) ** (j / (REL_BUCKETS - _REL_EXACT)))
    for j in range(1, REL_BUCKETS - _REL_EXACT))


def _params(n_axes):
    return pltpu.CompilerParams(dimension_semantics=("arbitrary",) * n_axes,
                                vmem_limit_bytes=VMEM_LIMIT_BYTES)


def _iota(shape, dim, dtype=jnp.int32):
    return lax.broadcasted_iota(dtype, shape, dim)


def _shift_div(x, d):
    k = d.bit_length() - 1
    assert d == 1 << k
    return lax.shift_right_logical(x, jnp.full(x.shape, k, jnp.int32))


def _sigmoid(x):
    return 1.0 / (1.0 + jnp.exp(-x))


def _log_sigmoid(x):
    return jnp.minimum(x, 0.0) - jnp.log(1.0 + jnp.exp(-jnp.abs(x)))


def _dot_nt(a, b, precision=None):
    return lax.dot_general(a, b, (((1,), (1,)), ((), ())), precision=precision,
                           preferred_element_type=F32)


def _dot(a, b, precision=None):
    return jnp.dot(a, b, precision=precision, preferred_element_type=F32)


def _masked_softmax_parts(s, mask):
    s = jnp.where(mask, s, NEG_INF)
    m = jnp.max(s, axis=-1, keepdims=True)
    m = jnp.where(m == NEG_INF, 0.0, m)
    e = jnp.exp(s - m)
    return e, jnp.maximum(jnp.sum(e, axis=-1, keepdims=True), 1e-30)


def _online_update(carry, s, v_bf16):
    m, l, acc = carry
    m_new = jnp.maximum(m, jnp.max(s, axis=-1, keepdims=True))
    m_safe = jnp.where(m_new == NEG_INF, 0.0, m_new)
    p = jnp.exp(s - m_safe)
    alpha = jnp.exp(m - m_safe)
    l = alpha * l + jnp.sum(p, axis=-1, keepdims=True)
    acc = alpha * acc + _dot(p.astype(BF16), v_bf16)
    return m_new, l, acc


def _rel_bucket(dist):
    n = jnp.maximum(dist, 0)
    large = jnp.full(n.shape, _REL_EXACT, jnp.int32)
    for t in _REL_THRESHOLDS:
        large = large + jnp.where(n >= t, 1, 0)
    return jnp.where(n < _REL_EXACT, n, large)


def _rel_bias(bucket, table_ref, head):
    out = jnp.zeros(bucket.shape, F32)
    for b in range(REL_BUCKETS):
        out = jnp.where(bucket == b, table_ref[b, head], out)
    return out


def _top_blocks(score, n_sel):
    lane = _iota(score.shape, 1).astype(F32)
    sel = jnp.zeros(score.shape, F32)
    for _ in range(n_sel):
        mx = jnp.max(score, axis=-1, keepdims=True)
        first = jnp.min(jnp.where(score == mx, lane, 1e9), axis=-1, keepdims=True)
        hit = lane == first
        sel = jnp.where(hit & (mx > NEG_INF), 1.0, sel)
        score = jnp.where(hit, NEG_INF, score)
    return sel


def _mm_kernel(x_ref, w_ref, o_ref):
    o_ref[...] = _dot(x_ref[...], w_ref[...])


def _mm(x, w, tm, tn, name):
    m, k = x.shape
    n = w.shape[1]
    assert m % tm == 0 and n % tn == 0
    return pl.pallas_call(
        _mm_kernel, grid=(n // tn, m // tm),
        in_specs=[pl.BlockSpec((tm, k), lambda j, i: (i, 0)),
                  pl.BlockSpec((k, tn), lambda j, i: (0, j))],
        out_specs=pl.BlockSpec((tm, tn), lambda j, i: (i, j)),
        out_shape=jax.ShapeDtypeStruct((m, n), F32),
        compiler_params=_params(2), name=name)(x, w)


def _mm_split_kernel(x_ref, w_ref, *o_refs, plan):
    j = pl.program_id(1)
    res = _dot(x_ref[...], w_ref[...])
    for j_lo, j_hi, outs in plan:
        @pl.when((j >= j_lo) & (j < j_hi))
        def _(outs=outs):
            for oi, lo, hi in outs:
                o_refs[oi][...] = res[:, lo:hi]


def _mm_split(x, m, w, groups, tm, tn, name):
    k = x.shape[1]
    assert m % tm == 0 and sum(sum(g) for g in groups) == w.shape[1]
    plan, out_specs, out_shapes = [], [], []
    j0 = 0
    for g in groups:
        nblk = sum(g) // tn
        assert sum(g) == nblk * tn and (len(g) == 1 or nblk == 1)
        outs, lane = [], 0
        for width in g:
            bw = min(width, tn)
            outs.append((len(out_specs), lane, lane + bw))
            lane += bw
            out_specs.append(pl.BlockSpec(
                (tm, bw), lambda i, j, j0=j0, nblk=nblk: (i, jnp.clip(j - j0, 0, nblk - 1))))
            out_shapes.append(jax.ShapeDtypeStruct((m, width), F32))
        plan.append((j0, j0 + nblk, outs))
        j0 += nblk
    return pl.pallas_call(
        functools.partial(_mm_split_kernel, plan=plan), grid=(m // tm, j0),
        in_specs=[pl.BlockSpec((tm, k), lambda i, j: (i, 0)),
                  pl.BlockSpec((k, tn), lambda i, j: (0, j))],
        out_specs=out_specs, out_shape=out_shapes,
        compiler_params=_params(2), name=name)(x, w)


def _layernorm_rows(z, g, b):
    mu = jnp.mean(z, axis=-1, keepdims=True)
    zc = z - mu
    var = jnp.mean(zc * zc, axis=-1, keepdims=True)
    return zc * lax.rsqrt(var + LN_EPS) * g + b


def _mm_res_ln_kernel(*refs, n_in):
    a_refs, w_refs = refs[:n_in], refs[n_in:2 * n_in]
    x_ref, g_ref, b_ref, o_ref, o16_ref = refs[2 * n_in:]
    y = _dot(a_refs[0][...].astype(BF16), w_refs[0][...])
    for a_ref, w_ref in zip(a_refs[1:], w_refs[1:]):
        y = y + _dot(a_ref[...].astype(BF16), w_ref[...])
    out = _layernorm_rows(DEEPNORM_ALPHA * x_ref[...] + y, g_ref[...], b_ref[...])
    o_ref[...] = out
    o16_ref[...] = out.astype(BF16)


def _mm_res_ln(acts, weights, x, g, b, tm, name):
    t, d = x.shape
    n_in = len(acts)
    in_specs = [pl.BlockSpec((tm, a.shape[1]), lambda i: (i, 0)) for a in acts]
    in_specs += [pl.BlockSpec(w.shape, lambda i: (0, 0)) for w in weights]
    in_specs += [pl.BlockSpec((tm, d), lambda i: (i, 0)),
                 pl.BlockSpec((1, d), lambda i: (0, 0)), pl.BlockSpec((1, d), lambda i: (0, 0))]
    return pl.pallas_call(
        functools.partial(_mm_res_ln_kernel, n_in=n_in), grid=(t // tm,),
        in_specs=in_specs,
        out_specs=[pl.BlockSpec((tm, d), lambda i: (i, 0))] * 2,
        out_shape=[jax.ShapeDtypeStruct((t, d), F32), jax.ShapeDtypeStruct((t, d), BF16)],
        compiler_params=_params(1), name=name)(*acts, *weights, x, g.reshape(1, d), b.reshape(1, d))


def _gates_a_kernel(x_ref, w_ref, wt_ref, b_ref, bt_ref, logf_ref, logft_ref, sg_ref):
    x = x_ref[...]
    z = _dot(x, w_ref[...], HIGHEST)
    logf_ref[...] = _log_sigmoid(z + b_ref[...])
    sg_ref[...] = _sigmoid(z)
    zt = _dot_nt(wt_ref[...], x, HIGHEST)
    logft_ref[...] = _log_sigmoid(zt + bt_ref[...])[:FOX_HEADS]


def _gates_a(x, w_small, b_small, tm):
    t, d = x.shape
    return pl.pallas_call(
        _gates_a_kernel, grid=(t // tm,),
        in_specs=[pl.BlockSpec((tm, d), lambda i: (i, 0)),
                  pl.BlockSpec((d, LANES), lambda i: (0, 0)),
                  pl.BlockSpec((LANES, d), lambda i: (0, 0)),
                  pl.BlockSpec((1, LANES), lambda i: (0, 0)),
                  pl.BlockSpec((LANES, 1), lambda i: (0, 0))],
        out_specs=[pl.BlockSpec((tm, LANES), lambda i: (i, 0)),
                   pl.BlockSpec((FOX_HEADS, tm), lambda i: (0, i)),
                   pl.BlockSpec((tm, LANES), lambda i: (i, 0))],
        out_shape=[jax.ShapeDtypeStruct((t, LANES), F32), jax.ShapeDtypeStruct((FOX_HEADS, t), F32),
                   jax.ShapeDtypeStruct((t, LANES), F32)],
        compiler_params=_params(1), name="gates_a")(
            x, w_small, w_small.T, b_small.reshape(1, LANES), b_small.reshape(LANES, 1))


def _cumsum_kernel(x_ref, o_ref, *, blk):
    s = x_ref.shape[1]
    upper = jnp.where(_iota((blk, blk), 0) <= _iota((blk, blk), 1), 1.0, 0.0)
    carry = jnp.zeros((x_ref.shape[0], 1), F32)
    for j in range(s // blk):
        c = _dot(x_ref[:, j * blk:(j + 1) * blk], upper, HIGHEST) + carry
        o_ref[:, j * blk:(j + 1) * blk] = c
        carry = c[:, blk - 1:blk]


def _cumsum_seq(xt, n, s):
    h = xt.shape[0]
    return pl.pallas_call(
        functools.partial(_cumsum_kernel, blk=min(256, s)), grid=(n,),
        in_specs=[pl.BlockSpec((h, s), lambda i: (0, i))],
        out_specs=pl.BlockSpec((None, h, s), lambda i: (i, 0, 0)),
        out_shape=jax.ShapeDtypeStruct((n, h, s), F32),
        compiler_params=_params(1), name="fox_cumsum")(xt)


def _fox_prompt_kernel(q_ref, k_ref, v_ref, cq_ref, ck_ref, o_ref, *, tq):
    qi = pl.program_id(2)
    q = q_ref[...].astype(BF16)
    cq = cq_ref[...]
    qpos = qi * tq + _iota((tq, tq), 0)

    def body(kb, carry):
        start = pl.multiple_of(kb * tq, tq)
        k = k_ref[pl.ds(start, tq), :].astype(BF16)
        v = v_ref[pl.ds(start, tq), :].astype(BF16)
        s = _dot_nt(q, k) * ATTN_SCALE + cq - ck_ref[:, pl.ds(start, tq)]
        s = jnp.where(start + _iota((tq, tq), 1) <= qpos, s, NEG_INF)
        return _online_update(carry, s, v)

    init = (jnp.full((tq, 1), NEG_INF, F32), jnp.zeros((tq, 1), F32), jnp.zeros((tq, HEAD_DIM), F32))
    _, l, acc = lax.fori_loop(0, qi + 1, body, init)
    o_ref[...] = acc / jnp.maximum(l, 1e-30)


def _fox_prompt(q, k, v, c_col, c_row, n, s, tq):
    nq = s // tq
    kv_spec = pl.BlockSpec((s, HEAD_DIM), lambda b, hd, qi: (b, hd))
    return pl.pallas_call(
        functools.partial(_fox_prompt_kernel, tq=tq), grid=(n, FOX_HEADS, nq),
        in_specs=[pl.BlockSpec((tq, HEAD_DIM), lambda b, hd, qi: (b * nq + qi, hd)),
                  kv_spec, kv_spec,
                  pl.BlockSpec((None, None, tq, 1), lambda b, hd, qi: (b, hd, qi, 0)),
                  pl.BlockSpec((None, None, 1, s), lambda b, hd, qi: (b, hd, 0, 0))],
        out_specs=pl.BlockSpec((tq, HEAD_DIM), lambda b, hd, qi: (b * nq + qi, hd)),
        out_shape=jax.ShapeDtypeStruct((n * s, FOX_W), F32),
        compiler_params=_params(3), name="fox_prompt")(q, k, v, c_col, c_row)


def _nsa_bias_kernel(table_ref, tiles_ref, cmpb_ref, bk_ref, bw_ref, bc_ref, *, tq, past):
    d = pl.program_id(0)
    dist = d * tq + _iota((tq, tq), 0) - _iota((tq, tq), 1)
    bucket = _rel_bucket(dist)
    nb = cmpb_ref.shape[-1]
    dist_c = jnp.minimum(d, pl.num_programs(0) - 2) * tq + _iota((tq, nb), 0) \
        - (_iota((tq, nb), 1) * NSA_BLOCK + NSA_BLOCK - 1)
    bucket_c = _rel_bucket(dist_c)
    for h in range(NSA_HEADS):
        tiles_ref[h] = _rel_bias(bucket, table_ref, h)
        cmpb_ref[h] = _rel_bias(bucket_c, table_ref, h)

    @pl.when(d == 0)
    def _():
        win = bw_ref.shape[1] // NSA_GROUPS
        bucket_k = _rel_bucket(past - _shift_div(_iota((1, bk_ref.shape[1]), 1), NSA_GROUPS))
        bucket_w = _rel_bucket(win - _shift_div(_iota((1, bw_ref.shape[1]), 1), NSA_GROUPS))
        bucket_b = _rel_bucket(past - (_iota((1, LANES), 1) * NSA_BLOCK + NSA_BLOCK - 1))
        for h in range(NSA_HEADS):
            bk_ref[h:h + 1, :] = _rel_bias(bucket_k, table_ref, h)
            bw_ref[h:h + 1, :] = _rel_bias(bucket_w, table_ref, h)
            bc_ref[h:h + 1, :] = _rel_bias(bucket_b, table_ref, h)


def _nsa_bias_tables(rel_table, s, tq, past, win):
    nq = s // tq
    nb = LANES
    assert s // NSA_BLOCK <= LANES
    return pl.pallas_call(
        functools.partial(_nsa_bias_kernel, tq=tq, past=past),
        grid=(nq + 1,),
        in_specs=[pl.BlockSpec(memory_space=pltpu.SMEM)],
        out_specs=[pl.BlockSpec((None, NSA_HEADS, tq, tq), lambda d: (d, 0, 0, 0)),
                   pl.BlockSpec((None, NSA_HEADS, tq, nb), lambda d: (jnp.minimum(d, nq - 1), 0, 0, 0)),
                   pl.BlockSpec((NSA_HEADS, NSA_GROUPS * past), lambda d: (0, 0)),
                   pl.BlockSpec((NSA_HEADS, NSA_GROUPS * win), lambda d: (0, 0)),
                   pl.BlockSpec((NSA_HEADS, LANES), lambda d: (0, 0))],
        out_shape=[jax.ShapeDtypeStruct((nq + 1, NSA_HEADS, tq, tq), F32),
                   jax.ShapeDtypeStruct((nq, NSA_HEADS, tq, nb), F32),
                   jax.ShapeDtypeStruct((NSA_HEADS, NSA_GROUPS * past), F32),
                   jax.ShapeDtypeStruct((NSA_HEADS, NSA_GROUPS * win), F32),
                   jax.ShapeDtypeStruct((NSA_HEADS, LANES), F32)],
        compiler_params=_params(1), name="nsa_bias")(rel_table)


def _nsa_prompt_kernel(q_ref, kc_ref, vc_ref, ks_ref, vs_ref, kw_ref, vw_ref, gate_ref, tiles_ref,
                       cmpb_ref, o_ref, *, tq):
    qi = pl.program_id(2)
    s_len = kc_ref.shape[0]
    n_blk = s_len // NSA_BLOCK
    nb = LANES
    rows = NSA_HPG * tq
    q = jnp.concatenate([q_ref[:, h * HEAD_DIM:(h + 1) * HEAD_DIM] for h in range(NSA_HPG)],
                        axis=0).astype(BF16)
    stack = lambda ref, idx: jnp.concatenate([ref[idx, h] for h in range(NSA_HPG)], axis=0)

    def block_means(ref):
        m = jnp.mean(ref[...].reshape(n_blk, NSA_BLOCK, HEAD_DIM), axis=1)
        return jnp.concatenate([m, jnp.zeros((nb - n_blk, HEAD_DIM), F32)], axis=0).astype(BF16)

    qpos_r = qi * tq + (_iota((rows, nb), 0) & (tq - 1))
    blk_r = _iota((rows, nb), 1)
    bias_c = jnp.concatenate([cmpb_ref[h] for h in range(NSA_HPG)], axis=0)
    s_c = _dot_nt(q, block_means(kc_ref)) * ATTN_SCALE + bias_c
    e_c, den_c = _masked_softmax_parts(
        s_c, (blk_r < n_blk) & (qpos_r >= blk_r * NSA_BLOCK + NSA_BLOCK - 1))
    p_c = e_c / den_c
    o_c = _dot(p_c.astype(BF16), block_means(vc_ref))

    p_sum = p_c[0:tq]
    for h in range(1, NSA_HPG):
        p_sum = p_sum + p_c[h * tq:(h + 1) * tq]
    blk = _iota((tq, nb), 1)
    cur = _shift_div(qi * tq + _iota((tq, nb), 0), NSA_BLOCK)
    forced = (blk == 0) | (blk == cur) | (blk == cur - 1)
    score = jnp.where(forced, POS_INF, jnp.where(blk <= cur, p_sum, NEG_INF))
    sel = _top_blocks(score, min(NSA_TOPK, n_blk)).astype(BF16)

    row_q = _iota((rows, tq), 0) & (tq - 1)
    col_k = _iota((rows, tq), 1)
    init = (jnp.full((rows, 1), NEG_INF, F32), jnp.zeros((rows, 1), F32), jnp.zeros((rows, HEAD_DIM), F32))

    def sel_body(kt, carry):
        start = pl.multiple_of(kt * tq, tq)
        delta = qi - kt
        k = ks_ref[pl.ds(start, tq), :].astype(BF16)
        v = vs_ref[pl.ds(start, tq), :].astype(BF16)
        s = _dot_nt(q, k) * ATTN_SCALE + stack(tiles_ref, delta)
        expand = jnp.where(_iota((nb, tq), 0) == _shift_div(start + _iota((nb, tq), 1), NSA_BLOCK), 1.0, 0.0)
        picked = _dot(sel, expand.astype(BF16))
        picked = jnp.concatenate([picked] * NSA_HPG, axis=0)
        dist = delta * tq + row_q - col_k
        s = jnp.where((picked > 0.5) & (dist >= 0), s, NEG_INF)
        return _online_update(carry, s, v)

    _, l_s, acc_s = lax.fori_loop(0, qi + 1, sel_body, init)
    o_s = acc_s / jnp.maximum(l_s, 1e-30)

    def win_body(kt, carry):
        start = pl.multiple_of(kt * tq, tq)
        delta = qi - kt
        k = kw_ref[pl.ds(start, tq), :].astype(BF16)
        v = vw_ref[pl.ds(start, tq), :].astype(BF16)
        s = _dot_nt(q, k) * ATTN_SCALE + stack(tiles_ref, delta)
        dist = delta * tq + row_q - col_k
        s = jnp.where((dist >= 0) & (dist <= NSA_WINDOW), s, NEG_INF)
        return _online_update(carry, s, v)

    first = jnp.maximum(qi - (NSA_WINDOW + tq - 1) // tq, 0)
    _, l_w, acc_w = lax.fori_loop(first, qi + 1, win_body, init)
    o_w = acc_w / jnp.maximum(l_w, 1e-30)

    gate = gate_ref[...]
    for h in range(NSA_HPG):
        r = slice(h * tq, (h + 1) * tq)
        o_ref[:, h * HEAD_DIM:(h + 1) * HEAD_DIM] = (
            gate[:, 3 * h:3 * h + 1] * o_c[r] + gate[:, 3 * h + 1:3 * h + 2] * o_s[r]
            + gate[:, 3 * h + 2:3 * h + 3] * o_w[r])


def _nsa_prompt(q, kv, gates, tiles, cmpb, n, s, tq):
    nq = s // tq
    grp_w = NSA_HPG * HEAD_DIM
    kv_spec = pl.BlockSpec((s, HEAD_DIM), lambda b, g, qi: (b, g))
    return pl.pallas_call(
        functools.partial(_nsa_prompt_kernel, tq=tq), grid=(n, NSA_GROUPS, nq),
        in_specs=[pl.BlockSpec((tq, grp_w), lambda b, g, qi: (b * nq + qi, g))]
        + [kv_spec] * 6
        + [pl.BlockSpec((None, tq, 3 * NSA_HPG), lambda b, g, qi: (g, b * nq + qi, 0)),
           pl.BlockSpec((nq + 1, NSA_HPG, tq, tq), lambda b, g, qi: (0, g, 0, 0)),
           pl.BlockSpec((None, NSA_HPG, tq, LANES), lambda b, g, qi: (qi, g, 0, 0))],
        out_specs=pl.BlockSpec((tq, grp_w), lambda b, g, qi: (b * nq + qi, g)),
        out_shape=jax.ShapeDtypeStruct((n * s, NSA_W), F32),
        compiler_params=_params(3), name="nsa_prompt")(q, *kv, gates, tiles, cmpb)


def _softmax_step(scr, s_parts, v_refs):
    m_scr, l_scr, acc_scr = scr
    s = jnp.concatenate(s_parts, axis=1)
    m = m_scr[...]
    m_new = jnp.maximum(m, jnp.max(s, axis=-1, keepdims=True))
    m_safe = jnp.where(m_new == NEG_INF, 0.0, m_new)
    p = jnp.exp(s - m_safe)
    alpha = jnp.exp(m - m_safe)
    l_scr[...] = alpha * l_scr[...] + jnp.sum(p, axis=-1, keepdims=True)
    acc = alpha * acc_scr[...]
    width = s_parts[0].shape[1]
    for i, v_ref in enumerate(v_refs):
        acc = acc + _dot(p[:, i * width:(i + 1) * width].astype(BF16), v_ref[...].astype(BF16))
    acc_scr[...] = acc
    m_scr[...] = m_new


def _fox_sample_kernel(pt_ref, q_ref, kn_ref, vn_ref, lfn_ref, later_ref, *rest, pp):
    k_refs, v_refs, lf_refs = rest[:pp], rest[pp:2 * pp], rest[2 * pp:3 * pp]
    o_ref, m_scr, l_scr, acc_scr, suf_scr = rest[3 * pp:]
    step = pl.program_id(1)
    q = q_ref[...]

    @pl.when(step == 0)
    def _():
        m_scr[...] = jnp.sum(q * kn_ref[...], axis=-1, keepdims=True) * ATTN_SCALE
        l_scr[...] = jnp.ones(l_scr.shape, F32)
        acc_scr[...] = vn_ref[...].astype(BF16).astype(F32)
        suf_scr[...] = lfn_ref[...]

    cols = k_refs[0].shape[0]
    own = (_iota((FOX_HEADS, cols), 1) & (FOX_HEADS - 1)) == _iota((FOX_HEADS, cols), 0)
    q16 = q.astype(BF16)
    later = later_ref[...]
    suf = suf_scr[...]
    s_parts = []
    for k_ref, lf_ref in zip(k_refs, lf_refs):
        lf = lf_ref[...]
        lf_hi = lf.astype(BF16)
        lf_lo = (lf - lf_hi.astype(F32)).astype(BF16)
        decay = _dot(lf_hi, later) + _dot(lf_lo, later) + suf
        s = _dot_nt(q16, k_ref[...].astype(BF16)) * ATTN_SCALE + decay
        s_parts.append(jnp.where(own, s, NEG_INF))
        suf = suf + jnp.sum(lf, axis=-1, keepdims=True)
    suf_scr[...] = suf
    _softmax_step((m_scr, l_scr, acc_scr), s_parts, v_refs)

    @pl.when(step == pl.num_programs(1) - 1)
    def _():
        o_ref[...] = acc_scr[...] / jnp.maximum(l_scr[...], 1e-30)


FOX_PAGES_PER_STEP = 4
NSA_PAGES_PER_STEP = 8


def _fox_sample(page_table, layer, q, k_new, v_new, logf_new, cache_k, cache_v, cache_logf_t):
    nb, n_pages = page_table.shape
    pp = FOX_PAGES_PER_STEP
    assert n_pages % pp == 0
    rows = PAGE_SIZE * FOX_HEADS
    later = np.repeat(np.tril(np.ones((PAGE_SIZE, PAGE_SIZE), np.float32), -1), FOX_HEADS, axis=1)
    per_b = lambda shape: pl.BlockSpec((None,) + shape, lambda b, p, pt: (b, 0, 0))
    page = lambda i: (lambda b, p, pt: (layer, pt[b, n_pages - 1 - (p * pp + i)], 0, 0))
    grid_spec = pltpu.PrefetchScalarGridSpec(
        num_scalar_prefetch=1, grid=(nb, n_pages // pp),
        in_specs=[per_b((FOX_HEADS, HEAD_DIM)), per_b((FOX_HEADS, HEAD_DIM)), per_b((FOX_HEADS, HEAD_DIM)),
                  per_b((FOX_HEADS, 1)),
                  pl.BlockSpec((PAGE_SIZE, rows), lambda b, p, pt: (0, 0))]
        + [pl.BlockSpec((None, None, rows, HEAD_DIM), page(i)) for i in range(pp)]
        + [pl.BlockSpec((None, None, rows, HEAD_DIM), page(i)) for i in range(pp)]
        + [pl.BlockSpec((None, None, FOX_HEADS, PAGE_SIZE), page(i)) for i in range(pp)],
        out_specs=per_b((FOX_HEADS, HEAD_DIM)),
        scratch_shapes=[pltpu.VMEM((FOX_HEADS, 1), F32), pltpu.VMEM((FOX_HEADS, 1), F32),
                        pltpu.VMEM((FOX_HEADS, HEAD_DIM), F32), pltpu.VMEM((FOX_HEADS, 1), F32)])
    return pl.pallas_call(
        functools.partial(_fox_sample_kernel, pp=pp), grid_spec=grid_spec,
        out_shape=jax.ShapeDtypeStruct((nb, FOX_HEADS, HEAD_DIM), F32),
        compiler_params=_params(2), name="fox_sample")(
            page_table, q, k_new, v_new, logf_new, jnp.asarray(later, BF16),
            *([cache_k] * pp), *([cache_v] * pp), *([cache_logf_t] * pp))


def _group_block_sums(x):
    sub = 8
    y = jnp.sum(x.reshape(x.shape[0] // sub, sub, x.shape[1]), axis=0)
    shift = NSA_GROUPS
    while shift < sub:
        y = y + pltpu.roll(y, shift, 0)
        shift *= 2
    return y


def _nsa_sample_cmp_kernel(pt_ref, q_ref, bc_ref, *rest, past, pp):
    k_refs, v_refs = rest[:pp], rest[pp:2 * pp]
    oc_ref, sel_ref, kcb_scr, vcb_scr = rest[2 * pp:]
    step = pl.program_id(1)
    n_blk = past // NSA_BLOCK
    per_page = PAGE_SIZE // NSA_BLOCK
    blk_rows = NSA_BLOCK * NSA_GROUPS

    @pl.when(step == 0)
    def _():
        kcb_scr[...] = jnp.zeros(kcb_scr.shape, F32)
        vcb_scr[...] = jnp.zeros(vcb_scr.shape, F32)

    row_id = _iota((LANES, HEAD_DIM), 0)
    for refs, scr in ((k_refs, kcb_scr), (v_refs, vcb_scr)):
        tabs = [scr[g] for g in range(NSA_GROUPS)]
        for i, ref in enumerate(refs):
            for bl in range(per_page):
                sums = _group_block_sums(ref[bl * blk_rows:(bl + 1) * blk_rows, :]) * (1.0 / NSA_BLOCK)
                blk_id = (step * pp + i) * per_page + bl
                for g in range(NSA_GROUPS):
                    tabs[g] = jnp.where(row_id == blk_id, sums[g:g + 1], tabs[g])
        for g in range(NSA_GROUPS):
            scr[g] = tabs[g]

    @pl.when(step == pl.num_programs(1) - 1)
    def _():
        q16 = q_ref[...].astype(BF16)
        blk = _iota((NSA_HEADS, LANES), 1)
        head_grp = _shift_div(_iota((NSA_HEADS, LANES), 0), NSA_HPG)
        by_group = lambda parts: functools.reduce(
            lambda acc, gp: jnp.where(head_grp == gp[0], gp[1], acc), enumerate(parts), jnp.zeros_like(parts[0]))
        s_c = by_group([_dot_nt(q16, kcb_scr[g].astype(BF16)) for g in range(NSA_GROUPS)])
        s_c = s_c * ATTN_SCALE + bc_ref[...]
        e_c, den_c = _masked_softmax_parts(s_c, (blk < n_blk) & (past >= blk * NSA_BLOCK + NSA_BLOCK - 1))
        p_c = e_c / den_c
        p16 = p_c.astype(BF16)
        oc_ref[...] = by_group([_dot(p16, vcb_scr[g].astype(BF16)) for g in range(NSA_GROUPS)])
        p_sum = jnp.zeros((NSA_HEADS, LANES), F32)
        for g in range(NSA_GROUPS):
            tot = jnp.sum(p_c[g * NSA_HPG:(g + 1) * NSA_HPG], axis=0, keepdims=True)
            p_sum = jnp.where(head_grp == g, tot, p_sum)
        cur = past // NSA_BLOCK
        forced = (blk == 0) | (blk == cur) | (blk == cur - 1)
        score = jnp.where(forced, POS_INF, jnp.where(blk <= cur, p_sum, NEG_INF))
        sel_ref[...] = _top_blocks(score, min(NSA_TOPK, cur + 1))


def _nsa_sample_cmp(page_table, layer, q, cache_k, cache_v, bc, past):
    assert past % NSA_BLOCK == 0 and past // NSA_BLOCK < LANES
    nb, n_pages = page_table.shape
    pp = NSA_PAGES_PER_STEP
    assert n_pages % pp == 0
    rows = PAGE_SIZE * NSA_GROUPS
    page = lambda i: (lambda b, p, pt: (layer, pt[b, p * pp + i], 0, 0))
    grid_spec = pltpu.PrefetchScalarGridSpec(
        num_scalar_prefetch=1, grid=(nb, n_pages // pp),
        in_specs=[pl.BlockSpec((None, NSA_HEADS, HEAD_DIM), lambda b, p, pt: (b, 0, 0)),
                  pl.BlockSpec((NSA_HEADS, LANES), lambda b, p, pt: (0, 0))]
        + [pl.BlockSpec((None, None, rows, HEAD_DIM), page(i)) for i in range(pp)] * 2,
        out_specs=[pl.BlockSpec((None, NSA_HEADS, HEAD_DIM), lambda b, p, pt: (b, 0, 0)),
                   pl.BlockSpec((None, NSA_HEADS, LANES), lambda b, p, pt: (b, 0, 0))],
        scratch_shapes=[pltpu.VMEM((NSA_GROUPS, LANES, HEAD_DIM), F32),
                        pltpu.VMEM((NSA_GROUPS, LANES, HEAD_DIM), F32)])
    return pl.pallas_call(
        functools.partial(_nsa_sample_cmp_kernel, past=past, pp=pp), grid_spec=grid_spec,
        out_shape=[jax.ShapeDtypeStruct((nb, NSA_HEADS, HEAD_DIM), F32),
                   jax.ShapeDtypeStruct((nb, NSA_HEADS, LANES), F32)],
        compiler_params=_params(2), name="nsa_sample_cmp")(
            page_table, q, bc, *([cache_k] * pp), *([cache_v] * pp))


def _per_head(new_ref):
    head_grp = _shift_div(_iota((NSA_HEADS, HEAD_DIM), 0), NSA_HPG)
    out = jnp.zeros((NSA_HEADS, HEAD_DIM), F32)
    for g in range(NSA_GROUPS):
        out = jnp.where(head_grp == g, new_ref[g:g + 1, :], out)
    return out


def _nsa_sample_sel_kernel(pt_ref, q_ref, ksn_ref, vsn_ref, kwn_ref, vwn_ref, sel_ref, oc_ref, gate_ref,
                           kw_ref, vw_ref, bk_ref, bw_ref, b0_ref, *rest, pp):
    ks_refs, vs_refs = rest[:pp], rest[pp:2 * pp]
    o_ref, m_scr, l_scr, acc_scr = rest[2 * pp:]
    step = pl.program_id(1)
    per_page = PAGE_SIZE // NSA_BLOCK
    q = q_ref[...]
    q16 = q.astype(BF16)
    b0 = b0_ref[...]
    own = lambda cols: ((_iota((NSA_HEADS, cols), 1) & (NSA_GROUPS - 1))
                        == _shift_div(_iota((NSA_HEADS, cols), 0), NSA_HPG))

    @pl.when(step == 0)
    def _():
        m_scr[...] = jnp.sum(q * _per_head(ksn_ref), axis=-1, keepdims=True) * ATTN_SCALE + b0
        l_scr[...] = jnp.ones(l_scr.shape, F32)
        acc_scr[...] = _per_head(vsn_ref).astype(BF16).astype(F32)

    cols = ks_refs[0].shape[0]
    sel16 = sel_ref[...].astype(BF16)
    key_blk = _shift_div(_iota((LANES, cols), 1), NSA_GROUPS * NSA_BLOCK)
    s_parts = []
    for i, ks_ref in enumerate(ks_refs):
        pg = step * pp + i
        expand = jnp.where(_iota((LANES, cols), 0) == pg * per_page + key_blk, 1.0, 0.0)
        picked = _dot(sel16, expand.astype(BF16))
        start = pl.multiple_of(pg * cols, cols)
        s = _dot_nt(q16, ks_ref[...].astype(BF16)) * ATTN_SCALE + bk_ref[:, pl.ds(start, cols)]
        s_parts.append(jnp.where(own(cols) & (picked > 0.5), s, NEG_INF))
    _softmax_step((m_scr, l_scr, acc_scr), s_parts, vs_refs)

    @pl.when(step == pl.num_programs(1) - 1)
    def _():
        o_s = acc_scr[...] / jnp.maximum(l_scr[...], 1e-30)
        wcols = kw_ref.shape[0]
        s_w = _dot_nt(q16, kw_ref[...].astype(BF16)) * ATTN_SCALE + bw_ref[...]
        s_w = jnp.where(own(wcols), s_w, NEG_INF)
        s_n = jnp.sum(q * _per_head(kwn_ref), axis=-1, keepdims=True) * ATTN_SCALE + b0
        m_w = jnp.maximum(jnp.max(s_w, axis=-1, keepdims=True), s_n)
        e_w = jnp.exp(s_w - m_w)
        e_n = jnp.exp(s_n - m_w)
        den = jnp.maximum(jnp.sum(e_w, axis=-1, keepdims=True) + e_n, 1e-30)
        p_w = (e_w / den).astype(BF16)
        p_n = (e_n / den).astype(BF16).astype(F32)
        o_w = _dot(p_w, vw_ref[...].astype(BF16)) + p_n * _per_head(vwn_ref).astype(BF16).astype(F32)
        gate = gate_ref[...]
        o_ref[...] = gate[:, 0:1] * oc_ref[...] + gate[:, 1:2] * o_s + gate[:, 2:3] * o_w


def _nsa_sample_sel(page_table, layer, q, ks_new, vs_new, kw_new, vw_new, sel, o_c, gates,
                    cache_ks, cache_vs, win_k, win_v, bk, bw, b0):
    nb, n_pages = page_table.shape
    pp = NSA_PAGES_PER_STEP
    assert n_pages % pp == 0
    wrows = win_k.shape[2]
    rows = PAGE_SIZE * NSA_GROUPS
    per_b = lambda shape: pl.BlockSpec((None,) + shape, lambda b, p, pt: (b, 0, 0))
    page = lambda i: (lambda b, p, pt: (layer, pt[b, p * pp + i], 0, 0))
    const = lambda shape: pl.BlockSpec(shape, lambda b, p, pt: (0, 0))
    new = per_b((NSA_GROUPS, HEAD_DIM))
    grid_spec = pltpu.PrefetchScalarGridSpec(
        num_scalar_prefetch=1, grid=(nb, n_pages // pp),
        in_specs=[per_b((NSA_HEADS, HEAD_DIM)), new, new, new, new,
                  per_b((NSA_HEADS, LANES)), per_b((NSA_HEADS, HEAD_DIM)), per_b((NSA_HEADS, 3)),
                  pl.BlockSpec((None, None, wrows, HEAD_DIM), lambda b, p, pt: (layer, b, 0, 0)),
                  pl.BlockSpec((None, None, wrows, HEAD_DIM), lambda b, p, pt: (layer, b, 0, 0)),
                  const((NSA_HEADS, n_pages * rows)), const((NSA_HEADS, wrows)), const((NSA_HEADS, 1))]
        + [pl.BlockSpec((None, None, rows, HEAD_DIM), page(i)) for i in range(pp)] * 2,
        out_specs=per_b((NSA_HEADS, HEAD_DIM)),
        scratch_shapes=[pltpu.VMEM((NSA_HEADS, 1), F32), pltpu.VMEM((NSA_HEADS, 1), F32),
                        pltpu.VMEM((NSA_HEADS, HEAD_DIM), F32)])
    return pl.pallas_call(
        functools.partial(_nsa_sample_sel_kernel, pp=pp), grid_spec=grid_spec,
        out_shape=jax.ShapeDtypeStruct((nb, NSA_HEADS, HEAD_DIM), F32),
        compiler_params=_params(2), name="nsa_sample_sel")(
            page_table, q, ks_new, vs_new, kw_new, vw_new, sel, o_c, gates,
            win_k, win_v, bk, bw, b0, *([cache_ks] * pp), *([cache_vs] * pp))


def _gla_gate_kernel(g_ref, w_ref, b_ref, o_ref):
    z = _dot(g_ref[...].astype(BF16), w_ref[...]) + b_ref[...]
    o_ref[...] = _log_sigmoid(z) * (1.0 / GLA_TAU)


def _gla_gate(hc, col_g, w_g2_pad, b_g, tm):
    t = hc.shape[0]
    n = w_g2_pad.shape[1]
    return pl.pallas_call(
        _gla_gate_kernel, grid=(t // tm,),
        in_specs=[pl.BlockSpec((tm, LANES), lambda i: (i, col_g)),
                  pl.BlockSpec((LANES, n), lambda i: (0, 0)),
                  pl.BlockSpec((1, n), lambda i: (0, 0))],
        out_specs=pl.BlockSpec((tm, n), lambda i: (i, 0)),
        out_shape=jax.ShapeDtypeStruct((t, n), F32),
        compiler_params=_params(1), name="gla_gate")(hc, w_g2_pad, b_g.reshape(1, n))


def _gla_out_gate(o, r, g_norm):
    o = o * lax.rsqrt(jnp.mean(o * o, axis=-1, keepdims=True) + LN_EPS) * g_norm
    return o * (r * _sigmoid(r))


def _gla_prompt_kernel(q_ref, k_ref, v_ref, r_ref, la_ref, gn_ref, o_ref, st_ref, stt_scr):
    c = pl.program_id(2)
    ch = q_ref.shape[0]

    @pl.when(c == 0)
    def _():
        stt_scr[...] = jnp.zeros(stt_scr.shape, F32)

    la = la_ref[...]
    row = _iota((ch, ch), 0)
    col = _iota((ch, ch), 1)
    b = _dot(jnp.where(row >= col, 1.0, 0.0), la, HIGHEST)
    sub_start = (row >> int(math.log2(GLA_SUB))) << int(math.log2(GLA_SUB))
    base = _dot(jnp.where(col < sub_start, 1.0, 0.0), la, HIGHEST)
    q = q_ref[...] * (GLA_DK ** -0.5)
    k = k_ref[...]
    v16 = v_ref[...].astype(BF16)
    stt = stt_scr[...]

    o = _dot_nt((q * jnp.exp(b)).astype(BF16), stt.astype(BF16))
    q_in = (q * jnp.exp(b - base)).astype(BF16)
    krow = _iota((ch, 1), 0)
    att_rows = []
    for i in range(ch // GLA_SUB):
        lo, hi = i * GLA_SUB, (i + 1) * GLA_SUB
        expo = jnp.where(krow < hi, base[lo:lo + 1] - b, NEG_INF)
        k_in = (k * jnp.exp(expo)).astype(BF16)
        att_rows.append(_dot_nt(q_in[lo:hi], k_in))
    att = jnp.where(row >= col, jnp.concatenate(att_rows, axis=0), 0.0)
    o = o + _dot(att.astype(BF16), v16)

    b_last = b[ch - 1:ch]
    k_dec = (k * jnp.exp(b_last - b)).astype(BF16)
    stt = stt * jnp.exp(b_last) + lax.dot_general(v16, k_dec, (((0,), (0,)), ((), ())),
                                                   preferred_element_type=F32)
    stt_scr[...] = stt
    o_ref[...] = _gla_out_gate(o, r_ref[...], gn_ref[...])

    @pl.when(c == pl.num_programs(2) - 1)
    def _():
        st_ref[...] = stt.T


def _gla_prompt(hc, la, g_norm, n, s):
    ch = GLA_CHUNK if s % GLA_CHUNK == 0 else s
    nc = s // ch
    rows = lambda b, h, c: b * nc + c
    qk_blocks = GLA_HEADS
    return pl.pallas_call(
        _gla_prompt_kernel, grid=(n, GLA_HEADS, nc),
        in_specs=[pl.BlockSpec((ch, GLA_DK), lambda b, h, c: (rows(b, h, c), h)),
                  pl.BlockSpec((ch, GLA_DK), lambda b, h, c: (rows(b, h, c), qk_blocks + h)),
                  pl.BlockSpec((ch, GLA_DV), lambda b, h, c: (rows(b, h, c), GLA_HEADS + h)),
                  pl.BlockSpec((ch, GLA_DV), lambda b, h, c: (rows(b, h, c), 2 * GLA_HEADS + h)),
                  pl.BlockSpec((ch, GLA_DK), lambda b, h, c: (rows(b, h, c), h)),
                  pl.BlockSpec((1, GLA_DV), lambda b, h, c: (0, 0))],
        out_specs=[pl.BlockSpec((ch, GLA_DV), lambda b, h, c: (rows(b, h, c), h)),
                   pl.BlockSpec((None, None, GLA_DK, GLA_DV), lambda b, h, c: (b, h, 0, 0))],
        out_shape=[jax.ShapeDtypeStruct((n * s, GLA_HEADS * GLA_DV), F32),
                   jax.ShapeDtypeStruct((n, GLA_HEADS, GLA_DK, GLA_DV), F32)],
        scratch_shapes=[pltpu.VMEM((GLA_DV, GLA_DK), F32)],
        compiler_params=_params(3), name="gla_prompt")(hc, hc, hc, hc, la, g_norm.reshape(1, GLA_DV))


def _gla_sample_kernel(q_ref, k_ref, la_ref, v_ref, r_ref, gn_ref, s0_ref, o_ref, st_ref):
    for h in range(GLA_HEADS):
        st = jnp.exp(la_ref[h]) * s0_ref[h] + k_ref[h] * v_ref[h]
        st_ref[h] = st
        o = jnp.sum((q_ref[h] * (GLA_DK ** -0.5)) * st, axis=0, keepdims=True)
        o_ref[h] = _gla_out_gate(o, r_ref[h], gn_ref[...])


def _gla_sample(q, k, la, v, r, g_norm, s0):
    nb = q.shape[0]
    col = pl.BlockSpec((None, GLA_HEADS, GLA_DK, 1), lambda b: (b, 0, 0, 0))
    rowv = pl.BlockSpec((None, GLA_HEADS, 1, GLA_DV), lambda b: (b, 0, 0, 0))
    full = pl.BlockSpec((None, GLA_HEADS, GLA_DK, GLA_DV), lambda b: (b, 0, 0, 0))
    return pl.pallas_call(
        _gla_sample_kernel, grid=(nb,),
        in_specs=[col, col, col, rowv, rowv, pl.BlockSpec((1, GLA_DV), lambda b: (0, 0)), full],
        out_specs=[rowv, full],
        out_shape=[jax.ShapeDtypeStruct((nb, GLA_HEADS, 1, GLA_DV), F32),
                   jax.ShapeDtypeStruct((nb, GLA_HEADS, GLA_DK, GLA_DV), F32)],
        compiler_params=_params(1), name="gla_sample")(q, k, la, v, r, g_norm.reshape(1, GLA_DV), s0)


def _router_kernel(x_ref, w_ref, b_ref, idx_ref, gate_ref):
    logits = _dot(x_ref[...], w_ref[...], HIGHEST) + b_ref[...]
    lane = _iota(logits.shape, 1).astype(F32)
    logits = jnp.where(lane < N_EXPERTS, logits, NEG_INF)
    top_v = jnp.full(logits.shape, NEG_INF, F32)
    top_i = jnp.zeros(logits.shape, F32)
    for kk in range(TOP_K):
        mx = jnp.max(logits, axis=-1, keepdims=True)
        first = jnp.min(jnp.where(logits == mx, lane, 1e9), axis=-1, keepdims=True)
        top_v = jnp.where(lane == kk, mx, top_v)
        top_i = jnp.where(lane == kk, first, top_i)
        logits = jnp.where(lane == first, NEG_INF, logits)
    e = jnp.exp(top_v - jnp.max(top_v, axis=-1, keepdims=True))
    gate_ref[...] = e / jnp.sum(e, axis=-1, keepdims=True)
    idx_ref[...] = top_i.astype(jnp.int32)


def _router(x, w_pad, b_pad, tm):
    t, d = x.shape
    return pl.pallas_call(
        _router_kernel, grid=(t // tm,),
        in_specs=[pl.BlockSpec((tm, d), lambda i: (i, 0)),
                  pl.BlockSpec((d, LANES), lambda i: (0, 0)),
                  pl.BlockSpec((1, LANES), lambda i: (0, 0))],
        out_specs=[pl.BlockSpec((tm, LANES), lambda i: (i, 0))] * 2,
        out_shape=[jax.ShapeDtypeStruct((t, LANES), jnp.int32), jax.ShapeDtypeStruct((t, LANES), F32)],
        compiler_params=_params(1), name="router")(x, w_pad, b_pad)


def _moe_gate_up_kernel(te_ref, tf_ref, tv_ref, x_ref, wg_ref, wu_ref, bg_ref, bu_ref, o_ref, wg16, wu16):
    i = pl.program_id(1)

    @pl.when(tf_ref[i] == 1)
    def _():
        wg16[...] = wg_ref[...].astype(BF16)
        wu16[...] = wu_ref[...].astype(BF16)

    @pl.when(tv_ref[i] == 1)
    def _():
        x = x_ref[...].astype(BF16)
        g = jnp.minimum(_dot(x, wg16[...]) + bg_ref[...], SWIGLU_LIMIT)
        u = jnp.clip(_dot(x, wu16[...]) + bu_ref[...], -SWIGLU_LIMIT, SWIGLU_LIMIT)
        o_ref[...] = ((u + 1.0) * g * _sigmoid(SWIGLU_ALPHA * g)).astype(BF16)

    @pl.when(tv_ref[i] == 0)
    def _():
        o_ref[...] = jnp.zeros(o_ref.shape, BF16)


def _moe_gate_up(te, tf, tv, xs, w_gu, b_gu, layer, tmm, tn):
    p_pad, d = xs.shape
    de = w_gu.shape[-1] // 2
    nj = de // tn
    grid_spec = pltpu.PrefetchScalarGridSpec(
        num_scalar_prefetch=3, grid=(nj, p_pad // tmm),
        in_specs=[pl.BlockSpec((tmm, d), lambda j, i, te, tf, tv: (i, 0)),
                  pl.BlockSpec((None, None, d, tn), lambda j, i, te, tf, tv: (layer, te[i], 0, j)),
                  pl.BlockSpec((None, None, d, tn), lambda j, i, te, tf, tv: (layer, te[i], 0, nj + j)),
                  pl.BlockSpec((None, None, 1, tn), lambda j, i, te, tf, tv: (layer, te[i], 0, j)),
                  pl.BlockSpec((None, None, 1, tn), lambda j, i, te, tf, tv: (layer, te[i], 0, nj + j))],
        out_specs=pl.BlockSpec((tmm, tn), lambda j, i, te, tf, tv: (i, j)),
        scratch_shapes=[pltpu.VMEM((d, tn), BF16), pltpu.VMEM((d, tn), BF16)])
    b4 = b_gu.reshape(b_gu.shape[0], b_gu.shape[1], 1, b_gu.shape[2])
    return pl.pallas_call(
        _moe_gate_up_kernel, grid_spec=grid_spec,
        out_shape=jax.ShapeDtypeStruct((p_pad, de), BF16),
        compiler_params=_params(2), name="moe_gate_up")(te, tf, tv, xs, w_gu, w_gu, b4, b4)


def _moe_down_kernel(te_ref, tf_ref, tv_ref, a_ref, w_ref, b_ref, o_ref, w16):
    i = pl.program_id(1)

    @pl.when(tf_ref[i] == 1)
    def _():
        w16[...] = w_ref[...].astype(BF16)

    @pl.when(tv_ref[i] == 1)
    def _():
        o_ref[...] = _dot(a_ref[...], w16[...]) + b_ref[...]

    @pl.when(tv_ref[i] == 0)
    def _():
        o_ref[...] = jnp.zeros(o_ref.shape, F32)


def _moe_down(te, tf, tv, a, w_d, b_d, layer, tmm, tn):
    p_pad, de = a.shape
    d = w_d.shape[-1]
    grid_spec = pltpu.PrefetchScalarGridSpec(
        num_scalar_prefetch=3, grid=(d // tn, p_pad // tmm),
        in_specs=[pl.BlockSpec((tmm, de), lambda j, i, te, tf, tv: (i, 0)),
                  pl.BlockSpec((None, None, de, tn), lambda j, i, te, tf, tv: (layer, te[i], 0, j)),
                  pl.BlockSpec((None, None, 1, tn), lambda j, i, te, tf, tv: (layer, te[i], 0, j))],
        out_specs=pl.BlockSpec((tmm, tn), lambda j, i, te, tf, tv: (i, j)),
        scratch_shapes=[pltpu.VMEM((de, tn), BF16)])
    b4 = b_d.reshape(b_d.shape[0], b_d.shape[1], 1, b_d.shape[2])
    return pl.pallas_call(
        _moe_down_kernel, grid_spec=grid_spec,
        out_shape=jax.ShapeDtypeStruct((p_pad, d), F32),
        compiler_params=_params(2), name="moe_down")(te, tf, tv, a, w_d, b4)


def _moe_combine_ln_kernel(*refs):
    y_refs = refs[:TOP_K]
    gate_ref, x_ref, g_ref, b_ref, o_ref, o16_ref = refs[TOP_K:]
    gate = gate_ref[...]
    out = gate[:, 0:1] * y_refs[0][...]
    for kk in range(1, TOP_K):
        out = out + gate[:, kk:kk + 1] * y_refs[kk][...]
    res = _layernorm_rows(DEEPNORM_ALPHA * x_ref[...] + out, g_ref[...], b_ref[...])
    o_ref[...] = res
    o16_ref[...] = res.astype(BF16)


def _moe_combine_ln(yk, gate, x, g, b, tm):
    t, d = x.shape
    nt = t // tm
    slot = lambda kk: pl.BlockSpec((tm, d), lambda i: (kk * nt + i, 0))
    return pl.pallas_call(
        _moe_combine_ln_kernel, grid=(nt,),
        in_specs=[slot(kk) for kk in range(TOP_K)]
        + [pl.BlockSpec((tm, LANES), lambda i: (i, 0)),
           pl.BlockSpec((tm, d), lambda i: (i, 0)),
           pl.BlockSpec((1, d), lambda i: (0, 0)), pl.BlockSpec((1, d), lambda i: (0, 0))],
        out_specs=[pl.BlockSpec((tm, d), lambda i: (i, 0))] * 2,
        out_shape=[jax.ShapeDtypeStruct((t, d), F32), jax.ShapeDtypeStruct((t, d), BF16)],
        compiler_params=_params(1), name="moe_combine_ln")(
            *([yk] * TOP_K), gate, x, g.reshape(1, d), b.reshape(1, d))


MOE_TILE_ROWS = 256
MOE_TILE_COLS = 1024
MOE_DOWN_TILE_COLS = 2048
ROW_TILE = 128


def _moe(x, x16, layer, router_w, router_b, w_gu, b_gu, w_d, b_d, ln_g, ln_b):
    t, d = x.shape
    w_pad = jnp.pad(router_w[layer], ((0, 0), (0, LANES - N_EXPERTS)))
    b_pad = jnp.pad(router_b[layer], (0, LANES - N_EXPERTS)).reshape(1, LANES)
    top_i, gate = _router(x, w_pad, b_pad, ROW_TILE)

    n_pairs = t * TOP_K
    n_tiles = n_pairs // MOE_TILE_ROWS + N_EXPERTS
    e_flat = top_i[:, :TOP_K].reshape(n_pairs)
    onehot = (e_flat[:, None] == jnp.arange(N_EXPERTS, dtype=jnp.int32)[None, :]).astype(jnp.int32)
    csum = jnp.cumsum(onehot, axis=0)
    counts = csum[-1]
    rank = jnp.sum(onehot * csum, axis=1) - 1
    tiles_per = (counts + MOE_TILE_ROWS - 1) // MOE_TILE_ROWS
    tile_end = jnp.cumsum(tiles_per)
    dest = ((tile_end - tiles_per) * MOE_TILE_ROWS)[e_flat] + rank
    row_token = jnp.zeros((n_tiles * MOE_TILE_ROWS,), jnp.int32).at[dest].set(
        jnp.arange(n_pairs, dtype=jnp.int32) // TOP_K)
    tile_id = jnp.arange(n_tiles, dtype=jnp.int32)
    tv = (tile_id < tile_end[-1]).astype(jnp.int32)
    last_tile = jnp.minimum(tile_id, tile_end[-1] - 1)
    te = jnp.sum((tile_end[None, :] <= last_tile[:, None]).astype(jnp.int32), axis=1)
    tf = jnp.concatenate([jnp.ones((1,), jnp.int32), (te[1:] != te[:-1]).astype(jnp.int32)])

    xs = x[row_token]
    act = _moe_gate_up(te, tf, tv, xs, w_gu, b_gu, layer, MOE_TILE_ROWS, MOE_TILE_COLS)
    ys = _moe_down(te, tf, tv, act, w_d, b_d, layer, MOE_TILE_ROWS, MOE_DOWN_TILE_COLS)
    yk = ys[dest.reshape(t, TOP_K).T.reshape(n_pairs)]
    return _moe_combine_ln(yk, gate, x, ln_g, ln_b, ROW_TILE)


PROJ_ROW_TILE = 640
PROJ_A_ROW_TILE = 512
PROJ_A_COL_TILE = 512
PROJ_C_COL_TILE = 896
OUT_ROW_TILE = 320
FOX_Q_TILE = 256
NSA_Q_TILE = 128


def _attention_layer(x, x16, i, n, s, nb, caches, page_table, w_in, b_f, w_out, rel_table, ln_g, ln_b):
    (cache_fox_k, cache_fox_v, cache_fox_logf, cache_cmp_k, cache_cmp_v, cache_sel_k, cache_sel_v,
     cache_win_k, cache_win_v) = caches
    tp = n * s
    past = page_table.shape[1] * PAGE_SIZE
    win = cache_win_k.shape[2]
    o_f = 3 * FOX_W
    o_n = o_f + FOX_HEADS
    o_g = o_n + NSA_W + 6 * NSA_KV_W
    w_main = jnp.concatenate([w_in[:, :o_f], w_in[:, o_n:o_g]], axis=1).astype(BF16)
    n_small = FOX_HEADS + 3 * NSA_HEADS
    w_small = jnp.pad(jnp.concatenate([w_in[:, o_f:o_n], w_in[:, o_g:]], axis=1), ((0, 0), (0, LANES - n_small)))
    b_small = jnp.pad(b_f, (0, LANES - FOX_HEADS))

    groups = [[FOX_W]] * 3 + [[NSA_W]] + [[NSA_KV_W, NSA_KV_W]] * 3
    qf_p, kf_p, vf_p, qn_p, *kv_p = _mm_split(x16, tp, w_main, groups, PROJ_A_ROW_TILE, PROJ_A_COL_TILE,
                                              "proj_a")
    hs = _mm(x16[tp:], w_main, nb, PROJ_A_COL_TILE, "proj_a_sample")
    logf_all, logf_t, sg_all = _gates_a(x, w_small, b_small, ROW_TILE)
    logf = logf_all[:, :FOX_HEADS]
    gates = sg_all[:, FOX_HEADS:n_small]

    c_row = _cumsum_seq(logf_t, n, s)
    o_fox_p = _fox_prompt(qf_p, kf_p, vf_p, c_row.reshape(n, FOX_HEADS, s, 1),
                          c_row.reshape(n, FOX_HEADS, 1, s), n, s, FOX_Q_TILE)
    tiles, cmpb, bk, bw, bc = _nsa_bias_tables(rel_table, s, NSA_Q_TILE, past, win)
    gates_g = gates.reshape(-1, NSA_GROUPS, 3 * NSA_HPG).transpose(1, 0, 2)
    col_nsa = 3 * FOX_W
    o_nsa_p = _nsa_prompt(qn_p, kv_p, gates_g, tiles, cmpb, n, s, NSA_Q_TILE)

    seg = lambda lo, w: hs[:, lo:lo + w]
    q_fox, k_fox, v_fox = seg(0, FOX_W), seg(FOX_W, FOX_W), seg(2 * FOX_W, FOX_W)
    q_nsa = seg(col_nsa, NSA_W).reshape(nb, NSA_HEADS, HEAD_DIM)
    kv = [seg(col_nsa + NSA_W + j * NSA_KV_W, NSA_KV_W).reshape(nb, NSA_GROUPS, HEAD_DIM) for j in range(6)]
    rows_view = lambda c: c.reshape(c.shape[0], c.shape[1], c.shape[2] * c.shape[3], HEAD_DIM)
    heads = lambda a: a.reshape(nb, FOX_HEADS, HEAD_DIM)
    o_fox_s = _fox_sample(page_table, i, heads(q_fox), heads(k_fox), heads(v_fox),
                          logf[tp:].reshape(nb, FOX_HEADS, 1),
                          rows_view(cache_fox_k), rows_view(cache_fox_v),
                          jnp.swapaxes(cache_fox_logf, 2, 3))
    o_c, sel = _nsa_sample_cmp(page_table, i, q_nsa, rows_view(cache_cmp_k), rows_view(cache_cmp_v), bc, past)
    o_nsa_s = _nsa_sample_sel(
        page_table, i, q_nsa, kv[2], kv[3], kv[4], kv[5], sel, o_c,
        gates[tp:].reshape(nb, NSA_HEADS, 3), rows_view(cache_sel_k), rows_view(cache_sel_v),
        rows_view(cache_win_k), rows_view(cache_win_v), bk, bw, rel_table[0].reshape(NSA_HEADS, 1))

    o_fox = jnp.concatenate([o_fox_p, o_fox_s.reshape(nb, FOX_W)], axis=0)
    o_nsa = jnp.concatenate([o_nsa_p, o_nsa_s.reshape(nb, NSA_W)], axis=0)
    w_out16 = w_out.astype(BF16)
    x_new, x_new16 = _mm_res_ln([o_fox, o_nsa], [w_out16[:FOX_W], w_out16[FOX_W:]], x, ln_g, ln_b,
                                OUT_ROW_TILE, "attn_out_ln")

    wb = min(NSA_WINDOW, s)

    def state(p, lo, heads, keep=s):
        p = p.reshape(n, s, heads, HEAD_DIM)[:, s - keep:]
        return p, hs[:, lo:lo + heads * HEAD_DIM].reshape(nb, 1, heads, HEAD_DIM)

    kv_lo = col_nsa + NSA_W
    states = [state(kf_p, FOX_W, FOX_HEADS), state(vf_p, 2 * FOX_W, FOX_HEADS),
              (logf[:tp].reshape(n, s, FOX_HEADS), logf[tp:].reshape(nb, 1, FOX_HEADS))]
    states += [state(kv_p[j], kv_lo + j * NSA_KV_W, NSA_GROUPS) for j in range(4)]
    states += [state(kv_p[j], kv_lo + j * NSA_KV_W, NSA_GROUPS, wb) for j in (4, 5)]
    return x_new, x_new16, states


def _gla_layer(x, x16, i, n, s, nb, state_gla, w_in, w_g2, b_g, g_norm, w_out, ln_g, ln_b):
    tp = n * s
    qk_w = GLA_HEADS * GLA_DK
    v_w = GLA_HEADS * GLA_DV
    o_g = 2 * qk_w + v_w
    o_r = o_g + GLA_GATE_RANK
    w_main = jnp.concatenate([w_in[:, :o_g], w_in[:, o_r:], w_in[:, o_g:o_r],
                              jnp.zeros((w_in.shape[0], LANES - GLA_GATE_RANK), w_in.dtype)], axis=1).astype(BF16)
    hc = _mm(x16, w_main, PROJ_ROW_TILE, PROJ_C_COL_TILE, "proj_c")
    w_g2_pad = jnp.pad(w_g2, ((0, LANES - GLA_GATE_RANK), (0, 0))).astype(BF16)
    la = _gla_gate(hc, (o_g + v_w) // LANES, w_g2_pad, b_g, PROJ_ROW_TILE)

    og_p, st_p = _gla_prompt(hc, la, g_norm, n, s)
    hs = hc[tp:]
    col = lambda a: a.reshape(nb, GLA_HEADS, GLA_DK, 1)
    rowv = lambda a: a.reshape(nb, GLA_HEADS, 1, GLA_DV)
    og_s, st_s = _gla_sample(col(hs[:, :qk_w]), col(hs[:, qk_w:2 * qk_w]), col(la[tp:]),
                             rowv(hs[:, 2 * qk_w:o_g]), rowv(hs[:, o_g:o_g + v_w]), g_norm, state_gla[i])
    og = jnp.concatenate([og_p, og_s.reshape(nb, v_w)], axis=0)
    x_new, x_new16 = _mm_res_ln([og], [w_out.astype(BF16)], x, ln_g, ln_b, OUT_ROW_TILE, "gla_out_ln")
    return x_new, x_new16, st_p, st_s


def kernel(x_prompt, x_sample, cache_fox_k, cache_fox_v, cache_fox_logf, cache_nsa_cmp_k, cache_nsa_cmp_v,
           cache_nsa_sel_k, cache_nsa_sel_v, cache_nsa_win_k, cache_nsa_win_v, state_gla, page_table,
           w_in_a, b_forget, w_out_a, rel_bias_table, w_in_c, w_gla_gate2, b_gla_gate, gla_norm_g, w_out_c,
           ln_g, ln_b, router_w, router_b, w_gate_up, b_gate_up, w_down, b_down):
    n, s, d = x_prompt.shape
    nb = x_sample.shape[0]
    assert x_sample.shape[1] == 1
    tp = n * s
    x = jnp.concatenate([x_prompt.reshape(tp, d), x_sample.reshape(nb, d)], axis=0)
    x16 = x.astype(BF16)
    caches = (cache_fox_k, cache_fox_v, cache_fox_logf, cache_nsa_cmp_k, cache_nsa_cmp_v,
              cache_nsa_sel_k, cache_nsa_sel_v, cache_nsa_win_k, cache_nsa_win_v)
    attn_states, gla_p, gla_s = [], [], []
    for layer in range(DEPTH):
        i = layer // 2
        if layer % 2 == 0:
            x, x16, st = _attention_layer(x, x16, i, n, s, nb, caches, page_table, w_in_a[i], b_forget[i],
                                          w_out_a[i], rel_bias_table, ln_g[layer, 0], ln_b[layer, 0])
            attn_states.append(st)
        else:
            x, x16, st_p, st_s = _gla_layer(x, x16, i, n, s, nb, state_gla, w_in_c[i], w_gla_gate2[i],
                                            b_gla_gate[i], gla_norm_g[i], w_out_c[i],
                                            ln_g[layer, 0], ln_b[layer, 0])
            gla_p.append(st_p)
            gla_s.append(st_s)
        x, x16 = _moe(x, x16, layer, router_w, router_b, w_gate_up, b_gate_up, w_down, b_down,
                      ln_g[layer, 1], ln_b[layer, 1])
    outs = [x[:tp].reshape(n, s, d), x[tp:].reshape(nb, 1, d)]
    for j in range(9):
        outs.append(jnp.stack([st[j][0] for st in attn_states]))
        outs.append(jnp.stack([st[j][1] for st in attn_states]))
    outs.append(jnp.stack(gla_p))
    outs.append(jnp.stack(gla_s))
    return tuple(outs)
```

```python
import functools
import math

import jax
import jax.numpy as jnp
import numpy as np
from jax import lax
from jax.experimental import pallas as pl
from jax.experimental.pallas import tpu as pltpu

F32 = jnp.float32
BF16 = jnp.bfloat16
HIGHEST = lax.Precision.HIGHEST
NEG_INF = float("-inf")
POS_INF = float("inf")

D_MODEL = 2048
PAGE_SIZE = 128
HEAD_DIM = 128
FOX_HEADS = 8
NSA_HEADS = 8
NSA_GROUPS = 2
NSA_HPG = NSA_HEADS // NSA_GROUPS
NSA_BLOCK = 64
NSA_TOPK = 8
NSA_WINDOW = 512
REL_BUCKETS = 32
REL_MAX_DIST = 1024
GLA_HEADS = 4
GLA_DK = D_MODEL // 2 // GLA_HEADS
GLA_DV = D_MODEL // GLA_HEADS
GLA_GATE_RANK = 16
GLA_TAU = 16.0
GLA_CHUNK = 64
GLA_SUB = 16
N_EXPERTS = 32
TOP_K = 4
SWIGLU_LIMIT = 7.0
SWIGLU_ALPHA = 1.702
LN_EPS = 1e-5
DEPTH = 2
DEEPNORM_ALPHA = (2 * DEPTH) ** 0.25
FOX_W = FOX_HEADS * HEAD_DIM
NSA_W = NSA_HEADS * HEAD_DIM
NSA_KV_W = NSA_GROUPS * HEAD_DIM
ATTN_SCALE = HEAD_DIM ** -0.5

VMEM_LIMIT_BYTES = 56 * 1024 * 1024
LANES = 128

_REL_EXACT = REL_BUCKETS // 2
_REL_THRESHOLDS = tuple(
    (_REL_EXACT * 8) if 2 * j == (REL_BUCKETS - _REL_EXACT) else
    math.ceil(_REL_EXACT * (REL_MAX_DIST / _REL_EXACT) ** (j / (REL_BUCKETS - _REL_EXACT)))
    for j in range(1, REL_BUCKETS - _REL_EXACT))


def _params(n_axes):
    return pltpu.CompilerParams(dimension_semantics=("arbitrary",) * n_axes,
                                vmem_limit_bytes=VMEM_LIMIT_BYTES)


def _iota(shape, dim, dtype=jnp.int32):
    return lax.broadcasted_iota(dtype, shape, dim)


def _shift_div(x, d):
    k = d.bit_length() - 1
    assert d == 1 << k
    return lax.shift_right_logical(x, jnp.full(x.shape, k, jnp.int32))


def _sigmoid(x):
    return 1.0 / (1.0 + jnp.exp(-x))


def _log_sigmoid(x):
    return jnp.minimum(x, 0.0) - jnp.log(1.0 + jnp.exp(-jnp.abs(x)))


def _dot_nt(a, b, precision=None):
    return lax.dot_general(a, b, (((1,), (1,)), ((), ())), precision=precision,
                           preferred_element_type=F32)


def _dot(a, b, precision=None):
    return jnp.dot(a, b, precision=precision, preferred_element_type=F32)


def _masked_softmax_parts(s, mask):
    s = jnp.where(mask, s, NEG_INF)
    m = jnp.max(s, axis=-1, keepdims=True)
    m = jnp.where(m == NEG_INF, 0.0, m)
    e = jnp.exp(s - m)
    return e, jnp.maximum(jnp.sum(e, axis=-1, keepdims=True), 1e-30)


def _online_update_t(carry, s, v_bf16):
    m, l, acc = carry
    m_new = jnp.maximum(m, jnp.max(s, axis=0, keepdims=True))
    m_safe = jnp.where(m_new == NEG_INF, 0.0, m_new)
    p = jnp.exp(s - m_safe)
    alpha = jnp.exp(m - m_safe)
    l = alpha * l + jnp.sum(p, axis=0, keepdims=True)
    pv = lax.dot_general(v_bf16, p.astype(BF16), (((0,), (0,)), ((), ())), preferred_element_type=F32)
    return m_new, l, alpha * acc + pv


def _rel_bucket(dist):
    n = jnp.maximum(dist, 0)
    large = jnp.full(n.shape, _REL_EXACT, jnp.int32)
    for t in _REL_THRESHOLDS:
        large = large + jnp.where(n >= t, 1, 0)
    return jnp.where(n < _REL_EXACT, n, large)


def _rel_bias(bucket, table_ref, head):
    out = jnp.zeros(bucket.shape, F32)
    for b in range(REL_BUCKETS):
        out = jnp.where(bucket == b, table_ref[b, head], out)
    return out


def _top_blocks(score, n_sel):
    lane = _iota(score.shape, 1).astype(F32)
    sel = jnp.zeros(score.shape, F32)
    for _ in range(n_sel):
        mx = jnp.max(score, axis=-1, keepdims=True)
        first = jnp.min(jnp.where(score == mx, lane, 1e9), axis=-1, keepdims=True)
        hit = lane == first
        sel = jnp.where(hit & (mx > NEG_INF), 1.0, sel)
        score = jnp.where(hit, NEG_INF, score)
    return sel


def _mm_kernel(x_ref, w_ref, o_ref):
    o_ref[...] = _dot(x_ref[...], w_ref[...])


def _mm(x, w, tm, tn, name):
    m, k = x.shape
    n = w.shape[1]
    assert m % tm == 0 and n % tn == 0
    return pl.pallas_call(
        _mm_kernel, grid=(n // tn, m // tm),
        in_specs=[pl.BlockSpec((tm, k), lambda j, i: (i, 0)),
                  pl.BlockSpec((k, tn), lambda j, i: (0, j))],
        out_specs=pl.BlockSpec((tm, tn), lambda j, i: (i, j)),
        out_shape=jax.ShapeDtypeStruct((m, n), F32),
        compiler_params=_params(2), name=name)(x, w)


def _mm_split_kernel(x_ref, w_ref, *o_refs, plan):
    j = pl.program_id(1)
    res = _dot(x_ref[...], w_ref[...])
    for j_lo, j_hi, outs in plan:
        @pl.when((j >= j_lo) & (j < j_hi))
        def _(outs=outs):
            for oi, lo, hi in outs:
                o_refs[oi][...] = res[:, lo:hi]


def _mm_split(x, m, w, groups, tm, tn, name):
    k = x.shape[1]
    assert m % tm == 0 and sum(sum(g) for g in groups) == w.shape[1]
    plan, out_specs, out_shapes = [], [], []
    j0 = 0
    for g in groups:
        nblk = sum(g) // tn
        assert sum(g) == nblk * tn and (len(g) == 1 or nblk == 1)
        outs, lane = [], 0
        for width in g:
            bw = min(width, tn)
            outs.append((len(out_specs), lane, lane + bw))
            lane += bw
            out_specs.append(pl.BlockSpec(
                (tm, bw), lambda i, j, j0=j0, nblk=nblk: (i, jnp.clip(j - j0, 0, nblk - 1))))
            out_shapes.append(jax.ShapeDtypeStruct((m, width), F32))
        plan.append((j0, j0 + nblk, outs))
        j0 += nblk
    return pl.pallas_call(
        functools.partial(_mm_split_kernel, plan=plan), grid=(m // tm, j0),
        in_specs=[pl.BlockSpec((tm, k), lambda i, j: (i, 0)),
                  pl.BlockSpec((k, tn), lambda i, j: (0, j))],
        out_specs=out_specs, out_shape=out_shapes,
        compiler_params=_params(2), name=name)(x, w)


def _layernorm_rows(z, g, b):
    mu = jnp.mean(z, axis=-1, keepdims=True)
    zc = z - mu
    var = jnp.mean(zc * zc, axis=-1, keepdims=True)
    return zc * lax.rsqrt(var + LN_EPS) * g + b


def _mm_res_ln_kernel(*refs, n_in):
    a_refs, w_refs = refs[:n_in], refs[n_in:2 * n_in]
    x_ref, g_ref, b_ref, o_ref, o16_ref = refs[2 * n_in:]
    y = _dot(a_refs[0][...].astype(BF16), w_refs[0][...])
    for a_ref, w_ref in zip(a_refs[1:], w_refs[1:]):
        y = y + _dot(a_ref[...].astype(BF16), w_ref[...])
    out = _layernorm_rows(DEEPNORM_ALPHA * x_ref[...] + y, g_ref[...], b_ref[...])
    o_ref[...] = out
    o16_ref[...] = out.astype(BF16)


def _mm_res_ln(acts, weights, x, g, b, tm, name):
    t, d = x.shape
    n_in = len(acts)
    in_specs = [pl.BlockSpec((tm, a.shape[1]), lambda i: (i, 0)) for a in acts]
    in_specs += [pl.BlockSpec(w.shape, lambda i: (0, 0)) for w in weights]
    in_specs += [pl.BlockSpec((tm, d), lambda i: (i, 0)),
                 pl.BlockSpec((1, d), lambda i: (0, 0)), pl.BlockSpec((1, d), lambda i: (0, 0))]
    return pl.pallas_call(
        functools.partial(_mm_res_ln_kernel, n_in=n_in), grid=(t // tm,),
        in_specs=in_specs,
        out_specs=[pl.BlockSpec((tm, d), lambda i: (i, 0))] * 2,
        out_shape=[jax.ShapeDtypeStruct((t, d), F32), jax.ShapeDtypeStruct((t, d), BF16)],
        compiler_params=_params(1), name=name)(*acts, *weights, x, g.reshape(1, d), b.reshape(1, d))


def _gates_a_kernel(x_ref, w_ref, wt_ref, b_ref, bt_ref, logf_ref, logft_ref, sg_ref):
    x = x_ref[...]
    z = _dot(x, w_ref[...], HIGHEST)
    logf_ref[...] = _log_sigmoid(z + b_ref[...])
    sg_ref[...] = _sigmoid(z)
    zt = _dot_nt(wt_ref[...], x, HIGHEST)
    logft_ref[...] = _log_sigmoid(zt + bt_ref[...])[:FOX_HEADS]


def _gates_a(x, w_small, b_small, tm):
    t, d = x.shape
    return pl.pallas_call(
        _gates_a_kernel, grid=(t // tm,),
        in_specs=[pl.BlockSpec((tm, d), lambda i: (i, 0)),
                  pl.BlockSpec((d, LANES), lambda i: (0, 0)),
                  pl.BlockSpec((LANES, d), lambda i: (0, 0)),
                  pl.BlockSpec((1, LANES), lambda i: (0, 0)),
                  pl.BlockSpec((LANES, 1), lambda i: (0, 0))],
        out_specs=[pl.BlockSpec((tm, LANES), lambda i: (i, 0)),
                   pl.BlockSpec((FOX_HEADS, tm), lambda i: (0, i)),
                   pl.BlockSpec((tm, LANES), lambda i: (i, 0))],
        out_shape=[jax.ShapeDtypeStruct((t, LANES), F32), jax.ShapeDtypeStruct((FOX_HEADS, t), F32),
                   jax.ShapeDtypeStruct((t, LANES), F32)],
        compiler_params=_params(1), name="gates_a")(
            x, w_small, w_small.T, b_small.reshape(1, LANES), b_small.reshape(LANES, 1))


def _cumsum_kernel(x_ref, o_ref, *, blk):
    s = x_ref.shape[1]
    upper = jnp.where(_iota((blk, blk), 0) <= _iota((blk, blk), 1), 1.0, 0.0)
    carry = jnp.zeros((x_ref.shape[0], 1), F32)
    for j in range(s // blk):
        c = _dot(x_ref[:, j * blk:(j + 1) * blk], upper, HIGHEST) + carry
        o_ref[:, j * blk:(j + 1) * blk] = c
        carry = c[:, blk - 1:blk]


def _cumsum_seq(xt, n, s):
    h = xt.shape[0]
    return pl.pallas_call(
        functools.partial(_cumsum_kernel, blk=min(256, s)), grid=(n,),
        in_specs=[pl.BlockSpec((h, s), lambda i: (0, i))],
        out_specs=pl.BlockSpec((None, h, s), lambda i: (i, 0, 0)),
        out_shape=jax.ShapeDtypeStruct((n, h, s), F32),
        compiler_params=_params(1), name="fox_cumsum")(xt)


def _fox_prompt_kernel(q_ref, k_ref, v_ref, cq_ref, ck_ref, o_ref, *, tq):
    qi = pl.program_id(2)
    q = q_ref[...].astype(BF16)
    cq = cq_ref[...]

    key_minus_query = _iota((tq, tq), 0) - _iota((tq, tq), 1)

    def tile(kb, carry):
        limit = jnp.where(kb < qi, tq, jnp.where(kb == qi, 0, -tq - 1))
        start = pl.multiple_of(jnp.minimum(kb, qi) * tq, tq)
        k = k_ref[pl.ds(start, tq), :].astype(BF16)
        v = v_ref[pl.ds(start, tq), :].astype(BF16)
        s = _dot_nt(k, q) * ATTN_SCALE + cq - ck_ref[pl.ds(start, tq), :]
        return _online_update_t(carry, jnp.where(key_minus_query <= limit, s, NEG_INF), v)

    def body(j, carry):
        return tile(2 * j + 1, tile(2 * j, carry))

    init = (jnp.full((1, tq), NEG_INF, F32), jnp.zeros((1, tq), F32), jnp.zeros((HEAD_DIM, tq), F32))
    _, l, acc = lax.fori_loop(0, (qi + 2) // 2, body, init)
    o_ref[...] = (acc / jnp.maximum(l, 1e-30)).T


def _fox_prompt(q, k, v, c_col, c_row, n, s, tq):
    nq = s // tq
    kv_spec = pl.BlockSpec((s, HEAD_DIM), lambda b, hd, qi: (b, hd))
    return pl.pallas_call(
        functools.partial(_fox_prompt_kernel, tq=tq), grid=(n, FOX_HEADS, nq),
        in_specs=[pl.BlockSpec((tq, HEAD_DIM), lambda b, hd, qi: (b * nq + qi, hd)),
                  kv_spec, kv_spec,
                  pl.BlockSpec((None, None, 1, tq), lambda b, hd, qi: (b, hd, 0, qi)),
                  pl.BlockSpec((None, None, s, 1), lambda b, hd, qi: (b, hd, 0, 0))],
        out_specs=pl.BlockSpec((tq, HEAD_DIM), lambda b, hd, qi: (b * nq + qi, hd)),
        out_shape=jax.ShapeDtypeStruct((n * s, FOX_W), F32),
        compiler_params=_params(3), name="fox_prompt")(q, k, v, c_row, c_col)


def _nsa_bias_kernel(table_ref, tiles_ref, wtiles_ref, cmpb_ref, bk_ref, bw_ref, bc_ref, *, tq, past, n_win):
    d = pl.program_id(0)
    offs = _iota((tq, tq), 1) - _iota((tq, tq), 0)
    dist = d * tq + offs
    bucket = _rel_bucket(dist)
    dist_w = jnp.minimum(d, n_win - 1) * tq + offs
    bucket_w = _rel_bucket(dist_w)
    in_window = (dist_w >= 0) & (dist_w <= NSA_WINDOW)
    nb = cmpb_ref.shape[-1]
    dist_c = jnp.minimum(d, pl.num_programs(0) - 2) * tq + _iota((tq, nb), 0) \
        - (_iota((tq, nb), 1) * NSA_BLOCK + NSA_BLOCK - 1)
    bucket_c = _rel_bucket(dist_c)
    for h in range(NSA_HEADS):
        tiles_ref[h] = jnp.where(dist >= 0, _rel_bias(bucket, table_ref, h), NEG_INF)
        wtiles_ref[h] = jnp.where(in_window, _rel_bias(bucket_w, table_ref, h), NEG_INF)
        cmpb_ref[h] = _rel_bias(bucket_c, table_ref, h)

    @pl.when(d == 0)
    def _():
        win = bw_ref.shape[1] // NSA_GROUPS
        bucket_k = _rel_bucket(past - _shift_div(_iota((1, bk_ref.shape[1]), 1), NSA_GROUPS))
        bucket_w = _rel_bucket(win - _shift_div(_iota((1, bw_ref.shape[1]), 1), NSA_GROUPS))
        bucket_b = _rel_bucket(past - (_iota((1, LANES), 1) * NSA_BLOCK + NSA_BLOCK - 1))
        for h in range(NSA_HEADS):
            bk_ref[h:h + 1, :] = _rel_bias(bucket_k, table_ref, h)
            bw_ref[h:h + 1, :] = _rel_bias(bucket_w, table_ref, h)
            bc_ref[h:h + 1, :] = _rel_bias(bucket_b, table_ref, h)


def _nsa_bias_tables(rel_table, s, tq, past, win):
    nq = s // tq
    nb = LANES
    n_win = min(_window_tiles(tq), nq) + 1
    assert s // NSA_BLOCK <= LANES
    return pl.pallas_call(
        functools.partial(_nsa_bias_kernel, tq=tq, past=past, n_win=n_win),
        grid=(nq + 1,),
        in_specs=[pl.BlockSpec(memory_space=pltpu.SMEM)],
        out_specs=[pl.BlockSpec((None, NSA_HEADS, tq, tq), lambda d: (d, 0, 0, 0)),
                   pl.BlockSpec((None, NSA_HEADS, tq, tq), lambda d: (jnp.minimum(d, n_win - 1), 0, 0, 0)),
                   pl.BlockSpec((None, NSA_HEADS, tq, nb), lambda d: (jnp.minimum(d, nq - 1), 0, 0, 0)),
                   pl.BlockSpec((NSA_HEADS, NSA_GROUPS * past), lambda d: (0, 0)),
                   pl.BlockSpec((NSA_HEADS, NSA_GROUPS * win), lambda d: (0, 0)),
                   pl.BlockSpec((NSA_HEADS, LANES), lambda d: (0, 0))],
        out_shape=[jax.ShapeDtypeStruct((nq + 1, NSA_HEADS, tq, tq), F32),
                   jax.ShapeDtypeStruct((n_win, NSA_HEADS, tq, tq), F32),
                   jax.ShapeDtypeStruct((nq, NSA_HEADS, tq, nb), F32),
                   jax.ShapeDtypeStruct((NSA_HEADS, NSA_GROUPS * past), F32),
                   jax.ShapeDtypeStruct((NSA_HEADS, NSA_GROUPS * win), F32),
                   jax.ShapeDtypeStruct((NSA_HEADS, LANES), F32)],
        compiler_params=_params(1), name="nsa_bias")(rel_table)


def _window_tiles(tq):
    return (NSA_WINDOW + tq - 1) // tq


def _nsa_prompt_kernel(q_ref, kc_ref, vc_ref, ks_ref, vs_ref, kw_ref, vw_ref, gate_ref, tiles_ref,
                       wtiles_ref, cmpb_ref, o_ref, *, tq):
    qi = pl.program_id(2)
    s_len = kc_ref.shape[0]
    n_blk = s_len // NSA_BLOCK
    nb = LANES
    rows = NSA_HPG * tq
    q = jnp.concatenate([q_ref[:, h * HEAD_DIM:(h + 1) * HEAD_DIM] for h in range(NSA_HPG)],
                        axis=0).astype(BF16)

    def block_means(ref):
        m = jnp.mean(ref[...].reshape(n_blk, NSA_BLOCK, HEAD_DIM), axis=1)
        return jnp.concatenate([m, jnp.zeros((nb - n_blk, HEAD_DIM), F32)], axis=0).astype(BF16)

    qpos_r = qi * tq + (_iota((rows, nb), 0) & (tq - 1))
    blk_r = _iota((rows, nb), 1)
    bias_c = jnp.concatenate([cmpb_ref[h] for h in range(NSA_HPG)], axis=0)
    s_c = _dot_nt(q, block_means(kc_ref)) * ATTN_SCALE + bias_c
    e_c, den_c = _masked_softmax_parts(
        s_c, (blk_r < n_blk) & (qpos_r >= blk_r * NSA_BLOCK + NSA_BLOCK - 1))
    p_c = e_c / den_c
    o_c = _dot(p_c.astype(BF16), block_means(vc_ref))

    p_sum = p_c[0:tq]
    for h in range(1, NSA_HPG):
        p_sum = p_sum + p_c[h * tq:(h + 1) * tq]
    blk = _iota((tq, nb), 1)
    cur = _shift_div(qi * tq + _iota((tq, nb), 0), NSA_BLOCK)
    forced = (blk == 0) | (blk == cur) | (blk == cur - 1)
    score = jnp.where(forced, POS_INF, jnp.where(blk <= cur, p_sum, NEG_INF))
    sel = _top_blocks(score, min(NSA_TOPK, n_blk)).astype(BF16)

    init = (jnp.full((1, rows), NEG_INF, F32), jnp.zeros((1, rows), F32), jnp.zeros((HEAD_DIM, rows), F32))
    lane_tile = lambda ref, idx: jnp.concatenate([ref[idx, h] for h in range(NSA_HPG)], axis=1)

    def sel_tile(kt, carry):
        threshold = jnp.where(kt <= qi, 0.5, 2.0)
        start = pl.multiple_of(jnp.minimum(kt, qi) * tq, tq)
        k = ks_ref[pl.ds(start, tq), :].astype(BF16)
        v = vs_ref[pl.ds(start, tq), :].astype(BF16)
        s = _dot_nt(k, q) * ATTN_SCALE + lane_tile(tiles_ref, jnp.maximum(qi - kt, 0))
        key_blk = _shift_div(start + _iota((tq, nb), 0), NSA_BLOCK)
        expand = jnp.where(_iota((tq, nb), 1) == key_blk, 1.0, 0.0).astype(BF16)
        picked = _dot_nt(expand, sel)
        picked = jnp.concatenate([picked] * NSA_HPG, axis=1)
        return _online_update_t(carry, jnp.where(picked > threshold, s, NEG_INF), v)

    def sel_body(j, carry):
        return sel_tile(2 * j + 1, sel_tile(2 * j, carry))

    _, l_s, acc_s = lax.fori_loop(0, (qi + 2) // 2, sel_body, init)
    o_s = acc_s / jnp.maximum(l_s, 1e-30)

    carry = init
    for delta in range(min(_window_tiles(tq), s_len // tq - 1), -1, -1):
        kt = qi - delta
        start = pl.multiple_of(jnp.maximum(kt, 0) * tq, tq)
        k = kw_ref[pl.ds(start, tq), :].astype(BF16)
        v = vw_ref[pl.ds(start, tq), :].astype(BF16)
        s = _dot_nt(k, q) * ATTN_SCALE + lane_tile(wtiles_ref, delta)
        carry = _online_update_t(carry, s if delta == 0 else jnp.where(kt >= 0, s, NEG_INF), v)
    _, l_w, acc_w = carry
    o_w = acc_w / jnp.maximum(l_w, 1e-30)

    gate = gate_ref[...]
    for h in range(NSA_HPG):
        r = slice(h * tq, (h + 1) * tq)
        o_ref[:, h * HEAD_DIM:(h + 1) * HEAD_DIM] = (
            gate[:, 3 * h:3 * h + 1] * o_c[r] + gate[:, 3 * h + 1:3 * h + 2] * o_s[:, r].T
            + gate[:, 3 * h + 2:3 * h + 3] * o_w[:, r].T)


def _nsa_prompt(q, kv, gates, tiles, wtiles, cmpb, n, s, tq):
    nq = s // tq
    grp_w = NSA_HPG * HEAD_DIM
    kv_spec = pl.BlockSpec((s, HEAD_DIM), lambda b, g, qi: (b, g))
    tile_spec = lambda a: pl.BlockSpec((a.shape[0], NSA_HPG, tq, tq), lambda b, g, qi: (0, g, 0, 0))
    return pl.pallas_call(
        functools.partial(_nsa_prompt_kernel, tq=tq), grid=(n, NSA_GROUPS, nq),
        in_specs=[pl.BlockSpec((tq, grp_w), lambda b, g, qi: (b * nq + qi, g))]
        + [kv_spec] * 6
        + [pl.BlockSpec((None, tq, 3 * NSA_HPG), lambda b, g, qi: (g, b * nq + qi, 0)),
           tile_spec(tiles), tile_spec(wtiles),
           pl.BlockSpec((None, NSA_HPG, tq, LANES), lambda b, g, qi: (qi, g, 0, 0))],
        out_specs=pl.BlockSpec((tq, grp_w), lambda b, g, qi: (b * nq + qi, g)),
        out_shape=jax.ShapeDtypeStruct((n * s, NSA_W), F32),
        compiler_params=_params(3), name="nsa_prompt")(q, *kv, gates, tiles, wtiles, cmpb)


def _softmax_step(scr, s_parts, v_refs):
    m_scr, l_scr, acc_scr = scr
    s = jnp.concatenate(s_parts, axis=1)
    m = m_scr[...]
    m_new = jnp.maximum(m, jnp.max(s, axis=-1, keepdims=True))
    m_safe = jnp.where(m_new == NEG_INF, 0.0, m_new)
    p = jnp.exp(s - m_safe)
    alpha = jnp.exp(m - m_safe)
    l_scr[...] = alpha * l_scr[...] + jnp.sum(p, axis=-1, keepdims=True)
    acc = alpha * acc_scr[...]
    width = s_parts[0].shape[1]
    for i, v_ref in enumerate(v_refs):
        acc = acc + _dot(p[:, i * width:(i + 1) * width].astype(BF16), v_ref[...].astype(BF16))
    acc_scr[...] = acc
    m_scr[...] = m_new


def _fox_sample_kernel(pt_ref, q_ref, kn_ref, vn_ref, lfn_ref, later_ref, *rest, pp):
    k_refs, v_refs, lf_refs = rest[:pp], rest[pp:2 * pp], rest[2 * pp:3 * pp]
    o_ref, m_scr, l_scr, acc_scr, suf_scr = rest[3 * pp:]
    step = pl.program_id(1)
    q = q_ref[...]

    @pl.when(step == 0)
    def _():
        m_scr[...] = jnp.sum(q * kn_ref[...], axis=-1, keepdims=True) * ATTN_SCALE
        l_scr[...] = jnp.ones(l_scr.shape, F32)
        acc_scr[...] = vn_ref[...].astype(BF16).astype(F32)
        suf_scr[...] = lfn_ref[...]

    cols = k_refs[0].shape[0]
    own = (_iota((FOX_HEADS, cols), 1) & (FOX_HEADS - 1)) == _iota((FOX_HEADS, cols), 0)
    q16 = q.astype(BF16)
    later = later_ref[...]
    suf = suf_scr[...]
    s_parts = []
    for k_ref, lf_ref in zip(k_refs, lf_refs):
        lf = lf_ref[...]
        lf_hi = lf.astype(BF16)
        lf_lo = (lf - lf_hi.astype(F32)).astype(BF16)
        decay = _dot(lf_hi, later) + _dot(lf_lo, later) + suf
        s = _dot_nt(q16, k_ref[...].astype(BF16)) * ATTN_SCALE + decay
        s_parts.append(jnp.where(own, s, NEG_INF))
        suf = suf + jnp.sum(lf, axis=-1, keepdims=True)
    suf_scr[...] = suf
    _softmax_step((m_scr, l_scr, acc_scr), s_parts, v_refs)

    @pl.when(step == pl.num_programs(1) - 1)
    def _():
        o_ref[...] = acc_scr[...] / jnp.maximum(l_scr[...], 1e-30)


FOX_PAGES_PER_STEP = 4
NSA_PAGES_PER_STEP = 8


def _fox_sample(page_table, layer, q, k_new, v_new, logf_new, cache_k, cache_v, cache_logf_t):
    nb, n_pages = page_table.shape
    pp = FOX_PAGES_PER_STEP
    assert n_pages % pp == 0
    rows = PAGE_SIZE * FOX_HEADS
    later = np.repeat(np.tril(np.ones((PAGE_SIZE, PAGE_SIZE), np.float32), -1), FOX_HEADS, axis=1)
    per_b = lambda shape: pl.BlockSpec((None,) + shape, lambda b, p, pt: (b, 0, 0))
    page = lambda i: (lambda b, p, pt: (layer, pt[b, n_pages - 1 - (p * pp + i)], 0, 0))
    grid_spec = pltpu.PrefetchScalarGridSpec(
        num_scalar_prefetch=1, grid=(nb, n_pages // pp),
        in_specs=[per_b((FOX_HEADS, HEAD_DIM)), per_b((FOX_HEADS, HEAD_DIM)), per_b((FOX_HEADS, HEAD_DIM)),
                  per_b((FOX_HEADS, 1)),
                  pl.BlockSpec((PAGE_SIZE, rows), lambda b, p, pt: (0, 0))]
        + [pl.BlockSpec((None, None, rows, HEAD_DIM), page(i)) for i in range(pp)]
        + [pl.BlockSpec((None, None, rows, HEAD_DIM), page(i)) for i in range(pp)]
        + [pl.BlockSpec((None, None, FOX_HEADS, PAGE_SIZE), page(i)) for i in range(pp)],
        out_specs=per_b((FOX_HEADS, HEAD_DIM)),
        scratch_shapes=[pltpu.VMEM((FOX_HEADS, 1), F32), pltpu.VMEM((FOX_HEADS, 1), F32),
                        pltpu.VMEM((FOX_HEADS, HEAD_DIM), F32), pltpu.VMEM((FOX_HEADS, 1), F32)])
    return pl.pallas_call(
        functools.partial(_fox_sample_kernel, pp=pp), grid_spec=grid_spec,
        out_shape=jax.ShapeDtypeStruct((nb, FOX_HEADS, HEAD_DIM), F32),
        compiler_params=_params(2), name="fox_sample")(
            page_table, q, k_new, v_new, logf_new, jnp.asarray(later, BF16),
            *([cache_k] * pp), *([cache_v] * pp), *([cache_logf_t] * pp))


def _group_block_sums(x):
    sub = 8
    y = jnp.sum(x.reshape(x.shape[0] // sub, sub, x.shape[1]), axis=0)
    shift = NSA_GROUPS
    while shift < sub:
        y = y + pltpu.roll(y, shift, 0)
        shift *= 2
    return y


def _nsa_sample_cmp_kernel(pt_ref, q_ref, bc_ref, *rest, past, pp):
    k_refs, v_refs = rest[:pp], rest[pp:2 * pp]
    oc_ref, sel_ref, kcb_scr, vcb_scr = rest[2 * pp:]
    step = pl.program_id(1)
    n_blk = past // NSA_BLOCK
    per_page = PAGE_SIZE // NSA_BLOCK
    blk_rows = NSA_BLOCK * NSA_GROUPS

    @pl.when(step == 0)
    def _():
        kcb_scr[...] = jnp.zeros(kcb_scr.shape, F32)
        vcb_scr[...] = jnp.zeros(vcb_scr.shape, F32)

    row_id = _iota((LANES, HEAD_DIM), 0)
    for refs, scr in ((k_refs, kcb_scr), (v_refs, vcb_scr)):
        tabs = [scr[g] for g in range(NSA_GROUPS)]
        for i, ref in enumerate(refs):
            for bl in range(per_page):
                sums = _group_block_sums(ref[bl * blk_rows:(bl + 1) * blk_rows, :]) * (1.0 / NSA_BLOCK)
                blk_id = (step * pp + i) * per_page + bl
                for g in range(NSA_GROUPS):
                    tabs[g] = jnp.where(row_id == blk_id, sums[g:g + 1], tabs[g])
        for g in range(NSA_GROUPS):
            scr[g] = tabs[g]

    @pl.when(step == pl.num_programs(1) - 1)
    def _():
        q16 = q_ref[...].astype(BF16)
        blk = _iota((NSA_HEADS, LANES), 1)
        head_grp = _shift_div(_iota((NSA_HEADS, LANES), 0), NSA_HPG)
        by_group = lambda parts: functools.reduce(
            lambda acc, gp: jnp.where(head_grp == gp[0], gp[1], acc), enumerate(parts), jnp.zeros_like(parts[0]))
        s_c = by_group([_dot_nt(q16, kcb_scr[g].astype(BF16)) for g in range(NSA_GROUPS)])
        s_c = s_c * ATTN_SCALE + bc_ref[...]
        e_c, den_c = _masked_softmax_parts(s_c, (blk < n_blk) & (past >= blk * NSA_BLOCK + NSA_BLOCK - 1))
        p_c = e_c / den_c
        p16 = p_c.astype(BF16)
        oc_ref[...] = by_group([_dot(p16, vcb_scr[g].astype(BF16)) for g in range(NSA_GROUPS)])
        p_sum = jnp.zeros((NSA_HEADS, LANES), F32)
        for g in range(NSA_GROUPS):
            tot = jnp.sum(p_c[g * NSA_HPG:(g + 1) * NSA_HPG], axis=0, keepdims=True)
            p_sum = jnp.where(head_grp == g, tot, p_sum)
        cur = past // NSA_BLOCK
        forced = (blk == 0) | (blk == cur) | (blk == cur - 1)
        score = jnp.where(forced, POS_INF, jnp.where(blk <= cur, p_sum, NEG_INF))
        sel_ref[...] = _top_blocks(score, min(NSA_TOPK, cur + 1))


def _nsa_sample_cmp(page_table, layer, q, cache_k, cache_v, bc, past):
    assert past % NSA_BLOCK == 0 and past // NSA_BLOCK < LANES
    nb, n_pages = page_table.shape
    pp = NSA_PAGES_PER_STEP
    assert n_pages % pp == 0
    rows = PAGE_SIZE * NSA_GROUPS
    page = lambda i: (lambda b, p, pt: (layer, pt[b, p * pp + i], 0, 0))
    grid_spec = pltpu.PrefetchScalarGridSpec(
        num_scalar_prefetch=1, grid=(nb, n_pages // pp),
        in_specs=[pl.BlockSpec((None, NSA_HEADS, HEAD_DIM), lambda b, p, pt: (b, 0, 0)),
                  pl.BlockSpec((NSA_HEADS, LANES), lambda b, p, pt: (0, 0))]
        + [pl.BlockSpec((None, None, rows, HEAD_DIM), page(i)) for i in range(pp)] * 2,
        out_specs=[pl.BlockSpec((None, NSA_HEADS, HEAD_DIM), lambda b, p, pt: (b, 0, 0)),
                   pl.BlockSpec((None, NSA_HEADS, LANES), lambda b, p, pt: (b, 0, 0))],
        scratch_shapes=[pltpu.VMEM((NSA_GROUPS, LANES, HEAD_DIM), F32),
                        pltpu.VMEM((NSA_GROUPS, LANES, HEAD_DIM), F32)])
    return pl.pallas_call(
        functools.partial(_nsa_sample_cmp_kernel, past=past, pp=pp), grid_spec=grid_spec,
        out_shape=[jax.ShapeDtypeStruct((nb, NSA_HEADS, HEAD_DIM), F32),
                   jax.ShapeDtypeStruct((nb, NSA_HEADS, LANES), F32)],
        compiler_params=_params(2), name="nsa_sample_cmp")(
            page_table, q, bc, *([cache_k] * pp), *([cache_v] * pp))


def _per_head(new_ref):
    head_grp = _shift_div(_iota((NSA_HEADS, HEAD_DIM), 0), NSA_HPG)
    out = jnp.zeros((NSA_HEADS, HEAD_DIM), F32)
    for g in range(NSA_GROUPS):
        out = jnp.where(head_grp == g, new_ref[g:g + 1, :], out)
    return out


def _nsa_sample_sel_kernel(pt_ref, q_ref, ksn_ref, vsn_ref, kwn_ref, vwn_ref, sel_ref, oc_ref, gate_ref,
                           kw_ref, vw_ref, bk_ref, bw_ref, b0_ref, *rest, pp):
    ks_refs, vs_refs = rest[:pp], rest[pp:2 * pp]
    o_ref, m_scr, l_scr, acc_scr = rest[2 * pp:]
    step = pl.program_id(1)
    per_page = PAGE_SIZE // NSA_BLOCK
    q = q_ref[...]
    q16 = q.astype(BF16)
    b0 = b0_ref[...]
    own = lambda cols: ((_iota((NSA_HEADS, cols), 1) & (NSA_GROUPS - 1))
                        == _shift_div(_iota((NSA_HEADS, cols), 0), NSA_HPG))

    @pl.when(step == 0)
    def _():
        m_scr[...] = jnp.sum(q * _per_head(ksn_ref), axis=-1, keepdims=True) * ATTN_SCALE + b0
        l_scr[...] = jnp.ones(l_scr.shape, F32)
        acc_scr[...] = _per_head(vsn_ref).astype(BF16).astype(F32)

    cols = ks_refs[0].shape[0]
    sel16 = sel_ref[...].astype(BF16)
    key_blk = _shift_div(_iota((LANES, cols), 1), NSA_GROUPS * NSA_BLOCK)
    s_parts = []
    for i, ks_ref in enumerate(ks_refs):
        pg = step * pp + i
        expand = jnp.where(_iota((LANES, cols), 0) == pg * per_page + key_blk, 1.0, 0.0)
        picked = _dot(sel16, expand.astype(BF16))
        start = pl.multiple_of(pg * cols, cols)
        s = _dot_nt(q16, ks_ref[...].astype(BF16)) * ATTN_SCALE + bk_ref[:, pl.ds(start, cols)]
        s_parts.append(jnp.where(own(cols) & (picked > 0.5), s, NEG_INF))
    _softmax_step((m_scr, l_scr, acc_scr), s_parts, vs_refs)

    @pl.when(step == pl.num_programs(1) - 1)
    def _():
        o_s = acc_scr[...] / jnp.maximum(l_scr[...], 1e-30)
        wcols = kw_ref.shape[0]
        s_w = _dot_nt(q16, kw_ref[...].astype(BF16)) * ATTN_SCALE + bw_ref[...]
        s_w = jnp.where(own(wcols), s_w, NEG_INF)
        s_n = jnp.sum(q * _per_head(kwn_ref), axis=-1, keepdims=True) * ATTN_SCALE + b0
        m_w = jnp.maximum(jnp.max(s_w, axis=-1, keepdims=True), s_n)
        e_w = jnp.exp(s_w - m_w)
        e_n = jnp.exp(s_n - m_w)
        den = jnp.maximum(jnp.sum(e_w, axis=-1, keepdims=True) + e_n, 1e-30)
        p_w = (e_w / den).astype(BF16)
        p_n = (e_n / den).astype(BF16).astype(F32)
        o_w = _dot(p_w, vw_ref[...].astype(BF16)) + p_n * _per_head(vwn_ref).astype(BF16).astype(F32)
        gate = gate_ref[...]
        o_ref[...] = gate[:, 0:1] * oc_ref[...] + gate[:, 1:2] * o_s + gate[:, 2:3] * o_w


def _nsa_sample_sel(page_table, layer, q, ks_new, vs_new, kw_new, vw_new, sel, o_c, gates,
                    cache_ks, cache_vs, win_k, win_v, bk, bw, b0):
    nb, n_pages = page_table.shape
    pp = NSA_PAGES_PER_STEP
    assert n_pages % pp == 0
    wrows = win_k.shape[2]
    rows = PAGE_SIZE * NSA_GROUPS
    per_b = lambda shape: pl.BlockSpec((None,) + shape, lambda b, p, pt: (b, 0, 0))
    page = lambda i: (lambda b, p, pt: (layer, pt[b, p * pp + i], 0, 0))
    const = lambda shape: pl.BlockSpec(shape, lambda b, p, pt: (0, 0))
    new = per_b((NSA_GROUPS, HEAD_DIM))
    grid_spec = pltpu.PrefetchScalarGridSpec(
        num_scalar_prefetch=1, grid=(nb, n_pages // pp),
        in_specs=[per_b((NSA_HEADS, HEAD_DIM)), new, new, new, new,
                  per_b((NSA_HEADS, LANES)), per_b((NSA_HEADS, HEAD_DIM)), per_b((NSA_HEADS, 3)),
                  pl.BlockSpec((None, None, wrows, HEAD_DIM), lambda b, p, pt: (layer, b, 0, 0)),
                  pl.BlockSpec((None, None, wrows, HEAD_DIM), lambda b, p, pt: (layer, b, 0, 0)),
                  const((NSA_HEADS, n_pages * rows)), const((NSA_HEADS, wrows)), const((NSA_HEADS, 1))]
        + [pl.BlockSpec((None, None, rows, HEAD_DIM), page(i)) for i in range(pp)] * 2,
        out_specs=per_b((NSA_HEADS, HEAD_DIM)),
        scratch_shapes=[pltpu.VMEM((NSA_HEADS, 1), F32), pltpu.VMEM((NSA_HEADS, 1), F32),
                        pltpu.VMEM((NSA_HEADS, HEAD_DIM), F32)])
    return pl.pallas_call(
        functools.partial(_nsa_sample_sel_kernel, pp=pp), grid_spec=grid_spec,
        out_shape=jax.ShapeDtypeStruct((nb, NSA_HEADS, HEAD_DIM), F32),
        compiler_params=_params(2), name="nsa_sample_sel")(
            page_table, q, ks_new, vs_new, kw_new, vw_new, sel, o_c, gates,
            win_k, win_v, bk, bw, b0, *([cache_ks] * pp), *([cache_vs] * pp))


def _gla_gate_kernel(g_ref, w_ref, b_ref, o_ref):
    z = _dot(g_ref[...].astype(BF16), w_ref[...]) + b_ref[...]
    o_ref[...] = _log_sigmoid(z) * (1.0 / GLA_TAU)


def _gla_gate(hc, col_g, w_g2_pad, b_g, tm):
    t = hc.shape[0]
    n = w_g2_pad.shape[1]
    return pl.pallas_call(
        _gla_gate_kernel, grid=(t // tm,),
        in_specs=[pl.BlockSpec((tm, LANES), lambda i: (i, col_g)),
                  pl.BlockSpec((LANES, n), lambda i: (0, 0)),
                  pl.BlockSpec((1, n), lambda i: (0, 0))],
        out_specs=pl.BlockSpec((tm, n), lambda i: (i, 0)),
        out_shape=jax.ShapeDtypeStruct((t, n), F32),
        compiler_params=_params(1), name="gla_gate")(hc, w_g2_pad, b_g.reshape(1, n))


def _gla_out_gate(o, r, g_norm):
    o = o * lax.rsqrt(jnp.mean(o * o, axis=-1, keepdims=True) + LN_EPS) * g_norm
    return o * (r * _sigmoid(r))


def _gla_prompt_kernel(q_ref, k_ref, v_ref, r_ref, la_ref, gn_ref, o_ref, st_ref, stt_scr):
    c = pl.program_id(1)
    ch = q_ref.shape[0]

    @pl.when(c == 0)
    def _():
        stt_scr[...] = jnp.zeros(stt_scr.shape, F32)

    la = la_ref[...]
    row = _iota((ch, ch), 0)
    col = _iota((ch, ch), 1)
    b_all = _dot(jnp.where(row >= col, 1.0, 0.0), la, HIGHEST)
    sub_start = (row >> int(math.log2(GLA_SUB))) << int(math.log2(GLA_SUB))
    base_all = _dot(jnp.where(col < sub_start, 1.0, 0.0), la, HIGHEST)
    krow = _iota((ch, 1), 0)
    for h in range(GLA_HEADS):
        dk = slice(h * GLA_DK, (h + 1) * GLA_DK)
        dv = slice(h * GLA_DV, (h + 1) * GLA_DV)
        b, base = b_all[:, dk], base_all[:, dk]
        q = q_ref[:, dk] * (GLA_DK ** -0.5)
        k = k_ref[:, dk]
        v16 = v_ref[:, dv].astype(BF16)
        stt = stt_scr[h]

        o = _dot_nt((q * jnp.exp(b)).astype(BF16), stt.astype(BF16))
        q_in = (q * jnp.exp(b - base)).astype(BF16)
        att_rows = []
        for i in range(ch // GLA_SUB):
            lo, hi = i * GLA_SUB, (i + 1) * GLA_SUB
            expo = jnp.where(krow < hi, base[lo:lo + 1] - b, NEG_INF)
            k_in = (k * jnp.exp(expo)).astype(BF16)
            att_rows.append(_dot_nt(q_in[lo:hi], k_in))
        att = jnp.where(row >= col, jnp.concatenate(att_rows, axis=0), 0.0)
        o = o + _dot(att.astype(BF16), v16)

        b_last = b[ch - 1:ch]
        k_dec = (k * jnp.exp(b_last - b)).astype(BF16)
        stt = stt * jnp.exp(b_last) + lax.dot_general(v16, k_dec, (((0,), (0,)), ((), ())),
                                                       preferred_element_type=F32)
        stt_scr[h] = stt
        o_ref[:, dv] = _gla_out_gate(o, r_ref[:, dv], gn_ref[...])

    @pl.when(c == pl.num_programs(1) - 1)
    def _():
        for h in range(GLA_HEADS):
            st_ref[h] = stt_scr[h].T


def _gla_prompt(hc, la, g_norm, n, s):
    ch = GLA_CHUNK if s % GLA_CHUNK == 0 else s
    nc = s // ch
    rows = lambda b, c: b * nc + c
    qk_w, v_w = GLA_HEADS * GLA_DK, GLA_HEADS * GLA_DV
    assert v_w == 2 * qk_w
    return pl.pallas_call(
        _gla_prompt_kernel, grid=(n, nc),
        in_specs=[pl.BlockSpec((ch, qk_w), lambda b, c: (rows(b, c), 0)),
                  pl.BlockSpec((ch, qk_w), lambda b, c: (rows(b, c), 1)),
                  pl.BlockSpec((ch, v_w), lambda b, c: (rows(b, c), 1)),
                  pl.BlockSpec((ch, v_w), lambda b, c: (rows(b, c), 2)),
                  pl.BlockSpec((ch, qk_w), lambda b, c: (rows(b, c), 0)),
                  pl.BlockSpec((1, GLA_DV), lambda b, c: (0, 0))],
        out_specs=[pl.BlockSpec((ch, v_w), lambda b, c: (rows(b, c), 0)),
                   pl.BlockSpec((None, GLA_HEADS, GLA_DK, GLA_DV), lambda b, c: (b, 0, 0, 0))],
        out_shape=[jax.ShapeDtypeStruct((n * s, v_w), F32),
                   jax.ShapeDtypeStruct((n, GLA_HEADS, GLA_DK, GLA_DV), F32)],
        scratch_shapes=[pltpu.VMEM((GLA_HEADS, GLA_DV, GLA_DK), F32)],
        compiler_params=_params(2), name="gla_prompt")(hc, hc, hc, hc, la, g_norm.reshape(1, GLA_DV))


def _gla_sample_kernel(q_ref, k_ref, la_ref, v_ref, r_ref, gn_ref, s0_ref, o_ref, st_ref):
    for h in range(GLA_HEADS):
        st = jnp.exp(la_ref[h]) * s0_ref[h] + k_ref[h] * v_ref[h]
        st_ref[h] = st
        o = jnp.sum((q_ref[h] * (GLA_DK ** -0.5)) * st, axis=0, keepdims=True)
        o_ref[h] = _gla_out_gate(o, r_ref[h], gn_ref[...])


def _gla_sample(q, k, la, v, r, g_norm, s0):
    nb = q.shape[0]
    col = pl.BlockSpec((None, GLA_HEADS, GLA_DK, 1), lambda b: (b, 0, 0, 0))
    rowv = pl.BlockSpec((None, GLA_HEADS, 1, GLA_DV), lambda b: (b, 0, 0, 0))
    full = pl.BlockSpec((None, GLA_HEADS, GLA_DK, GLA_DV), lambda b: (b, 0, 0, 0))
    return pl.pallas_call(
        _gla_sample_kernel, grid=(nb,),
        in_specs=[col, col, col, rowv, rowv, pl.BlockSpec((1, GLA_DV), lambda b: (0, 0)), full],
        out_specs=[rowv, full],
        out_shape=[jax.ShapeDtypeStruct((nb, GLA_HEADS, 1, GLA_DV), F32),
                   jax.ShapeDtypeStruct((nb, GLA_HEADS, GLA_DK, GLA_DV), F32)],
        compiler_params=_params(1), name="gla_sample")(q, k, la, v, r, g_norm.reshape(1, GLA_DV), s0)


def _router_kernel(x_ref, w_ref, b_ref, idx_ref, gate_ref):
    logits = _dot(x_ref[...], w_ref[...], HIGHEST) + b_ref[...]
    lane = _iota(logits.shape, 1).astype(F32)
    logits = jnp.where(lane < N_EXPERTS, logits, NEG_INF)
    top_v = jnp.full(logits.shape, NEG_INF, F32)
    top_i = jnp.zeros(logits.shape, F32)
    for kk in range(TOP_K):
        mx = jnp.max(logits, axis=-1, keepdims=True)
        first = jnp.min(jnp.where(logits == mx, lane, 1e9), axis=-1, keepdims=True)
        top_v = jnp.where(lane == kk, mx, top_v)
        top_i = jnp.where(lane == kk, first, top_i)
        logits = jnp.where(lane == first, NEG_INF, logits)
    e = jnp.exp(top_v - jnp.max(top_v, axis=-1, keepdims=True))
    gate_ref[...] = e / jnp.sum(e, axis=-1, keepdims=True)
    idx_ref[...] = top_i.astype(jnp.int32)


def _router(x, w_pad, b_pad, tm):
    t, d = x.shape
    return pl.pallas_call(
        _router_kernel, grid=(t // tm,),
        in_specs=[pl.BlockSpec((tm, d), lambda i: (i, 0)),
                  pl.BlockSpec((d, LANES), lambda i: (0, 0)),
                  pl.BlockSpec((1, LANES), lambda i: (0, 0))],
        out_specs=[pl.BlockSpec((tm, LANES), lambda i: (i, 0))] * 2,
        out_shape=[jax.ShapeDtypeStruct((t, LANES), jnp.int32), jax.ShapeDtypeStruct((t, LANES), F32)],
        compiler_params=_params(1), name="router")(x, w_pad, b_pad)


def _moe_gate_up_kernel(te_ref, tf_ref, tv_ref, x_ref, wg_ref, wu_ref, bg_ref, bu_ref, o_ref, wg16, wu16):
    i = pl.program_id(1)

    @pl.when(tf_ref[i] == 1)
    def _():
        wg16[...] = wg_ref[...].astype(BF16)
        wu16[...] = wu_ref[...].astype(BF16)

    @pl.when(tv_ref[i] == 1)
    def _():
        x = x_ref[...].astype(BF16)
        for lo in range(0, o_ref.shape[1], MOE_ACT_CHUNK):
            cs = slice(lo, lo + MOE_ACT_CHUNK)
            g = jnp.minimum(_dot(x, wg16[:, cs]) + bg_ref[:, cs], SWIGLU_LIMIT)
            u = jnp.clip(_dot(x, wu16[:, cs]) + bu_ref[:, cs], -SWIGLU_LIMIT, SWIGLU_LIMIT)
            o_ref[:, cs] = ((u + 1.0) * g * _sigmoid(SWIGLU_ALPHA * g)).astype(BF16)

    @pl.when(tv_ref[i] == 0)
    def _():
        o_ref[...] = jnp.zeros(o_ref.shape, BF16)


def _moe_gate_up(te, tf, tv, xs, w_gu, b_gu, layer, tmm, tn):
    p_pad, d = xs.shape
    de = w_gu.shape[-1] // 2
    nj = de // tn
    grid_spec = pltpu.PrefetchScalarGridSpec(
        num_scalar_prefetch=3, grid=(nj, p_pad // tmm),
        in_specs=[pl.BlockSpec((tmm, d), lambda j, i, te, tf, tv: (i, 0)),
                  pl.BlockSpec((None, None, d, tn), lambda j, i, te, tf, tv: (layer, te[i], 0, j)),
                  pl.BlockSpec((None, None, d, tn), lambda j, i, te, tf, tv: (layer, te[i], 0, nj + j)),
                  pl.BlockSpec((None, None, 1, tn), lambda j, i, te, tf, tv: (layer, te[i], 0, j)),
                  pl.BlockSpec((None, None, 1, tn), lambda j, i, te, tf, tv: (layer, te[i], 0, nj + j))],
        out_specs=pl.BlockSpec((tmm, tn), lambda j, i, te, tf, tv: (i, j)),
        scratch_shapes=[pltpu.VMEM((d, tn), BF16), pltpu.VMEM((d, tn), BF16)])
    b4 = b_gu.reshape(b_gu.shape[0], b_gu.shape[1], 1, b_gu.shape[2])
    return pl.pallas_call(
        _moe_gate_up_kernel, grid_spec=grid_spec,
        out_shape=jax.ShapeDtypeStruct((p_pad, de), BF16),
        compiler_params=_params(2), name="moe_gate_up")(te, tf, tv, xs, w_gu, w_gu, b4, b4)


def _moe_down_kernel(te_ref, tf_ref, tv_ref, a_ref, w_ref, b_ref, o_ref, w16):
    i = pl.program_id(1)

    @pl.when(tf_ref[i] == 1)
    def _():
        w16[...] = w_ref[...].astype(BF16)

    @pl.when(tv_ref[i] == 1)
    def _():
        o_ref[...] = _dot(a_ref[...], w16[...]) + b_ref[...]

    @pl.when(tv_ref[i] == 0)
    def _():
        o_ref[...] = jnp.zeros(o_ref.shape, F32)


def _moe_down(te, tf, tv, a, w_d, b_d, layer, tmm, tn):
    p_pad, de = a.shape
    d = w_d.shape[-1]
    grid_spec = pltpu.PrefetchScalarGridSpec(
        num_scalar_prefetch=3, grid=(d // tn, p_pad // tmm),
        in_specs=[pl.BlockSpec((tmm, de), lambda j, i, te, tf, tv: (i, 0)),
                  pl.BlockSpec((None, None, de, tn), lambda j, i, te, tf, tv: (layer, te[i], 0, j)),
                  pl.BlockSpec((None, None, 1, tn), lambda j, i, te, tf, tv: (layer, te[i], 0, j))],
        out_specs=pl.BlockSpec((tmm, tn), lambda j, i, te, tf, tv: (i, j)),
        scratch_shapes=[pltpu.VMEM((de, tn), BF16)])
    b4 = b_d.reshape(b_d.shape[0], b_d.shape[1], 1, b_d.shape[2])
    return pl.pallas_call(
        _moe_down_kernel, grid_spec=grid_spec,
        out_shape=jax.ShapeDtypeStruct((p_pad, d), F32),
        compiler_params=_params(2), name="moe_down")(te, tf, tv, a, w_d, b4)


def _moe_combine_ln_kernel(*refs):
    y_refs = refs[:TOP_K]
    gate_ref, x_ref, g_ref, b_ref, o_ref, o16_ref = refs[TOP_K:]
    gate = gate_ref[...]
    out = gate[:, 0:1] * y_refs[0][...]
    for kk in range(1, TOP_K):
        out = out + gate[:, kk:kk + 1] * y_refs[kk][...]
    res = _layernorm_rows(DEEPNORM_ALPHA * x_ref[...] + out, g_ref[...], b_ref[...])
    o_ref[...] = res
    o16_ref[...] = res.astype(BF16)


def _moe_combine_ln(yk, gate, x, g, b, tm):
    t, d = x.shape
    nt = t // tm
    slot = lambda kk: pl.BlockSpec((tm, d), lambda i: (kk * nt + i, 0))
    return pl.pallas_call(
        _moe_combine_ln_kernel, grid=(nt,),
        in_specs=[slot(kk) for kk in range(TOP_K)]
        + [pl.BlockSpec((tm, LANES), lambda i: (i, 0)),
           pl.BlockSpec((tm, d), lambda i: (i, 0)),
           pl.BlockSpec((1, d), lambda i: (0, 0)), pl.BlockSpec((1, d), lambda i: (0, 0))],
        out_specs=[pl.BlockSpec((tm, d), lambda i: (i, 0))] * 2,
        out_shape=[jax.ShapeDtypeStruct((t, d), F32), jax.ShapeDtypeStruct((t, d), BF16)],
        compiler_params=_params(1), name="moe_combine_ln")(
            *([yk] * TOP_K), gate, x, g.reshape(1, d), b.reshape(1, d))


MOE_TILE_ROWS = 256
MOE_TILE_COLS = 1024
MOE_DOWN_TILE_COLS = 2048
MOE_ACT_CHUNK = 256
ROW_TILE = 128


def _moe(x, x16, layer, router_w, router_b, w_gu, b_gu, w_d, b_d, ln_g, ln_b):
    t, d = x.shape
    w_pad = jnp.pad(router_w[layer], ((0, 0), (0, LANES - N_EXPERTS)))
    b_pad = jnp.pad(router_b[layer], (0, LANES - N_EXPERTS)).reshape(1, LANES)
    top_i, gate = _router(x, w_pad, b_pad, ROW_TILE)

    n_pairs = t * TOP_K
    n_tiles = n_pairs // MOE_TILE_ROWS + N_EXPERTS
    e_flat = top_i[:, :TOP_K].reshape(n_pairs)
    onehot = (e_flat[:, None] == jnp.arange(N_EXPERTS, dtype=jnp.int32)[None, :]).astype(jnp.int32)
    csum = jnp.cumsum(onehot, axis=0)
    counts = csum[-1]
    rank = jnp.sum(onehot * csum, axis=1) - 1
    tiles_per = (counts + MOE_TILE_ROWS - 1) // MOE_TILE_ROWS
    tile_end = jnp.cumsum(tiles_per)
    dest = ((tile_end - tiles_per) * MOE_TILE_ROWS)[e_flat] + rank
    row_token = jnp.zeros((n_tiles * MOE_TILE_ROWS,), jnp.int32).at[dest].set(
        jnp.arange(n_pairs, dtype=jnp.int32) // TOP_K)
    tile_id = jnp.arange(n_tiles, dtype=jnp.int32)
    tv = (tile_id < tile_end[-1]).astype(jnp.int32)
    last_tile = jnp.minimum(tile_id, tile_end[-1] - 1)
    te = jnp.sum((tile_end[None, :] <= last_tile[:, None]).astype(jnp.int32), axis=1)
    tf = jnp.concatenate([jnp.ones((1,), jnp.int32), (te[1:] != te[:-1]).astype(jnp.int32)])

    xs = x[row_token]
    act = _moe_gate_up(te, tf, tv, xs, w_gu, b_gu, layer, MOE_TILE_ROWS, MOE_TILE_COLS)
    ys = _moe_down(te, tf, tv, act, w_d, b_d, layer, MOE_TILE_ROWS, MOE_DOWN_TILE_COLS)
    yk = ys[dest.reshape(t, TOP_K).T.reshape(n_pairs)]
    return _moe_combine_ln(yk, gate, x, ln_g, ln_b, ROW_TILE)


PROJ_ROW_TILE = 640
PROJ_A_ROW_TILE = 512
PROJ_A_COL_TILE = 512
PROJ_C_COL_TILE = 896
OUT_ROW_TILE = 320
FOX_Q_TILE = 256
NSA_Q_TILE = 128


def _attention_layer(x, x16, i, n, s, nb, caches, page_table, w_in, b_f, w_out, rel_table, ln_g, ln_b):
    (cache_fox_k, cache_fox_v, cache_fox_logf, cache_cmp_k, cache_cmp_v, cache_sel_k, cache_sel_v,
     cache_win_k, cache_win_v) = caches
    tp = n * s
    past = page_table.shape[1] * PAGE_SIZE
    win = cache_win_k.shape[2]
    o_f = 3 * FOX_W
    o_n = o_f + FOX_HEADS
    o_g = o_n + NSA_W + 6 * NSA_KV_W
    w_main = jnp.concatenate([w_in[:, :o_f], w_in[:, o_n:o_g]], axis=1).astype(BF16)
    n_small = FOX_HEADS + 3 * NSA_HEADS
    w_small = jnp.pad(jnp.concatenate([w_in[:, o_f:o_n], w_in[:, o_g:]], axis=1), ((0, 0), (0, LANES - n_small)))
    b_small = jnp.pad(b_f, (0, LANES - FOX_HEADS))

    groups = [[FOX_W]] * 3 + [[NSA_W]] + [[NSA_KV_W, NSA_KV_W]] * 3
    qf_p, kf_p, vf_p, qn_p, *kv_p = _mm_split(x16, tp, w_main, groups, PROJ_A_ROW_TILE, PROJ_A_COL_TILE,
                                              "proj_a")
    hs = _mm(x16[tp:], w_main, nb, PROJ_A_COL_TILE, "proj_a_sample")
    logf_all, logf_t, sg_all = _gates_a(x, w_small, b_small, ROW_TILE)
    logf = logf_all[:, :FOX_HEADS]
    gates = sg_all[:, FOX_HEADS:n_small]

    c_row = _cumsum_seq(logf_t, n, s)
    o_fox_p = _fox_prompt(qf_p, kf_p, vf_p, c_row.reshape(n, FOX_HEADS, s, 1),
                          c_row.reshape(n, FOX_HEADS, 1, s), n, s, FOX_Q_TILE)
    tiles, wtiles, cmpb, bk, bw, bc = _nsa_bias_tables(rel_table, s, NSA_Q_TILE, past, win)
    gates_g = gates.reshape(-1, NSA_GROUPS, 3 * NSA_HPG).transpose(1, 0, 2)
    col_nsa = 3 * FOX_W
    o_nsa_p = _nsa_prompt(qn_p, kv_p, gates_g, tiles, wtiles, cmpb, n, s, NSA_Q_TILE)

    seg = lambda lo, w: hs[:, lo:lo + w]
    q_fox, k_fox, v_fox = seg(0, FOX_W), seg(FOX_W, FOX_W), seg(2 * FOX_W, FOX_W)
    q_nsa = seg(col_nsa, NSA_W).reshape(nb, NSA_HEADS, HEAD_DIM)
    kv = [seg(col_nsa + NSA_W + j * NSA_KV_W, NSA_KV_W).reshape(nb, NSA_GROUPS, HEAD_DIM) for j in range(6)]
    rows_view = lambda c: c.reshape(c.shape[0], c.shape[1], c.shape[2] * c.shape[3], HEAD_DIM)
    heads = lambda a: a.reshape(nb, FOX_HEADS, HEAD_DIM)
    o_fox_s = _fox_sample(page_table, i, heads(q_fox), heads(k_fox), heads(v_fox),
                          logf[tp:].reshape(nb, FOX_HEADS, 1),
                          rows_view(cache_fox_k), rows_view(cache_fox_v),
                          jnp.swapaxes(cache_fox_logf, 2, 3))
    o_c, sel = _nsa_sample_cmp(page_table, i, q_nsa, rows_view(cache_cmp_k), rows_view(cache_cmp_v), bc, past)
    o_nsa_s = _nsa_sample_sel(
        page_table, i, q_nsa, kv[2], kv[3], kv[4], kv[5], sel, o_c,
        gates[tp:].reshape(nb, NSA_HEADS, 3), rows_view(cache_sel_k), rows_view(cache_sel_v),
        rows_view(cache_win_k), rows_view(cache_win_v), bk, bw, rel_table[0].reshape(NSA_HEADS, 1))

    o_fox = jnp.concatenate([o_fox_p, o_fox_s.reshape(nb, FOX_W)], axis=0)
    o_nsa = jnp.concatenate([o_nsa_p, o_nsa_s.reshape(nb, NSA_W)], axis=0)
    w_out16 = w_out.astype(BF16)
    x_new, x_new16 = _mm_res_ln([o_fox, o_nsa], [w_out16[:FOX_W], w_out16[FOX_W:]], x, ln_g, ln_b,
                                OUT_ROW_TILE, "attn_out_ln")

    wb = min(NSA_WINDOW, s)

    def state(p, lo, heads, keep=s):
        p = p.reshape(n, s, heads, HEAD_DIM)[:, s - keep:]
        return p, hs[:, lo:lo + heads * HEAD_DIM].reshape(nb, 1, heads, HEAD_DIM)

    kv_lo = col_nsa + NSA_W
    states = [state(kf_p, FOX_W, FOX_HEADS), state(vf_p, 2 * FOX_W, FOX_HEADS),
              (logf[:tp].reshape(n, s, FOX_HEADS), logf[tp:].reshape(nb, 1, FOX_HEADS))]
    states += [state(kv_p[j], kv_lo + j * NSA_KV_W, NSA_GROUPS) for j in range(4)]
    states += [state(kv_p[j], kv_lo + j * NSA_KV_W, NSA_GROUPS, wb) for j in (4, 5)]
    return x_new, x_new16, states


def _gla_layer(x, x16, i, n, s, nb, state_gla, w_in, w_g2, b_g, g_norm, w_out, ln_g, ln_b):
    tp = n * s
    qk_w = GLA_HEADS * GLA_DK
    v_w = GLA_HEADS * GLA_DV
    o_g = 2 * qk_w + v_w
    o_r = o_g + GLA_GATE_RANK
    w_main = jnp.concatenate([w_in[:, :o_g], w_in[:, o_r:], w_in[:, o_g:o_r],
                              jnp.zeros((w_in.shape[0], LANES - GLA_GATE_RANK), w_in.dtype)], axis=1).astype(BF16)
    hc = _mm(x16, w_main, PROJ_ROW_TILE, PROJ_C_COL_TILE, "proj_c")
    w_g2_pad = jnp.pad(w_g2, ((0, LANES - GLA_GATE_RANK), (0, 0))).astype(BF16)
    la = _gla_gate(hc, (o_g + v_w) // LANES, w_g2_pad, b_g, PROJ_ROW_TILE)

    og_p, st_p = _gla_prompt(hc, la, g_norm, n, s)
    hs = hc[tp:]
    col = lambda a: a.reshape(nb, GLA_HEADS, GLA_DK, 1)
    rowv = lambda a: a.reshape(nb, GLA_HEADS, 1, GLA_DV)
    og_s, st_s = _gla_sample(col(hs[:, :qk_w]), col(hs[:, qk_w:2 * qk_w]), col(la[tp:]),
                             rowv(hs[:, 2 * qk_w:o_g]), rowv(hs[:, o_g:o_g + v_w]), g_norm, state_gla[i])
    og = jnp.concatenate([og_p, og_s.reshape(nb, v_w)], axis=0)
    x_new, x_new16 = _mm_res_ln([og], [w_out.astype(BF16)], x, ln_g, ln_b, OUT_ROW_TILE, "gla_out_ln")
    return x_new, x_new16, st_p, st_s


def kernel(x_prompt, x_sample, cache_fox_k, cache_fox_v, cache_fox_logf, cache_nsa_cmp_k, cache_nsa_cmp_v,
           cache_nsa_sel_k, cache_nsa_sel_v, cache_nsa_win_k, cache_nsa_win_v, state_gla, page_table,
           w_in_a, b_forget, w_out_a, rel_bias_table, w_in_c, w_gla_gate2, b_gla_gate, gla_norm_g, w_out_c,
           ln_g, ln_b, router_w, router_b, w_gate_up, b_gate_up, w_down, b_down):
    n, s, d = x_prompt.shape
    nb = x_sample.shape[0]
    assert x_sample.shape[1] == 1
    tp = n * s
    x = jnp.concatenate([x_prompt.reshape(tp, d), x_sample.reshape(nb, d)], axis=0)
    x16 = x.astype(BF16)
    caches = (cache_fox_k, cache_fox_v, cache_fox_logf, cache_nsa_cmp_k, cache_nsa_cmp_v,
              cache_nsa_sel_k, cache_nsa_sel_v, cache_nsa_win_k, cache_nsa_win_v)
    attn_states, gla_p, gla_s = [], [], []
    for layer in range(DEPTH):
        i = layer // 2
        if layer % 2 == 0:
            x, x16, st = _attention_layer(x, x16, i, n, s, nb, caches, page_table, w_in_a[i], b_forget[i],
                                          w_out_a[i], rel_bias_table, ln_g[layer, 0], ln_b[layer, 0])
            attn_states.append(st)
        else:
            x, x16, st_p, st_s = _gla_layer(x, x16, i, n, s, nb, state_gla, w_in_c[i], w_gla_gate2[i],
                                            b_gla_gate[i], gla_norm_g[i], w_out_c[i],
                                            ln_g[layer, 0], ln_b[layer, 0])
            gla_p.append(st_p)
            gla_s.append(st_s)
        x, x16 = _moe(x, x16, layer, router_w, router_b, w_gate_up, b_gate_up, w_down, b_down,
                      ln_g[layer, 1], ln_b[layer, 1])
    outs = [x[:tp].reshape(n, s, d), x[tp:].reshape(nb, 1, d)]
    for j in range(9):
        outs.append(jnp.stack([st[j][0] for st in attn_states]))
        outs.append(jnp.stack([st[j][1] for st in attn_states]))
    outs.append(jnp.stack(gla_p))
    outs.append(jnp.stack(gla_s))
    return tuple(outs)
```

```python
import functools
import math

import jax
import jax.numpy as jnp
import numpy as np
from jax import lax
from jax.experimental import pallas as pl
from jax.experimental.pallas import tpu as pltpu

F32 = jnp.float32
BF16 = jnp.bfloat16
HIGHEST = lax.Precision.HIGHEST
NEG_INF = float("-inf")
POS_INF = float("inf")

D_MODEL = 2048
PAGE_SIZE = 128
HEAD_DIM = 128
FOX_HEADS = 8
NSA_HEADS = 8
NSA_GROUPS = 2
NSA_HPG = NSA_HEADS // NSA_GROUPS
NSA_BLOCK = 64
NSA_TOPK = 8
NSA_WINDOW = 512
REL_BUCKETS = 32
REL_MAX_DIST = 1024
GLA_HEADS = 4
GLA_DK = D_MODEL // 2 // GLA_HEADS
GLA_DV = D_MODEL // GLA_HEADS
GLA_GATE_RANK = 16
GLA_TAU = 16.0
GLA_CHUNK = 64
GLA_SUB = 16
N_EXPERTS = 32
TOP_K = 4
SWIGLU_LIMIT = 7.0
SWIGLU_ALPHA = 1.702
LN_EPS = 1e-5
DEPTH = 2
DEEPNORM_ALPHA = (2 * DEPTH) ** 0.25
FOX_W = FOX_HEADS * HEAD_DIM
NSA_W = NSA_HEADS * HEAD_DIM
NSA_KV_W = NSA_GROUPS * HEAD_DIM
ATTN_SCALE = HEAD_DIM ** -0.5

VMEM_LIMIT_BYTES = 56 * 1024 * 1024
LANES = 128

_REL_EXACT = REL_BUCKETS // 2
_REL_THRESHOLDS = tuple(
    (_REL_EXACT * 8) if 2 * j == (REL_BUCKETS - _REL_EXACT) else
    math.ceil(_REL_EXACT * (REL_MAX_DIST / _REL_EXACT) ** (j / (REL_BUCKETS - _REL_EXACT)))
    for j in range(1, REL_BUCKETS - _REL_EXACT))


def _params(n_axes):
    return pltpu.CompilerParams(dimension_semantics=("arbitrary",) * n_axes,
                                vmem_limit_bytes=VMEM_LIMIT_BYTES)


def _iota(shape, dim, dtype=jnp.int32):
    return lax.broadcasted_iota(dtype, shape, dim)


def _shift_div(x, d):
    k = d.bit_length() - 1
    assert d == 1 << k
    return lax.shift_right_logical(x, jnp.full(x.shape, k, jnp.int32))


def _sigmoid(x):
    return 1.0 / (1.0 + jnp.exp(-x))


def _log_sigmoid(x):
    return jnp.minimum(x, 0.0) - jnp.log(1.0 + jnp.exp(-jnp.abs(x)))


def _dot_nt(a, b, precision=None):
    return lax.dot_general(a, b, (((1,), (1,)), ((), ())), precision=precision,
                           preferred_element_type=F32)


def _dot(a, b, precision=None):
    return jnp.dot(a, b, precision=precision, preferred_element_type=F32)


def _masked_softmax_parts(s, mask):
    s = jnp.where(mask, s, NEG_INF)
    m = jnp.max(s, axis=-1, keepdims=True)
    m = jnp.where(m == NEG_INF, 0.0, m)
    e = jnp.exp(s - m)
    return e, jnp.maximum(jnp.sum(e, axis=-1, keepdims=True), 1e-30)


def _online_update_t(carry, s, v_bf16):
    m, l, acc = carry
    m_new = jnp.maximum(m, jnp.max(s, axis=0, keepdims=True))
    m_safe = jnp.where(m_new == NEG_INF, 0.0, m_new)
    p = jnp.exp(s - m_safe)
    alpha = jnp.exp(m - m_safe)
    l = alpha * l + jnp.sum(p, axis=0, keepdims=True)
    pv = lax.dot_general(v_bf16, p.astype(BF16), (((0,), (0,)), ((), ())), preferred_element_type=F32)
    return m_new, l, alpha * acc + pv


def _rel_bucket(dist):
    n = jnp.maximum(dist, 0)
    large = jnp.full(n.shape, _REL_EXACT, jnp.int32)
    for t in _REL_THRESHOLDS:
        large = large + jnp.where(n >= t, 1, 0)
    return jnp.where(n < _REL_EXACT, n, large)


def _rel_bias(bucket, table_ref, head):
    out = jnp.zeros(bucket.shape, F32)
    for b in range(REL_BUCKETS):
        out = jnp.where(bucket == b, table_ref[b, head], out)
    return out


def _top_blocks(score, n_sel):
    lane = _iota(score.shape, 1).astype(F32)
    sel = jnp.zeros(score.shape, F32)
    for _ in range(n_sel):
        mx = jnp.max(score, axis=-1, keepdims=True)
        first = jnp.min(jnp.where(score == mx, lane, 1e9), axis=-1, keepdims=True)
        hit = lane == first
        sel = jnp.where(hit & (mx > NEG_INF), 1.0, sel)
        score = jnp.where(hit, NEG_INF, score)
    return sel


def _mm_kernel(x_ref, w_ref, o_ref):
    o_ref[...] = _dot(x_ref[...], w_ref[...])


def _mm(x, w, tm, tn, name):
    m, k = x.shape
    n = w.shape[1]
    assert m % tm == 0 and n % tn == 0
    return pl.pallas_call(
        _mm_kernel, grid=(n // tn, m // tm),
        in_specs=[pl.BlockSpec((tm, k), lambda j, i: (i, 0)),
                  pl.BlockSpec((k, tn), lambda j, i: (0, j))],
        out_specs=pl.BlockSpec((tm, tn), lambda j, i: (i, j)),
        out_shape=jax.ShapeDtypeStruct((m, n), F32),
        compiler_params=_params(2), name=name)(x, w)


def _mm_split_kernel(x_ref, w_ref, *o_refs, plan):
    j = pl.program_id(1)
    res = _dot(x_ref[...], w_ref[...])
    for j_lo, j_hi, outs in plan:
        @pl.when((j >= j_lo) & (j < j_hi))
        def _(outs=outs):
            for oi, lo, hi in outs:
                o_refs[oi][...] = res[:, lo:hi]


def _mm_split(x, m, w, groups, tm, tn, name):
    k = x.shape[1]
    assert m % tm == 0 and sum(sum(g) for g in groups) == w.shape[1]
    plan, out_specs, out_shapes = [], [], []
    j0 = 0
    for g in groups:
        nblk = sum(g) // tn
        assert sum(g) == nblk * tn and (len(g) == 1 or nblk == 1)
        outs, lane = [], 0
        for width in g:
            bw = min(width, tn)
            outs.append((len(out_specs), lane, lane + bw))
            lane += bw
            out_specs.append(pl.BlockSpec(
                (tm, bw), lambda i, j, j0=j0, nblk=nblk: (i, jnp.clip(j - j0, 0, nblk - 1))))
            out_shapes.append(jax.ShapeDtypeStruct((m, width), F32))
        plan.append((j0, j0 + nblk, outs))
        j0 += nblk
    return pl.pallas_call(
        functools.partial(_mm_split_kernel, plan=plan), grid=(m // tm, j0),
        in_specs=[pl.BlockSpec((tm, k), lambda i, j: (i, 0)),
                  pl.BlockSpec((k, tn), lambda i, j: (0, j))],
        out_specs=out_specs, out_shape=out_shapes,
        compiler_params=_params(2), name=name)(x, w)


def _layernorm_rows(z, g, b):
    mu = jnp.mean(z, axis=-1, keepdims=True)
    zc = z - mu
    var = jnp.mean(zc * zc, axis=-1, keepdims=True)
    return zc * lax.rsqrt(var + LN_EPS) * g + b


def _mm_res_ln_kernel(*refs, n_in):
    a_refs, w_refs = refs[:n_in], refs[n_in:2 * n_in]
    x_ref, g_ref, b_ref, o_ref, o16_ref = refs[2 * n_in:]
    y = _dot(a_refs[0][...].astype(BF16), w_refs[0][...])
    for a_ref, w_ref in zip(a_refs[1:], w_refs[1:]):
        y = y + _dot(a_ref[...].astype(BF16), w_ref[...])
    out = _layernorm_rows(DEEPNORM_ALPHA * x_ref[...] + y, g_ref[...], b_ref[...])
    o_ref[...] = out
    o16_ref[...] = out.astype(BF16)


def _mm_res_ln(acts, weights, x, g, b, tm, name):
    t, d = x.shape
    n_in = len(acts)
    in_specs = [pl.BlockSpec((tm, a.shape[1]), lambda i: (i, 0)) for a in acts]
    in_specs += [pl.BlockSpec(w.shape, lambda i: (0, 0)) for w in weights]
    in_specs += [pl.BlockSpec((tm, d), lambda i: (i, 0)),
                 pl.BlockSpec((1, d), lambda i: (0, 0)), pl.BlockSpec((1, d), lambda i: (0, 0))]
    return pl.pallas_call(
        functools.partial(_mm_res_ln_kernel, n_in=n_in), grid=(t // tm,),
        in_specs=in_specs,
        out_specs=[pl.BlockSpec((tm, d), lambda i: (i, 0))] * 2,
        out_shape=[jax.ShapeDtypeStruct((t, d), F32), jax.ShapeDtypeStruct((t, d), BF16)],
        compiler_params=_params(1), name=name)(*acts, *weights, x, g.reshape(1, d), b.reshape(1, d))


def _gates_a_kernel(x_ref, w_ref, wt_ref, b_ref, bt_ref, logf_ref, logft_ref, sg_ref):
    x = x_ref[...]
    z = _dot(x, w_ref[...], HIGHEST)
    logf_ref[...] = _log_sigmoid(z + b_ref[...])
    sg_ref[...] = _sigmoid(z)
    zt = _dot_nt(wt_ref[...], x, HIGHEST)
    logft_ref[...] = _log_sigmoid(zt + bt_ref[...])[:FOX_HEADS]


def _gates_a(x, w_small, b_small, tm):
    t, d = x.shape
    return pl.pallas_call(
        _gates_a_kernel, grid=(t // tm,),
        in_specs=[pl.BlockSpec((tm, d), lambda i: (i, 0)),
                  pl.BlockSpec((d, LANES), lambda i: (0, 0)),
                  pl.BlockSpec((LANES, d), lambda i: (0, 0)),
                  pl.BlockSpec((1, LANES), lambda i: (0, 0)),
                  pl.BlockSpec((LANES, 1), lambda i: (0, 0))],
        out_specs=[pl.BlockSpec((tm, LANES), lambda i: (i, 0)),
                   pl.BlockSpec((FOX_HEADS, tm), lambda i: (0, i)),
                   pl.BlockSpec((tm, LANES), lambda i: (i, 0))],
        out_shape=[jax.ShapeDtypeStruct((t, LANES), F32), jax.ShapeDtypeStruct((FOX_HEADS, t), F32),
                   jax.ShapeDtypeStruct((t, LANES), F32)],
        compiler_params=_params(1), name="gates_a")(
            x, w_small, w_small.T, b_small.reshape(1, LANES), b_small.reshape(LANES, 1))


def _cumsum_kernel(x_ref, o_ref, *, blk):
    s = x_ref.shape[1]
    upper = jnp.where(_iota((blk, blk), 0) <= _iota((blk, blk), 1), 1.0, 0.0)
    carry = jnp.zeros((x_ref.shape[0], 1), F32)
    for j in range(s // blk):
        c = _dot(x_ref[:, j * blk:(j + 1) * blk], upper, HIGHEST) + carry
        o_ref[:, j * blk:(j + 1) * blk] = c
        carry = c[:, blk - 1:blk]


def _cumsum_seq(xt, n, s):
    h = xt.shape[0]
    return pl.pallas_call(
        functools.partial(_cumsum_kernel, blk=min(256, s)), grid=(n,),
        in_specs=[pl.BlockSpec((h, s), lambda i: (0, i))],
        out_specs=pl.BlockSpec((None, h, s), lambda i: (i, 0, 0)),
        out_shape=jax.ShapeDtypeStruct((n, h, s), F32),
        compiler_params=_params(1), name="fox_cumsum")(xt)


def _fox_prompt_kernel(q_ref, k_ref, v_ref, cq_ref, ck_ref, o_ref, *, tq):
    qi = pl.program_id(2)
    q = q_ref[...].astype(BF16)
    cq = cq_ref[...]

    key_minus_query = _iota((tq, tq), 0) - _iota((tq, tq), 1)

    def tile(kb, carry):
        limit = jnp.where(kb < qi, tq, jnp.where(kb == qi, 0, -tq - 1))
        start = pl.multiple_of(jnp.minimum(kb, qi) * tq, tq)
        k = k_ref[pl.ds(start, tq), :].astype(BF16)
        v = v_ref[pl.ds(start, tq), :].astype(BF16)
        s = _dot_nt(k, q) * ATTN_SCALE + cq - ck_ref[pl.ds(start, tq), :]
        return _online_update_t(carry, jnp.where(key_minus_query <= limit, s, NEG_INF), v)

    def body(j, carry):
        return tile(2 * j + 1, tile(2 * j, carry))

    init = (jnp.full((1, tq), NEG_INF, F32), jnp.zeros((1, tq), F32), jnp.zeros((HEAD_DIM, tq), F32))
    _, l, acc = lax.fori_loop(0, (qi + 2) // 2, body, init)
    o_ref[...] = (acc / jnp.maximum(l, 1e-30)).T


def _fox_prompt(q, k, v, c_col, c_row, n, s, tq):
    nq = s // tq
    kv_spec = pl.BlockSpec((s, HEAD_DIM), lambda b, hd, qi: (b, hd))
    return pl.pallas_call(
        functools.partial(_fox_prompt_kernel, tq=tq), grid=(n, FOX_HEADS, nq),
        in_specs=[pl.BlockSpec((tq, HEAD_DIM), lambda b, hd, qi: (b * nq + qi, hd)),
                  kv_spec, kv_spec,
                  pl.BlockSpec((None, None, 1, tq), lambda b, hd, qi: (b, hd, 0, qi)),
                  pl.BlockSpec((None, None, s, 1), lambda b, hd, qi: (b, hd, 0, 0))],
        out_specs=pl.BlockSpec((tq, HEAD_DIM), lambda b, hd, qi: (b * nq + qi, hd)),
        out_shape=jax.ShapeDtypeStruct((n * s, FOX_W), F32),
        compiler_params=_params(3), name="fox_prompt")(q, k, v, c_row, c_col)


def _nsa_bias_kernel(table_ref, tiles_ref, wtiles_ref, cmpb_ref, bk_ref, bw_ref, bc_ref, *, tq, past, n_win):
    d = pl.program_id(0)
    offs = _iota((tq, tq), 1) - _iota((tq, tq), 0)
    dist = d * tq + offs
    bucket = _rel_bucket(dist)
    dist_w = jnp.minimum(d, n_win - 1) * tq + offs
    bucket_w = _rel_bucket(dist_w)
    in_window = (dist_w >= 0) & (dist_w <= NSA_WINDOW)
    nb = cmpb_ref.shape[-1]
    dist_c = jnp.minimum(d, pl.num_programs(0) - 2) * tq + _iota((tq, nb), 0) \
        - (_iota((tq, nb), 1) * NSA_BLOCK + NSA_BLOCK - 1)
    bucket_c = _rel_bucket(dist_c)
    for h in range(NSA_HEADS):
        tiles_ref[h] = jnp.where(dist >= 0, _rel_bias(bucket, table_ref, h), NEG_INF)
        wtiles_ref[h] = jnp.where(in_window, _rel_bias(bucket_w, table_ref, h), NEG_INF)
        cmpb_ref[h] = _rel_bias(bucket_c, table_ref, h)

    @pl.when(d == 0)
    def _():
        win = bw_ref.shape[1] // NSA_GROUPS
        bucket_k = _rel_bucket(past - _shift_div(_iota((1, bk_ref.shape[1]), 1), NSA_GROUPS))
        bucket_w = _rel_bucket(win - _shift_div(_iota((1, bw_ref.shape[1]), 1), NSA_GROUPS))
        bucket_b = _rel_bucket(past - (_iota((1, LANES), 1) * NSA_BLOCK + NSA_BLOCK - 1))
        for h in range(NSA_HEADS):
            bk_ref[h:h + 1, :] = _rel_bias(bucket_k, table_ref, h)
            bw_ref[h:h + 1, :] = _rel_bias(bucket_w, table_ref, h)
            bc_ref[h:h + 1, :] = _rel_bias(bucket_b, table_ref, h)


def _nsa_bias_tables(rel_table, s, tq, past, win):
    nq = s // tq
    nb = LANES
    n_win = min(_window_tiles(tq), nq) + 1
    assert s // NSA_BLOCK <= LANES
    return pl.pallas_call(
        functools.partial(_nsa_bias_kernel, tq=tq, past=past, n_win=n_win),
        grid=(nq + 1,),
        in_specs=[pl.BlockSpec(memory_space=pltpu.SMEM)],
        out_specs=[pl.BlockSpec((None, NSA_HEADS, tq, tq), lambda d: (d, 0, 0, 0)),
                   pl.BlockSpec((None, NSA_HEADS, tq, tq), lambda d: (jnp.minimum(d, n_win - 1), 0, 0, 0)),
                   pl.BlockSpec((None, NSA_HEADS, tq, nb), lambda d: (jnp.minimum(d, nq - 1), 0, 0, 0)),
                   pl.BlockSpec((NSA_HEADS, NSA_GROUPS * past), lambda d: (0, 0)),
                   pl.BlockSpec((NSA_HEADS, NSA_GROUPS * win), lambda d: (0, 0)),
                   pl.BlockSpec((NSA_HEADS, LANES), lambda d: (0, 0))],
        out_shape=[jax.ShapeDtypeStruct((nq + 1, NSA_HEADS, tq, tq), F32),
                   jax.ShapeDtypeStruct((n_win, NSA_HEADS, tq, tq), F32),
                   jax.ShapeDtypeStruct((nq, NSA_HEADS, tq, nb), F32),
                   jax.ShapeDtypeStruct((NSA_HEADS, NSA_GROUPS * past), F32),
                   jax.ShapeDtypeStruct((NSA_HEADS, NSA_GROUPS * win), F32),
                   jax.ShapeDtypeStruct((NSA_HEADS, LANES), F32)],
        compiler_params=_params(1), name="nsa_bias")(rel_table)


def _window_tiles(tq):
    return (NSA_WINDOW + tq - 1) // tq


def _nsa_prompt_kernel(q_ref, kc_ref, vc_ref, ks_ref, vs_ref, kw_ref, vw_ref, gate_ref, tiles_ref,
                       wtiles_ref, cmpb_ref, o_ref, *, tq):
    qi = pl.program_id(2)
    s_len = kc_ref.shape[0]
    n_blk = s_len // NSA_BLOCK
    nb = LANES
    rows = NSA_HPG * tq
    q = jnp.concatenate([q_ref[:, h * HEAD_DIM:(h + 1) * HEAD_DIM] for h in range(NSA_HPG)],
                        axis=0).astype(BF16)

    def block_means(ref):
        m = jnp.mean(ref[...].reshape(n_blk, NSA_BLOCK, HEAD_DIM), axis=1)
        return jnp.concatenate([m, jnp.zeros((nb - n_blk, HEAD_DIM), F32)], axis=0).astype(BF16)

    qpos_r = qi * tq + (_iota((rows, nb), 0) & (tq - 1))
    blk_r = _iota((rows, nb), 1)
    bias_c = jnp.concatenate([cmpb_ref[h] for h in range(NSA_HPG)], axis=0)
    s_c = _dot_nt(q, block_means(kc_ref)) * ATTN_SCALE + bias_c
    e_c, den_c = _masked_softmax_parts(
        s_c, (blk_r < n_blk) & (qpos_r >= blk_r * NSA_BLOCK + NSA_BLOCK - 1))
    p_c = e_c / den_c
    o_c = _dot(p_c.astype(BF16), block_means(vc_ref))

    p_sum = p_c[0:tq]
    for h in range(1, NSA_HPG):
        p_sum = p_sum + p_c[h * tq:(h + 1) * tq]
    blk = _iota((tq, nb), 1)
    cur = _shift_div(qi * tq + _iota((tq, nb), 0), NSA_BLOCK)
    forced = (blk == 0) | (blk == cur) | (blk == cur - 1)
    score = jnp.where(forced, POS_INF, jnp.where(blk <= cur, p_sum, NEG_INF))
    sel = _top_blocks(score, min(NSA_TOPK, n_blk)).astype(BF16)

    init = (jnp.full((1, rows), NEG_INF, F32), jnp.zeros((1, rows), F32), jnp.zeros((HEAD_DIM, rows), F32))
    lane_tile = lambda ref, idx: jnp.concatenate([ref[idx, h] for h in range(NSA_HPG)], axis=1)

    def sel_tile(kt, carry):
        threshold = jnp.where(kt <= qi, 0.5, 2.0)
        start = pl.multiple_of(jnp.minimum(kt, qi) * tq, tq)
        k = ks_ref[pl.ds(start, tq), :].astype(BF16)
        v = vs_ref[pl.ds(start, tq), :].astype(BF16)
        s = _dot_nt(k, q) * ATTN_SCALE + lane_tile(tiles_ref, jnp.maximum(qi - kt, 0))
        key_blk = _shift_div(start + _iota((tq, nb), 0), NSA_BLOCK)
        expand = jnp.where(_iota((tq, nb), 1) == key_blk, 1.0, 0.0).astype(BF16)
        picked = _dot_nt(expand, sel)
        picked = jnp.concatenate([picked] * NSA_HPG, axis=1)
        return _online_update_t(carry, jnp.where(picked > threshold, s, NEG_INF), v)

    def sel_body(j, carry):
        return sel_tile(2 * j + 1, sel_tile(2 * j, carry))

    _, l_s, acc_s = lax.fori_loop(0, (qi + 2) // 2, sel_body, init)
    o_s = acc_s / jnp.maximum(l_s, 1e-30)

    carry = init
    for delta in range(min(_window_tiles(tq), s_len // tq - 1), -1, -1):
        kt = qi - delta
        start = pl.multiple_of(jnp.maximum(kt, 0) * tq, tq)
        k = kw_ref[pl.ds(start, tq), :].astype(BF16)
        v = vw_ref[pl.ds(start, tq), :].astype(BF16)
        s = _dot_nt(k, q) * ATTN_SCALE + lane_tile(wtiles_ref, delta)
        carry = _online_update_t(carry, s if delta == 0 else jnp.where(kt >= 0, s, NEG_INF), v)
    _, l_w, acc_w = carry
    o_w = acc_w / jnp.maximum(l_w, 1e-30)

    gate = gate_ref[...]
    for h in range(NSA_HPG):
        r = slice(h * tq, (h + 1) * tq)
        o_ref[:, h * HEAD_DIM:(h + 1) * HEAD_DIM] = (
            gate[:, 3 * h:3 * h + 1] * o_c[r] + gate[:, 3 * h + 1:3 * h + 2] * o_s[:, r].T
            + gate[:, 3 * h + 2:3 * h + 3] * o_w[:, r].T)


def _nsa_prompt(q, kv, gates, tiles, wtiles, cmpb, n, s, tq):
    nq = s // tq
    grp_w = NSA_HPG * HEAD_DIM
    kv_spec = pl.BlockSpec((s, HEAD_DIM), lambda b, g, qi: (b, g))
    tile_spec = lambda a: pl.BlockSpec((a.shape[0], NSA_HPG, tq, tq), lambda b, g, qi: (0, g, 0, 0))
    return pl.pallas_call(
        functools.partial(_nsa_prompt_kernel, tq=tq), grid=(n, NSA_GROUPS, nq),
        in_specs=[pl.BlockSpec((tq, grp_w), lambda b, g, qi: (b * nq + qi, g))]
        + [kv_spec] * 6
        + [pl.BlockSpec((None, tq, 3 * NSA_HPG), lambda b, g, qi: (g, b * nq + qi, 0)),
           tile_spec(tiles), tile_spec(wtiles),
           pl.BlockSpec((None, NSA_HPG, tq, LANES), lambda b, g, qi: (qi, g, 0, 0))],
        out_specs=pl.BlockSpec((tq, grp_w), lambda b, g, qi: (b * nq + qi, g)),
        out_shape=jax.ShapeDtypeStruct((n * s, NSA_W), F32),
        compiler_params=_params(3), name="nsa_prompt")(q, *kv, gates, tiles, wtiles, cmpb)


def _softmax_step(scr, s_parts, v_refs):
    m_scr, l_scr, acc_scr = scr
    s = jnp.concatenate(s_parts, axis=1)
    m = m_scr[...]
    m_new = jnp.maximum(m, jnp.max(s, axis=-1, keepdims=True))
    m_safe = jnp.where(m_new == NEG_INF, 0.0, m_new)
    p = jnp.exp(s - m_safe)
    alpha = jnp.exp(m - m_safe)
    l_scr[...] = alpha * l_scr[...] + jnp.sum(p, axis=-1, keepdims=True)
    acc = alpha * acc_scr[...]
    width = s_parts[0].shape[1]
    for i, v_ref in enumerate(v_refs):
        acc = acc + _dot(p[:, i * width:(i + 1) * width].astype(BF16), v_ref[...].astype(BF16))
    acc_scr[...] = acc
    m_scr[...] = m_new


def _fox_sample_kernel(pt_ref, q_ref, kn_ref, vn_ref, lfn_ref, later_ref, *rest, pp):
    k_refs, v_refs, lf_refs = rest[:pp], rest[pp:2 * pp], rest[2 * pp:3 * pp]
    o_ref, m_scr, l_scr, acc_scr, suf_scr = rest[3 * pp:]
    step = pl.program_id(1)
    q = q_ref[...]

    @pl.when(step == 0)
    def _():
        m_scr[...] = jnp.sum(q * kn_ref[...], axis=-1, keepdims=True) * ATTN_SCALE
        l_scr[...] = jnp.ones(l_scr.shape, F32)
        acc_scr[...] = vn_ref[...].astype(BF16).astype(F32)
        suf_scr[...] = lfn_ref[...]

    cols = k_refs[0].shape[0]
    own = (_iota((FOX_HEADS, cols), 1) & (FOX_HEADS - 1)) == _iota((FOX_HEADS, cols), 0)
    q16 = q.astype(BF16)
    later = later_ref[...]
    lf_all = jnp.concatenate([lf_ref[...] for lf_ref in lf_refs], axis=0)
    lf_hi = lf_all.astype(BF16)
    lf_lo = (lf_all - lf_hi.astype(F32)).astype(BF16)
    within = _dot(lf_hi, later) + _dot(lf_lo, later)
    suf = suf_scr[...]
    s_parts = []
    for i, k_ref in enumerate(k_refs):
        rows = slice(i * FOX_HEADS, (i + 1) * FOX_HEADS)
        s = _dot_nt(q16, k_ref[...].astype(BF16)) * ATTN_SCALE + (within[rows] + suf)
        s_parts.append(jnp.where(own, s, NEG_INF))
        suf = suf + jnp.sum(lf_all[rows], axis=-1, keepdims=True)
    suf_scr[...] = suf
    _softmax_step((m_scr, l_scr, acc_scr), s_parts, v_refs)

    @pl.when(step == pl.num_programs(1) - 1)
    def _():
        o_ref[...] = acc_scr[...] / jnp.maximum(l_scr[...], 1e-30)


FOX_PAGES_PER_STEP = 8
NSA_PAGES_PER_STEP = 16


def _fox_sample(page_table, layer, q, k_new, v_new, logf_new, cache_k, cache_v, cache_logf_t):
    nb, n_pages = page_table.shape
    pp = FOX_PAGES_PER_STEP
    assert n_pages % pp == 0
    rows = PAGE_SIZE * FOX_HEADS
    later = np.repeat(np.tril(np.ones((PAGE_SIZE, PAGE_SIZE), np.float32), -1), FOX_HEADS, axis=1)
    per_b = lambda shape: pl.BlockSpec((None,) + shape, lambda b, p, pt: (b, 0, 0))
    page = lambda i: (lambda b, p, pt: (layer, pt[b, n_pages - 1 - (p * pp + i)], 0, 0))
    grid_spec = pltpu.PrefetchScalarGridSpec(
        num_scalar_prefetch=1, grid=(nb, n_pages // pp),
        in_specs=[per_b((FOX_HEADS, HEAD_DIM)), per_b((FOX_HEADS, HEAD_DIM)), per_b((FOX_HEADS, HEAD_DIM)),
                  per_b((FOX_HEADS, 1)),
                  pl.BlockSpec((PAGE_SIZE, rows), lambda b, p, pt: (0, 0))]
        + [pl.BlockSpec((None, None, rows, HEAD_DIM), page(i)) for i in range(pp)]
        + [pl.BlockSpec((None, None, rows, HEAD_DIM), page(i)) for i in range(pp)]
        + [pl.BlockSpec((None, None, FOX_HEADS, PAGE_SIZE), page(i)) for i in range(pp)],
        out_specs=per_b((FOX_HEADS, HEAD_DIM)),
        scratch_shapes=[pltpu.VMEM((FOX_HEADS, 1), F32), pltpu.VMEM((FOX_HEADS, 1), F32),
                        pltpu.VMEM((FOX_HEADS, HEAD_DIM), F32), pltpu.VMEM((FOX_HEADS, 1), F32)])
    return pl.pallas_call(
        functools.partial(_fox_sample_kernel, pp=pp), grid_spec=grid_spec,
        out_shape=jax.ShapeDtypeStruct((nb, FOX_HEADS, HEAD_DIM), F32),
        compiler_params=_params(2), name="fox_sample")(
            page_table, q, k_new, v_new, logf_new, jnp.asarray(later, BF16),
            *([cache_k] * pp), *([cache_v] * pp), *([cache_logf_t] * pp))


def _group_block_sums(x):
    sub = 8
    y = jnp.sum(x.reshape(x.shape[0] // sub, sub, x.shape[1]), axis=0)
    shift = NSA_GROUPS
    while shift < sub:
        y = y + pltpu.roll(y, shift, 0)
        shift *= 2
    return y


def _nsa_sample_cmp_kernel(pt_ref, q_ref, bc_ref, *rest, past, pp):
    k_refs, v_refs = rest[:pp], rest[pp:2 * pp]
    oc_ref, sel_ref, kcb_scr, vcb_scr = rest[2 * pp:]
    step = pl.program_id(1)
    n_blk = past // NSA_BLOCK
    per_page = PAGE_SIZE // NSA_BLOCK
    blk_rows = NSA_BLOCK * NSA_GROUPS

    @pl.when(step == 0)
    def _():
        kcb_scr[...] = jnp.zeros(kcb_scr.shape, F32)
        vcb_scr[...] = jnp.zeros(vcb_scr.shape, F32)

    row_id = _iota((LANES, HEAD_DIM), 0)
    for refs, scr in ((k_refs, kcb_scr), (v_refs, vcb_scr)):
        tabs = [scr[g] for g in range(NSA_GROUPS)]
        for i, ref in enumerate(refs):
            for bl in range(per_page):
                sums = _group_block_sums(ref[bl * blk_rows:(bl + 1) * blk_rows, :]) * (1.0 / NSA_BLOCK)
                blk_id = (step * pp + i) * per_page + bl
                for g in range(NSA_GROUPS):
                    tabs[g] = jnp.where(row_id == blk_id, sums[g:g + 1], tabs[g])
        for g in range(NSA_GROUPS):
            scr[g] = tabs[g]

    @pl.when(step == pl.num_programs(1) - 1)
    def _():
        q16 = q_ref[...].astype(BF16)
        blk = _iota((NSA_HEADS, LANES), 1)
        head_grp = _shift_div(_iota((NSA_HEADS, LANES), 0), NSA_HPG)
        by_group = lambda parts: functools.reduce(
            lambda acc, gp: jnp.where(head_grp == gp[0], gp[1], acc), enumerate(parts), jnp.zeros_like(parts[0]))
        s_c = by_group([_dot_nt(q16, kcb_scr[g].astype(BF16)) for g in range(NSA_GROUPS)])
        s_c = s_c * ATTN_SCALE + bc_ref[...]
        e_c, den_c = _masked_softmax_parts(s_c, (blk < n_blk) & (past >= blk * NSA_BLOCK + NSA_BLOCK - 1))
        p_c = e_c / den_c
        p16 = p_c.astype(BF16)
        oc_ref[...] = by_group([_dot(p16, vcb_scr[g].astype(BF16)) for g in range(NSA_GROUPS)])
        p_sum = jnp.zeros((NSA_HEADS, LANES), F32)
        for g in range(NSA_GROUPS):
            tot = jnp.sum(p_c[g * NSA_HPG:(g + 1) * NSA_HPG], axis=0, keepdims=True)
            p_sum = jnp.where(head_grp == g, tot, p_sum)
        cur = past // NSA_BLOCK
        forced = (blk == 0) | (blk == cur) | (blk == cur - 1)
        score = jnp.where(forced, POS_INF, jnp.where(blk <= cur, p_sum, NEG_INF))
        sel_ref[...] = _top_blocks(score, min(NSA_TOPK, cur + 1))


def _nsa_sample_cmp(page_table, layer, q, cache_k, cache_v, bc, past):
    assert past % NSA_BLOCK == 0 and past // NSA_BLOCK < LANES
    nb, n_pages = page_table.shape
    pp = NSA_PAGES_PER_STEP
    assert n_pages % pp == 0
    rows = PAGE_SIZE * NSA_GROUPS
    page = lambda i: (lambda b, p, pt: (layer, pt[b, p * pp + i], 0, 0))
    grid_spec = pltpu.PrefetchScalarGridSpec(
        num_scalar_prefetch=1, grid=(nb, n_pages // pp),
        in_specs=[pl.BlockSpec((None, NSA_HEADS, HEAD_DIM), lambda b, p, pt: (b, 0, 0)),
                  pl.BlockSpec((NSA_HEADS, LANES), lambda b, p, pt: (0, 0))]
        + [pl.BlockSpec((None, None, rows, HEAD_DIM), page(i)) for i in range(pp)] * 2,
        out_specs=[pl.BlockSpec((None, NSA_HEADS, HEAD_DIM), lambda b, p, pt: (b, 0, 0)),
                   pl.BlockSpec((None, NSA_HEADS, LANES), lambda b, p, pt: (b, 0, 0))],
        scratch_shapes=[pltpu.VMEM((NSA_GROUPS, LANES, HEAD_DIM), F32),
                        pltpu.VMEM((NSA_GROUPS, LANES, HEAD_DIM), F32)])
    return pl.pallas_call(
        functools.partial(_nsa_sample_cmp_kernel, past=past, pp=pp), grid_spec=grid_spec,
        out_shape=[jax.ShapeDtypeStruct((nb, NSA_HEADS, HEAD_DIM), F32),
                   jax.ShapeDtypeStruct((nb, NSA_HEADS, LANES), F32)],
        compiler_params=_params(2), name="nsa_sample_cmp")(
            page_table, q, bc, *([cache_k] * pp), *([cache_v] * pp))


def _per_head(new_ref):
    head_grp = _shift_div(_iota((NSA_HEADS, HEAD_DIM), 0), NSA_HPG)
    out = jnp.zeros((NSA_HEADS, HEAD_DIM), F32)
    for g in range(NSA_GROUPS):
        out = jnp.where(head_grp == g, new_ref[g:g + 1, :], out)
    return out


def _nsa_sample_sel_kernel(pt_ref, q_ref, ksn_ref, vsn_ref, kwn_ref, vwn_ref, sel_ref, oc_ref, gate_ref,
                           kw_ref, vw_ref, bk_ref, bw_ref, b0_ref, *rest, pp):
    ks_refs, vs_refs = rest[:pp], rest[pp:2 * pp]
    o_ref, m_scr, l_scr, acc_scr = rest[2 * pp:]
    step = pl.program_id(1)
    per_page = PAGE_SIZE // NSA_BLOCK
    q = q_ref[...]
    q16 = q.astype(BF16)
    b0 = b0_ref[...]
    own = lambda cols: ((_iota((NSA_HEADS, cols), 1) & (NSA_GROUPS - 1))
                        == _shift_div(_iota((NSA_HEADS, cols), 0), NSA_HPG))

    @pl.when(step == 0)
    def _():
        m_scr[...] = jnp.sum(q * _per_head(ksn_ref), axis=-1, keepdims=True) * ATTN_SCALE + b0
        l_scr[...] = jnp.ones(l_scr.shape, F32)
        acc_scr[...] = _per_head(vsn_ref).astype(BF16).astype(F32)

    cols = ks_refs[0].shape[0]
    sel16 = sel_ref[...].astype(BF16)
    key_blk = _shift_div(_iota((LANES, cols), 1), NSA_GROUPS * NSA_BLOCK)
    s_parts = []
    for i, ks_ref in enumerate(ks_refs):
        pg = step * pp + i
        expand = jnp.where(_iota((LANES, cols), 0) == pg * per_page + key_blk, 1.0, 0.0)
        picked = _dot(sel16, expand.astype(BF16))
        start = pl.multiple_of(pg * cols, cols)
        s = _dot_nt(q16, ks_ref[...].astype(BF16)) * ATTN_SCALE + bk_ref[:, pl.ds(start, cols)]
        s_parts.append(jnp.where(own(cols) & (picked > 0.5), s, NEG_INF))
    _softmax_step((m_scr, l_scr, acc_scr), s_parts, vs_refs)

    @pl.when(step == pl.num_programs(1) - 1)
    def _():
        o_s = acc_scr[...] / jnp.maximum(l_scr[...], 1e-30)
        wcols = kw_ref.shape[0]
        s_w = _dot_nt(q16, kw_ref[...].astype(BF16)) * ATTN_SCALE + bw_ref[...]
        s_w = jnp.where(own(wcols), s_w, NEG_INF)
        s_n = jnp.sum(q * _per_head(kwn_ref), axis=-1, keepdims=True) * ATTN_SCALE + b0
        m_w = jnp.maximum(jnp.max(s_w, axis=-1, keepdims=True), s_n)
        e_w = jnp.exp(s_w - m_w)
        e_n = jnp.exp(s_n - m_w)
        den = jnp.maximum(jnp.sum(e_w, axis=-1, keepdims=True) + e_n, 1e-30)
        p_w = (e_w / den).astype(BF16)
        p_n = (e_n / den).astype(BF16).astype(F32)
        o_w = _dot(p_w, vw_ref[...].astype(BF16)) + p_n * _per_head(vwn_ref).astype(BF16).astype(F32)
        gate = gate_ref[...]
        o_ref[...] = gate[:, 0:1] * oc_ref[...] + gate[:, 1:2] * o_s + gate[:, 2:3] * o_w


def _nsa_sample_sel(page_table, layer, q, ks_new, vs_new, kw_new, vw_new, sel, o_c, gates,
                    cache_ks, cache_vs, win_k, win_v, bk, bw, b0):
    nb, n_pages = page_table.shape
    pp = NSA_PAGES_PER_STEP
    assert n_pages % pp == 0
    wrows = win_k.shape[2]
    rows = PAGE_SIZE * NSA_GROUPS
    per_b = lambda shape: pl.BlockSpec((None,) + shape, lambda b, p, pt: (b, 0, 0))
    page = lambda i: (lambda b, p, pt: (layer, pt[b, p * pp + i], 0, 0))
    const = lambda shape: pl.BlockSpec(shape, lambda b, p, pt: (0, 0))
    new = per_b((NSA_GROUPS, HEAD_DIM))
    grid_spec = pltpu.PrefetchScalarGridSpec(
        num_scalar_prefetch=1, grid=(nb, n_pages // pp),
        in_specs=[per_b((NSA_HEADS, HEAD_DIM)), new, new, new, new,
                  per_b((NSA_HEADS, LANES)), per_b((NSA_HEADS, HEAD_DIM)), per_b((NSA_HEADS, 3)),
                  pl.BlockSpec((None, None, wrows, HEAD_DIM), lambda b, p, pt: (layer, b, 0, 0)),
                  pl.BlockSpec((None, None, wrows, HEAD_DIM), lambda b, p, pt: (layer, b, 0, 0)),
                  const((NSA_HEADS, n_pages * rows)), const((NSA_HEADS, wrows)), const((NSA_HEADS, 1))]
        + [pl.BlockSpec((None, None, rows, HEAD_DIM), page(i)) for i in range(pp)] * 2,
        out_specs=per_b((NSA_HEADS, HEAD_DIM)),
        scratch_shapes=[pltpu.VMEM((NSA_HEADS, 1), F32), pltpu.VMEM((NSA_HEADS, 1), F32),
                        pltpu.VMEM((NSA_HEADS, HEAD_DIM), F32)])
    return pl.pallas_call(
        functools.partial(_nsa_sample_sel_kernel, pp=pp), grid_spec=grid_spec,
        out_shape=jax.ShapeDtypeStruct((nb, NSA_HEADS, HEAD_DIM), F32),
        compiler_params=_params(2), name="nsa_sample_sel")(
            page_table, q, ks_new, vs_new, kw_new, vw_new, sel, o_c, gates,
            win_k, win_v, bk, bw, b0, *([cache_ks] * pp), *([cache_vs] * pp))


def _gla_gate_kernel(g_ref, w_ref, b_ref, o_ref):
    z = _dot(g_ref[...].astype(BF16), w_ref[...]) + b_ref[...]
    o_ref[...] = _log_sigmoid(z) * (1.0 / GLA_TAU)


def _gla_gate(hc, col_g, w_g2_pad, b_g, tm):
    t = hc.shape[0]
    n = w_g2_pad.shape[1]
    return pl.pallas_call(
        _gla_gate_kernel, grid=(t // tm,),
        in_specs=[pl.BlockSpec((tm, LANES), lambda i: (i, col_g)),
                  pl.BlockSpec((LANES, n), lambda i: (0, 0)),
                  pl.BlockSpec((1, n), lambda i: (0, 0))],
        out_specs=pl.BlockSpec((tm, n), lambda i: (i, 0)),
        out_shape=jax.ShapeDtypeStruct((t, n), F32),
        compiler_params=_params(1), name="gla_gate")(hc, w_g2_pad, b_g.reshape(1, n))


def _gla_out_gate(o, r, g_norm):
    o = o * lax.rsqrt(jnp.mean(o * o, axis=-1, keepdims=True) + LN_EPS) * g_norm
    return o * (r * _sigmoid(r))


def _gla_prompt_kernel(q_ref, k_ref, v_ref, r_ref, la_ref, gn_ref, o_ref, st_ref, stt_scr):
    c = pl.program_id(1)
    ch = q_ref.shape[0]

    @pl.when(c == 0)
    def _():
        stt_scr[...] = jnp.zeros(stt_scr.shape, F32)

    la = la_ref[...]
    row = _iota((ch, ch), 0)
    col = _iota((ch, ch), 1)
    b_all = _dot(jnp.where(row >= col, 1.0, 0.0), la, HIGHEST)
    sub_start = (row >> int(math.log2(GLA_SUB))) << int(math.log2(GLA_SUB))
    base_all = _dot(jnp.where(col < sub_start, 1.0, 0.0), la, HIGHEST)
    krow = _iota((ch, 1), 0)
    for h in range(GLA_HEADS):
        dk = slice(h * GLA_DK, (h + 1) * GLA_DK)
        dv = slice(h * GLA_DV, (h + 1) * GLA_DV)
        b, base = b_all[:, dk], base_all[:, dk]
        q = q_ref[:, dk] * (GLA_DK ** -0.5)
        k = k_ref[:, dk]
        v16 = v_ref[:, dv].astype(BF16)
        stt = stt_scr[h]

        o = _dot_nt((q * jnp.exp(b)).astype(BF16), stt.astype(BF16))
        q_in = (q * jnp.exp(b - base)).astype(BF16)
        att_rows = []
        for i in range(ch // GLA_SUB):
            lo, hi = i * GLA_SUB, (i + 1) * GLA_SUB
            expo = jnp.where(krow < hi, base[lo:lo + 1] - b, NEG_INF)
            k_in = (k * jnp.exp(expo)).astype(BF16)
            att_rows.append(_dot_nt(q_in[lo:hi], k_in))
        att = jnp.where(row >= col, jnp.concatenate(att_rows, axis=0), 0.0)
        o = o + _dot(att.astype(BF16), v16)

        b_last = b[ch - 1:ch]
        k_dec = (k * jnp.exp(b_last - b)).astype(BF16)
        stt = stt * jnp.exp(b_last) + lax.dot_general(v16, k_dec, (((0,), (0,)), ((), ())),
                                                       preferred_element_type=F32)
        stt_scr[h] = stt
        o_ref[:, dv] = _gla_out_gate(o, r_ref[:, dv], gn_ref[...])

    @pl.when(c == pl.num_programs(1) - 1)
    def _():
        for h in range(GLA_HEADS):
            st_ref[h] = stt_scr[h].T


def _gla_prompt(hc, la, g_norm, n, s):
    ch = GLA_CHUNK if s % GLA_CHUNK == 0 else s
    nc = s // ch
    rows = lambda b, c: b * nc + c
    qk_w, v_w = GLA_HEADS * GLA_DK, GLA_HEADS * GLA_DV
    assert v_w == 2 * qk_w
    return pl.pallas_call(
        _gla_prompt_kernel, grid=(n, nc),
        in_specs=[pl.BlockSpec((ch, qk_w), lambda b, c: (rows(b, c), 0)),
                  pl.BlockSpec((ch, qk_w), lambda b, c: (rows(b, c), 1)),
                  pl.BlockSpec((ch, v_w), lambda b, c: (rows(b, c), 1)),
                  pl.BlockSpec((ch, v_w), lambda b, c: (rows(b, c), 2)),
                  pl.BlockSpec((ch, qk_w), lambda b, c: (rows(b, c), 0)),
                  pl.BlockSpec((1, GLA_DV), lambda b, c: (0, 0))],
        out_specs=[pl.BlockSpec((ch, v_w), lambda b, c: (rows(b, c), 0)),
                   pl.BlockSpec((None, GLA_HEADS, GLA_DK, GLA_DV), lambda b, c: (b, 0, 0, 0))],
        out_shape=[jax.ShapeDtypeStruct((n * s, v_w), F32),
                   jax.ShapeDtypeStruct((n, GLA_HEADS, GLA_DK, GLA_DV), F32)],
        scratch_shapes=[pltpu.VMEM((GLA_HEADS, GLA_DV, GLA_DK), F32)],
        compiler_params=_params(2), name="gla_prompt")(hc, hc, hc, hc, la, g_norm.reshape(1, GLA_DV))


def _gla_sample_kernel(q_ref, k_ref, la_ref, v_ref, r_ref, gn_ref, s0_ref, o_ref, st_ref):
    for h in range(GLA_HEADS):
        st = jnp.exp(la_ref[h]) * s0_ref[h] + k_ref[h] * v_ref[h]
        st_ref[h] = st
        o = jnp.sum((q_ref[h] * (GLA_DK ** -0.5)) * st, axis=0, keepdims=True)
        o_ref[h] = _gla_out_gate(o, r_ref[h], gn_ref[...])


def _gla_sample(q, k, la, v, r, g_norm, s0):
    nb = q.shape[0]
    col = pl.BlockSpec((None, GLA_HEADS, GLA_DK, 1), lambda b: (b, 0, 0, 0))
    rowv = pl.BlockSpec((None, GLA_HEADS, 1, GLA_DV), lambda b: (b, 0, 0, 0))
    full = pl.BlockSpec((None, GLA_HEADS, GLA_DK, GLA_DV), lambda b: (b, 0, 0, 0))
    return pl.pallas_call(
        _gla_sample_kernel, grid=(nb,),
        in_specs=[col, col, col, rowv, rowv, pl.BlockSpec((1, GLA_DV), lambda b: (0, 0)), full],
        out_specs=[rowv, full],
        out_shape=[jax.ShapeDtypeStruct((nb, GLA_HEADS, 1, GLA_DV), F32),
                   jax.ShapeDtypeStruct((nb, GLA_HEADS, GLA_DK, GLA_DV), F32)],
        compiler_params=_params(1), name="gla_sample")(q, k, la, v, r, g_norm.reshape(1, GLA_DV), s0)


def _router_kernel(x_ref, w_ref, b_ref, idx_ref, gate_ref):
    logits = _dot(x_ref[...], w_ref[...], HIGHEST) + b_ref[...]
    lane = _iota(logits.shape, 1).astype(F32)
    logits = jnp.where(lane < N_EXPERTS, logits, NEG_INF)
    top_v = jnp.full(logits.shape, NEG_INF, F32)
    top_i = jnp.zeros(logits.shape, F32)
    for kk in range(TOP_K):
        mx = jnp.max(logits, axis=-1, keepdims=True)
        first = jnp.min(jnp.where(logits == mx, lane, 1e9), axis=-1, keepdims=True)
        top_v = jnp.where(lane == kk, mx, top_v)
        top_i = jnp.where(lane == kk, first, top_i)
        logits = jnp.where(lane == first, NEG_INF, logits)
    e = jnp.exp(top_v - jnp.max(top_v, axis=-1, keepdims=True))
    gate_ref[...] = e / jnp.sum(e, axis=-1, keepdims=True)
    idx_ref[...] = top_i.astype(jnp.int32)


def _router(x, w_pad, b_pad, tm):
    t, d = x.shape
    return pl.pallas_call(
        _router_kernel, grid=(t // tm,),
        in_specs=[pl.BlockSpec((tm, d), lambda i: (i, 0)),
                  pl.BlockSpec((d, LANES), lambda i: (0, 0)),
                  pl.BlockSpec((1, LANES), lambda i: (0, 0))],
        out_specs=[pl.BlockSpec((tm, LANES), lambda i: (i, 0))] * 2,
        out_shape=[jax.ShapeDtypeStruct((t, LANES), jnp.int32), jax.ShapeDtypeStruct((t, LANES), F32)],
        compiler_params=_params(1), name="router")(x, w_pad, b_pad)


def _moe_gate_up_kernel(te_ref, tf_ref, tv_ref, x_ref, wg_ref, wu_ref, bg_ref, bu_ref, o_ref, wg16, wu16):
    i = pl.program_id(1)

    @pl.when(tf_ref[i] == 1)
    def _():
        wg16[...] = wg_ref[...].astype(BF16)
        wu16[...] = wu_ref[...].astype(BF16)

    @pl.when(tv_ref[i] == 1)
    def _():
        x = x_ref[...].astype(BF16)
        for lo in range(0, o_ref.shape[1], MOE_ACT_CHUNK):
            cs = slice(lo, lo + MOE_ACT_CHUNK)
            g = jnp.minimum(_dot(x, wg16[:, cs]) + bg_ref[:, cs], SWIGLU_LIMIT)
            u = jnp.clip(_dot(x, wu16[:, cs]) + bu_ref[:, cs], -SWIGLU_LIMIT, SWIGLU_LIMIT)
            o_ref[:, cs] = ((u + 1.0) * g * _sigmoid(SWIGLU_ALPHA * g)).astype(BF16)

    @pl.when(tv_ref[i] == 0)
    def _():
        o_ref[...] = jnp.zeros(o_ref.shape, BF16)


def _moe_gate_up(te, tf, tv, xs, w_gu, b_gu, layer, tmm, tn):
    p_pad, d = xs.shape
    de = w_gu.shape[-1] // 2
    nj = de // tn
    grid_spec = pltpu.PrefetchScalarGridSpec(
        num_scalar_prefetch=3, grid=(nj, p_pad // tmm),
        in_specs=[pl.BlockSpec((tmm, d), lambda j, i, te, tf, tv: (i, 0)),
                  pl.BlockSpec((None, None, d, tn), lambda j, i, te, tf, tv: (layer, te[i], 0, j)),
                  pl.BlockSpec((None, None, d, tn), lambda j, i, te, tf, tv: (layer, te[i], 0, nj + j)),
                  pl.BlockSpec((None, None, 1, tn), lambda j, i, te, tf, tv: (layer, te[i], 0, j)),
                  pl.BlockSpec((None, None, 1, tn), lambda j, i, te, tf, tv: (layer, te[i], 0, nj + j))],
        out_specs=pl.BlockSpec((tmm, tn), lambda j, i, te, tf, tv: (i, j)),
        scratch_shapes=[pltpu.VMEM((d, tn), BF16), pltpu.VMEM((d, tn), BF16)])
    b4 = b_gu.reshape(b_gu.shape[0], b_gu.shape[1], 1, b_gu.shape[2])
    return pl.pallas_call(
        _moe_gate_up_kernel, grid_spec=grid_spec,
        out_shape=jax.ShapeDtypeStruct((p_pad, de), BF16),
        compiler_params=_params(2), name="moe_gate_up")(te, tf, tv, xs, w_gu, w_gu, b4, b4)


def _moe_down_kernel(te_ref, tf_ref, tv_ref, a_ref, w_ref, b_ref, o_ref, w16):
    i = pl.program_id(1)

    @pl.when(tf_ref[i] == 1)
    def _():
        w16[...] = w_ref[...].astype(BF16)

    @pl.when(tv_ref[i] == 1)
    def _():
        o_ref[...] = _dot(a_ref[...], w16[...]) + b_ref[...]

    @pl.when(tv_ref[i] == 0)
    def _():
        o_ref[...] = jnp.zeros(o_ref.shape, F32)


def _moe_down(te, tf, tv, a, w_d, b_d, layer, tmm, tn):
    p_pad, de = a.shape
    d = w_d.shape[-1]
    grid_spec = pltpu.PrefetchScalarGridSpec(
        num_scalar_prefetch=3, grid=(d // tn, p_pad // tmm),
        in_specs=[pl.BlockSpec((tmm, de), lambda j, i, te, tf, tv: (i, 0)),
                  pl.BlockSpec((None, None, de, tn), lambda j, i, te, tf, tv: (layer, te[i], 0, j)),
                  pl.BlockSpec((None, None, 1, tn), lambda j, i, te, tf, tv: (layer, te[i], 0, j))],
        out_specs=pl.BlockSpec((tmm, tn), lambda j, i, te, tf, tv: (i, j)),
        scratch_shapes=[pltpu.VMEM((de, tn), BF16)])
    b4 = b_d.reshape(b_d.shape[0], b_d.shape[1], 1, b_d.shape[2])
    return pl.pallas_call(
        _moe_down_kernel, grid_spec=grid_spec,
        out_shape=jax.ShapeDtypeStruct((p_pad, d), F32),
        compiler_params=_params(2), name="moe_down")(te, tf, tv, a, w_d, b4)


def _moe_combine_ln_kernel(*refs):
    y_refs = refs[:TOP_K]
    gate_ref, x_ref, g_ref, b_ref, o_ref, o16_ref = refs[TOP_K:]
    gate = gate_ref[...]
    out = gate[:, 0:1] * y_refs[0][...]
    for kk in range(1, TOP_K):
        out = out + gate[:, kk:kk + 1] * y_refs[kk][...]
    res = _layernorm_rows(DEEPNORM_ALPHA * x_ref[...] + out, g_ref[...], b_ref[...])
    o_ref[...] = res
    o16_ref[...] = res.astype(BF16)


def _moe_combine_ln(yk, gate, x, g, b, tm):
    t, d = x.shape
    nt = t // tm
    slot = lambda kk: pl.BlockSpec((tm, d), lambda i: (kk * nt + i, 0))
    return pl.pallas_call(
        _moe_combine_ln_kernel, grid=(nt,),
        in_specs=[slot(kk) for kk in range(TOP_K)]
        + [pl.BlockSpec((tm, LANES), lambda i: (i, 0)),
           pl.BlockSpec((tm, d), lambda i: (i, 0)),
           pl.BlockSpec((1, d), lambda i: (0, 0)), pl.BlockSpec((1, d), lambda i: (0, 0))],
        out_specs=[pl.BlockSpec((tm, d), lambda i: (i, 0))] * 2,
        out_shape=[jax.ShapeDtypeStruct((t, d), F32), jax.ShapeDtypeStruct((t, d), BF16)],
        compiler_params=_params(1), name="moe_combine_ln")(
            *([yk] * TOP_K), gate, x, g.reshape(1, d), b.reshape(1, d))


MOE_TILE_ROWS = 256
MOE_TILE_COLS = 1024
MOE_DOWN_TILE_COLS = 2048
MOE_ACT_CHUNK = 256
ROW_TILE = 128


def _moe(x, x16, layer, router_w, router_b, w_gu, b_gu, w_d, b_d, ln_g, ln_b):
    t, d = x.shape
    w_pad = jnp.pad(router_w[layer], ((0, 0), (0, LANES - N_EXPERTS)))
    b_pad = jnp.pad(router_b[layer], (0, LANES - N_EXPERTS)).reshape(1, LANES)
    top_i, gate = _router(x, w_pad, b_pad, ROW_TILE)

    n_pairs = t * TOP_K
    n_tiles = n_pairs // MOE_TILE_ROWS + N_EXPERTS
    e_flat = top_i[:, :TOP_K].reshape(n_pairs)
    onehot = (e_flat[:, None] == jnp.arange(N_EXPERTS, dtype=jnp.int32)[None, :]).astype(jnp.int32)
    csum = jnp.cumsum(onehot, axis=0)
    counts = csum[-1]
    rank = jnp.sum(onehot * csum, axis=1) - 1
    tiles_per = (counts + MOE_TILE_ROWS - 1) // MOE_TILE_ROWS
    tile_end = jnp.cumsum(tiles_per)
    dest = ((tile_end - tiles_per) * MOE_TILE_ROWS)[e_flat] + rank
    row_token = jnp.zeros((n_tiles * MOE_TILE_ROWS,), jnp.int32).at[dest].set(
        jnp.arange(n_pairs, dtype=jnp.int32) // TOP_K)
    tile_id = jnp.arange(n_tiles, dtype=jnp.int32)
    tv = (tile_id < tile_end[-1]).astype(jnp.int32)
    last_tile = jnp.minimum(tile_id, tile_end[-1] - 1)
    te = jnp.sum((tile_end[None, :] <= last_tile[:, None]).astype(jnp.int32), axis=1)
    tf = jnp.concatenate([jnp.ones((1,), jnp.int32), (te[1:] != te[:-1]).astype(jnp.int32)])

    xs = x[row_token]
    act = _moe_gate_up(te, tf, tv, xs, w_gu, b_gu, layer, MOE_TILE_ROWS, MOE_TILE_COLS)
    ys = _moe_down(te, tf, tv, act, w_d, b_d, layer, MOE_TILE_ROWS, MOE_DOWN_TILE_COLS)
    yk = ys[dest.reshape(t, TOP_K).T.reshape(n_pairs)]
    return _moe_combine_ln(yk, gate, x, ln_g, ln_b, ROW_TILE)


PROJ_ROW_TILE = 640
PROJ_A_ROW_TILE = 512
PROJ_A_COL_TILE = 512
PROJ_C_COL_TILE = 896
OUT_ROW_TILE = 320
FOX_Q_TILE = 256
NSA_Q_TILE = 128


def _attention_layer(x, x16, i, n, s, nb, caches, page_table, w_in, b_f, w_out, rel_table, ln_g, ln_b):
    (cache_fox_k, cache_fox_v, cache_fox_logf, cache_cmp_k, cache_cmp_v, cache_sel_k, cache_sel_v,
     cache_win_k, cache_win_v) = caches
    tp = n * s
    past = page_table.shape[1] * PAGE_SIZE
    win = cache_win_k.shape[2]
    o_f = 3 * FOX_W
    o_n = o_f + FOX_HEADS
    o_g = o_n + NSA_W + 6 * NSA_KV_W
    w_main = jnp.concatenate([w_in[:, :o_f], w_in[:, o_n:o_g]], axis=1).astype(BF16)
    n_small = FOX_HEADS + 3 * NSA_HEADS
    w_small = jnp.pad(jnp.concatenate([w_in[:, o_f:o_n], w_in[:, o_g:]], axis=1), ((0, 0), (0, LANES - n_small)))
    b_small = jnp.pad(b_f, (0, LANES - FOX_HEADS))

    groups = [[FOX_W]] * 3 + [[NSA_W]] + [[NSA_KV_W, NSA_KV_W]] * 3
    qf_p, kf_p, vf_p, qn_p, *kv_p = _mm_split(x16, tp, w_main, groups, PROJ_A_ROW_TILE, PROJ_A_COL_TILE,
                                              "proj_a")
    hs = _mm(x16[tp:], w_main, nb, PROJ_A_COL_TILE, "proj_a_sample")
    logf_all, logf_t, sg_all = _gates_a(x, w_small, b_small, ROW_TILE)
    logf = logf_all[:, :FOX_HEADS]
    gates = sg_all[:, FOX_HEADS:n_small]

    c_row = _cumsum_seq(logf_t, n, s)
    o_fox_p = _fox_prompt(qf_p, kf_p, vf_p, c_row.reshape(n, FOX_HEADS, s, 1),
                          c_row.reshape(n, FOX_HEADS, 1, s), n, s, FOX_Q_TILE)
    tiles, wtiles, cmpb, bk, bw, bc = _nsa_bias_tables(rel_table, s, NSA_Q_TILE, past, win)
    gates_g = gates.reshape(-1, NSA_GROUPS, 3 * NSA_HPG).transpose(1, 0, 2)
    col_nsa = 3 * FOX_W
    o_nsa_p = _nsa_prompt(qn_p, kv_p, gates_g, tiles, wtiles, cmpb, n, s, NSA_Q_TILE)

    seg = lambda lo, w: hs[:, lo:lo + w]
    q_fox, k_fox, v_fox = seg(0, FOX_W), seg(FOX_W, FOX_W), seg(2 * FOX_W, FOX_W)
    q_nsa = seg(col_nsa, NSA_W).reshape(nb, NSA_HEADS, HEAD_DIM)
    kv = [seg(col_nsa + NSA_W + j * NSA_KV_W, NSA_KV_W).reshape(nb, NSA_GROUPS, HEAD_DIM) for j in range(6)]
    rows_view = lambda c: c.reshape(c.shape[0], c.shape[1], c.shape[2] * c.shape[3], HEAD_DIM)
    heads = lambda a: a.reshape(nb, FOX_HEADS, HEAD_DIM)
    o_fox_s = _fox_sample(page_table, i, heads(q_fox), heads(k_fox), heads(v_fox),
                          logf[tp:].reshape(nb, FOX_HEADS, 1),
                          rows_view(cache_fox_k), rows_view(cache_fox_v),
                          jnp.swapaxes(cache_fox_logf, 2, 3))
    o_c, sel = _nsa_sample_cmp(page_table, i, q_nsa, rows_view(cache_cmp_k), rows_view(cache_cmp_v), bc, past)
    o_nsa_s = _nsa_sample_sel(
        page_table, i, q_nsa, kv[2], kv[3], kv[4], kv[5], sel, o_c,
        gates[tp:].reshape(nb, NSA_HEADS, 3), rows_view(cache_sel_k), rows_view(cache_sel_v),
        rows_view(cache_win_k), rows_view(cache_win_v), bk, bw, rel_table[0].reshape(NSA_HEADS, 1))

    o_fox = jnp.concatenate([o_fox_p, o_fox_s.reshape(nb, FOX_W)], axis=0)
    o_nsa = jnp.concatenate([o_nsa_p, o_nsa_s.reshape(nb, NSA_W)], axis=0)
    w_out16 = w_out.astype(BF16)
    x_new, x_new16 = _mm_res_ln([o_fox, o_nsa], [w_out16[:FOX_W], w_out16[FOX_W:]], x, ln_g, ln_b,
                                OUT_ROW_TILE, "attn_out_ln")

    wb = min(NSA_WINDOW, s)

    def state(p, lo, heads, keep=s):
        p = p.reshape(n, s, heads, HEAD_DIM)[:, s - keep:]
        return p, hs[:, lo:lo + heads * HEAD_DIM].reshape(nb, 1, heads, HEAD_DIM)

    kv_lo = col_nsa + NSA_W
    states = [state(kf_p, FOX_W, FOX_HEADS), state(vf_p, 2 * FOX_W, FOX_HEADS),
              (logf[:tp].reshape(n, s, FOX_HEADS), logf[tp:].reshape(nb, 1, FOX_HEADS))]
    states += [state(kv_p[j], kv_lo + j * NSA_KV_W, NSA_GROUPS) for j in range(4)]
    states += [state(kv_p[j], kv_lo + j * NSA_KV_W, NSA_GROUPS, wb) for j in (4, 5)]
    return x_new, x_new16, states


def _gla_layer(x, x16, i, n, s, nb, state_gla, w_in, w_g2, b_g, g_norm, w_out, ln_g, ln_b):
    tp = n * s
    qk_w = GLA_HEADS * GLA_DK
    v_w = GLA_HEADS * GLA_DV
    o_g = 2 * qk_w + v_w
    o_r = o_g + GLA_GATE_RANK
    w_main = jnp.concatenate([w_in[:, :o_g], w_in[:, o_r:], w_in[:, o_g:o_r],
                              jnp.zeros((w_in.shape[0], LANES - GLA_GATE_RANK), w_in.dtype)], axis=1).astype(BF16)
    hc = _mm(x16, w_main, PROJ_ROW_TILE, PROJ_C_COL_TILE, "proj_c")
    w_g2_pad = jnp.pad(w_g2, ((0, LANES - GLA_GATE_RANK), (0, 0))).astype(BF16)
    la = _gla_gate(hc, (o_g + v_w) // LANES, w_g2_pad, b_g, PROJ_ROW_TILE)

    og_p, st_p = _gla_prompt(hc, la, g_norm, n, s)
    hs = hc[tp:]
    col = lambda a: a.reshape(nb, GLA_HEADS, GLA_DK, 1)
    rowv = lambda a: a.reshape(nb, GLA_HEADS, 1, GLA_DV)
    og_s, st_s = _gla_sample(col(hs[:, :qk_w]), col(hs[:, qk_w:2 * qk_w]), col(la[tp:]),
                             rowv(hs[:, 2 * qk_w:o_g]), rowv(hs[:, o_g:o_g + v_w]), g_norm, state_gla[i])
    og = jnp.concatenate([og_p, og_s.reshape(nb, v_w)], axis=0)
    x_new, x_new16 = _mm_res_ln([og], [w_out.astype(BF16)], x, ln_g, ln_b, OUT_ROW_TILE, "gla_out_ln")
    return x_new, x_new16, st_p, st_s


def kernel(x_prompt, x_sample, cache_fox_k, cache_fox_v, cache_fox_logf, cache_nsa_cmp_k, cache_nsa_cmp_v,
           cache_nsa_sel_k, cache_nsa_sel_v, cache_nsa_win_k, cache_nsa_win_v, state_gla, page_table,
           w_in_a, b_forget, w_out_a, rel_bias_table, w_in_c, w_gla_gate2, b_gla_gate, gla_norm_g, w_out_c,
           ln_g, ln_b, router_w, router_b, w_gate_up, b_gate_up, w_down, b_down):
    n, s, d = x_prompt.shape
    nb = x_sample.shape[0]
    assert x_sample.shape[1] == 1
    tp = n * s
    x = jnp.concatenate([x_prompt.reshape(tp, d), x_sample.reshape(nb, d)], axis=0)
    x16 = x.astype(BF16)
    caches = (cache_fox_k, cache_fox_v, cache_fox_logf, cache_nsa_cmp_k, cache_nsa_cmp_v,
              cache_nsa_sel_k, cache_nsa_sel_v, cache_nsa_win_k, cache_nsa_win_v)
    attn_states, gla_p, gla_s = [], [], []
    for layer in range(DEPTH):
        i = layer // 2
        if layer % 2 == 0:
            x, x16, st = _attention_layer(x, x16, i, n, s, nb, caches, page_table, w_in_a[i], b_forget[i],
                                          w_out_a[i], rel_bias_table, ln_g[layer, 0], ln_b[layer, 0])
            attn_states.append(st)
        else:
            x, x16, st_p, st_s = _gla_layer(x, x16, i, n, s, nb, state_gla, w_in_c[i], w_gla_gate2[i],
                                            b_gla_gate[i], gla_norm_g[i], w_out_c[i],
                                            ln_g[layer, 0], ln_b[layer, 0])
            gla_p.append(st_p)
            gla_s.append(st_s)
        x, x16 = _moe(x, x16, layer, router_w, router_b, w_gate_up, b_gate_up, w_down, b_down,
                      ln_g[layer, 1], ln_b[layer, 1])
    outs = [x[:tp].reshape(n, s, d), x[tp:].reshape(nb, 1, d)]
    for j in range(9):
        outs.append(jnp.stack([st[j][0] for st in attn_states]))
        outs.append(jnp.stack([st[j][1] for st in attn_states]))
    outs.append(jnp.stack(gla_p))
    outs.append(jnp.stack(gla_s))
    return tuple(outs)
```

```python
import functools
import math

import jax
import jax.numpy as jnp
import numpy as np
from jax import lax
from jax.experimental import pallas as pl
from jax.experimental.pallas import tpu as pltpu

F32 = jnp.float32
BF16 = jnp.bfloat16
HIGHEST = lax.Precision.HIGHEST
NEG_INF = float("-inf")
POS_INF = float("inf")

D_MODEL = 2048
PAGE_SIZE = 128
HEAD_DIM = 128
FOX_HEADS = 8
NSA_HEADS = 8
NSA_GROUPS = 2
NSA_HPG = NSA_HEADS // NSA_GROUPS
NSA_BLOCK = 64
NSA_TOPK = 8
NSA_WINDOW = 512
REL_BUCKETS = 32
REL_MAX_DIST = 1024
GLA_HEADS = 4
GLA_DK = D_MODEL // 2 // GLA_HEADS
GLA_DV = D_MODEL // GLA_HEADS
GLA_GATE_RANK = 16
GLA_TAU = 16.0
GLA_CHUNK = 64
GLA_SUB = 16
N_EXPERTS = 32
TOP_K = 4
SWIGLU_LIMIT = 7.0
SWIGLU_ALPHA = 1.702
LN_EPS = 1e-5
DEPTH = 2
DEEPNORM_ALPHA = (2 * DEPTH) ** 0.25
FOX_W = FOX_HEADS * HEAD_DIM
NSA_W = NSA_HEADS * HEAD_DIM
NSA_KV_W = NSA_GROUPS * HEAD_DIM
ATTN_SCALE = HEAD_DIM ** -0.5

VMEM_LIMIT_BYTES = 56 * 1024 * 1024
LANES = 128

_REL_EXACT = REL_BUCKETS // 2
_REL_THRESHOLDS = tuple(
    (_REL_EXACT * 8) if 2 * j == (REL_BUCKETS - _REL_EXACT) else
    math.ceil(_REL_EXACT * (REL_MAX_DIST / _REL_EXACT) ** (j / (REL_BUCKETS - _REL_EXACT)))
    for j in range(1, REL_BUCKETS - _REL_EXACT))


def _params(n_axes):
    return pltpu.CompilerParams(dimension_semantics=("arbitrary",) * n_axes,
                                vmem_limit_bytes=VMEM_LIMIT_BYTES)


def _iota(shape, dim, dtype=jnp.int32):
    return lax.broadcasted_iota(dtype, shape, dim)


def _shift_div(x, d):
    k = d.bit_length() - 1
    assert d == 1 << k
    return lax.shift_right_logical(x, jnp.full(x.shape, k, jnp.int32))


def _sigmoid(x):
    return 1.0 / (1.0 + jnp.exp(-x))


def _log_sigmoid(x):
    return jnp.minimum(x, 0.0) - jnp.log(1.0 + jnp.exp(-jnp.abs(x)))


def _dot_nt(a, b, precision=None):
    return lax.dot_general(a, b, (((1,), (1,)), ((), ())), precision=precision,
                           preferred_element_type=F32)


def _dot(a, b, precision=None):
    return jnp.dot(a, b, precision=precision, preferred_element_type=F32)


def _masked_softmax_parts(s, mask):
    s = jnp.where(mask, s, NEG_INF)
    m = jnp.max(s, axis=-1, keepdims=True)
    m = jnp.where(m == NEG_INF, 0.0, m)
    e = jnp.exp(s - m)
    return e, jnp.maximum(jnp.sum(e, axis=-1, keepdims=True), 1e-30)


def _online_update_t(carry, s, v_bf16):
    m, l, acc = carry
    m_new = jnp.maximum(m, jnp.max(s, axis=0, keepdims=True))
    m_safe = jnp.where(m_new == NEG_INF, 0.0, m_new)
    p = jnp.exp(s - m_safe)
    alpha = jnp.exp(m - m_safe)
    l = alpha * l + jnp.sum(p, axis=0, keepdims=True)
    pv = lax.dot_general(v_bf16, p.astype(BF16), (((0,), (0,)), ((), ())), preferred_element_type=F32)
    return m_new, l, alpha * acc + pv


def _rel_bucket(dist):
    n = jnp.maximum(dist, 0)
    large = jnp.full(n.shape, _REL_EXACT, jnp.int32)
    for t in _REL_THRESHOLDS:
        large = large + jnp.where(n >= t, 1, 0)
    return jnp.where(n < _REL_EXACT, n, large)


def _rel_bias(bucket, table_ref, head):
    out = jnp.zeros(bucket.shape, F32)
    for b in range(REL_BUCKETS):
        out = jnp.where(bucket == b, table_ref[b, head], out)
    return out


def _top_blocks(score, n_sel, axis=1):
    lane = _iota(score.shape, axis).astype(F32)
    sel = jnp.zeros(score.shape, F32)
    for _ in range(n_sel):
        mx = jnp.max(score, axis=axis, keepdims=True)
        first = jnp.min(jnp.where(score == mx, lane, 1e9), axis=axis, keepdims=True)
        hit = lane == first
        sel = jnp.where(hit & (mx > NEG_INF), 1.0, sel)
        score = jnp.where(hit, NEG_INF, score)
    return sel


def _mm_kernel(x_ref, w_ref, o_ref):
    o_ref[...] = _dot(x_ref[...], w_ref[...])


def _mm(x, w, tm, tn, name):
    m, k = x.shape
    n = w.shape[1]
    assert m % tm == 0 and n % tn == 0
    return pl.pallas_call(
        _mm_kernel, grid=(n // tn, m // tm),
        in_specs=[pl.BlockSpec((tm, k), lambda j, i: (i, 0)),
                  pl.BlockSpec((k, tn), lambda j, i: (0, j))],
        out_specs=pl.BlockSpec((tm, tn), lambda j, i: (i, j)),
        out_shape=jax.ShapeDtypeStruct((m, n), F32),
        compiler_params=_params(2), name=name)(x, w)


def _mm_split_kernel(x_ref, w_ref, *o_refs, plan):
    j = pl.program_id(1)
    res = _dot(x_ref[...], w_ref[...])
    for j_lo, j_hi, outs in plan:
        @pl.when((j >= j_lo) & (j < j_hi))
        def _(outs=outs):
            for oi, lo, hi in outs:
                o_refs[oi][...] = res[:, lo:hi]


def _mm_split(x, m, w, groups, tm, tn, name):
    k = x.shape[1]
    assert m % tm == 0 and sum(sum(g) for g in groups) == w.shape[1]
    plan, out_specs, out_shapes = [], [], []
    j0 = 0
    for g in groups:
        nblk = sum(g) // tn
        assert sum(g) == nblk * tn and (len(g) == 1 or nblk == 1)
        outs, lane = [], 0
        for width in g:
            bw = min(width, tn)
            outs.append((len(out_specs), lane, lane + bw))
            lane += bw
            out_specs.append(pl.BlockSpec(
                (tm, bw), lambda i, j, j0=j0, nblk=nblk: (i, jnp.clip(j - j0, 0, nblk - 1))))
            out_shapes.append(jax.ShapeDtypeStruct((m, width), F32))
        plan.append((j0, j0 + nblk, outs))
        j0 += nblk
    return pl.pallas_call(
        functools.partial(_mm_split_kernel, plan=plan), grid=(m // tm, j0),
        in_specs=[pl.BlockSpec((tm, k), lambda i, j: (i, 0)),
                  pl.BlockSpec((k, tn), lambda i, j: (0, j))],
        out_specs=out_specs, out_shape=out_shapes,
        compiler_params=_params(2), name=name)(x, w)


def _layernorm_rows(z, g, b):
    mu = jnp.mean(z, axis=-1, keepdims=True)
    zc = z - mu
    var = jnp.mean(zc * zc, axis=-1, keepdims=True)
    return zc * lax.rsqrt(var + LN_EPS) * g + b


def _mm_res_ln_kernel(*refs, n_in):
    a_refs, w_refs = refs[:n_in], refs[n_in:2 * n_in]
    x_ref, g_ref, b_ref, o_ref, o16_ref = refs[2 * n_in:]
    y = _dot(a_refs[0][...].astype(BF16), w_refs[0][...])
    for a_ref, w_ref in zip(a_refs[1:], w_refs[1:]):
        y = y + _dot(a_ref[...].astype(BF16), w_ref[...])
    out = _layernorm_rows(DEEPNORM_ALPHA * x_ref[...] + y, g_ref[...], b_ref[...])
    o_ref[...] = out
    o16_ref[...] = out.astype(BF16)


def _mm_res_ln(acts, weights, x, g, b, tm, name):
    t, d = x.shape
    n_in = len(acts)
    in_specs = [pl.BlockSpec((tm, a.shape[1]), lambda i: (i, 0)) for a in acts]
    in_specs += [pl.BlockSpec(w.shape, lambda i: (0, 0)) for w in weights]
    in_specs += [pl.BlockSpec((tm, d), lambda i: (i, 0)),
                 pl.BlockSpec((1, d), lambda i: (0, 0)), pl.BlockSpec((1, d), lambda i: (0, 0))]
    return pl.pallas_call(
        functools.partial(_mm_res_ln_kernel, n_in=n_in), grid=(t // tm,),
        in_specs=in_specs,
        out_specs=[pl.BlockSpec((tm, d), lambda i: (i, 0))] * 2,
        out_shape=[jax.ShapeDtypeStruct((t, d), F32), jax.ShapeDtypeStruct((t, d), BF16)],
        compiler_params=_params(1), name=name)(*acts, *weights, x, g.reshape(1, d), b.reshape(1, d))


def _gates_a_kernel(x_ref, w_ref, wt_ref, b_ref, bt_ref, logf_ref, logft_ref, sg_ref):
    x = x_ref[...]
    z = _dot(x, w_ref[...], HIGHEST)
    logf_ref[...] = _log_sigmoid(z + b_ref[...])
    sg_ref[...] = _sigmoid(z)
    zt = _dot_nt(wt_ref[...], x, HIGHEST)
    logft_ref[...] = _log_sigmoid(zt + bt_ref[...])[:FOX_HEADS]


def _gates_a(x, w_small, b_small, tm):
    t, d = x.shape
    return pl.pallas_call(
        _gates_a_kernel, grid=(t // tm,),
        in_specs=[pl.BlockSpec((tm, d), lambda i: (i, 0)),
                  pl.BlockSpec((d, LANES), lambda i: (0, 0)),
                  pl.BlockSpec((LANES, d), lambda i: (0, 0)),
                  pl.BlockSpec((1, LANES), lambda i: (0, 0)),
                  pl.BlockSpec((LANES, 1), lambda i: (0, 0))],
        out_specs=[pl.BlockSpec((tm, LANES), lambda i: (i, 0)),
                   pl.BlockSpec((FOX_HEADS, tm), lambda i: (0, i)),
                   pl.BlockSpec((tm, LANES), lambda i: (i, 0))],
        out_shape=[jax.ShapeDtypeStruct((t, LANES), F32), jax.ShapeDtypeStruct((FOX_HEADS, t), F32),
                   jax.ShapeDtypeStruct((t, LANES), F32)],
        compiler_params=_params(1), name="gates_a")(
            x, w_small, w_small.T, b_small.reshape(1, LANES), b_small.reshape(LANES, 1))


def _cumsum_kernel(x_ref, o_ref, *, blk):
    s = x_ref.shape[1]
    upper = jnp.where(_iota((blk, blk), 0) <= _iota((blk, blk), 1), 1.0, 0.0)
    carry = jnp.zeros((x_ref.shape[0], 1), F32)
    for j in range(s // blk):
        c = _dot(x_ref[:, j * blk:(j + 1) * blk], upper, HIGHEST) + carry
        o_ref[:, j * blk:(j + 1) * blk] = c
        carry = c[:, blk - 1:blk]


def _cumsum_seq(xt, n, s):
    h = xt.shape[0]
    return pl.pallas_call(
        functools.partial(_cumsum_kernel, blk=min(256, s)), grid=(n,),
        in_specs=[pl.BlockSpec((h, s), lambda i: (0, i))],
        out_specs=pl.BlockSpec((None, h, s), lambda i: (i, 0, 0)),
        out_shape=jax.ShapeDtypeStruct((n, h, s), F32),
        compiler_params=_params(1), name="fox_cumsum")(xt)


def _fox_prompt_kernel(q_ref, k_ref, v_ref, cq_ref, ck_ref, o_ref, *, tq):
    qi = pl.program_id(2)
    q = q_ref[...].astype(BF16)
    cq = cq_ref[...]

    key_minus_query = _iota((tq, tq), 0) - _iota((tq, tq), 1)

    def tile(kb, carry):
        limit = jnp.where(kb < qi, tq, jnp.where(kb == qi, 0, -tq - 1))
        start = pl.multiple_of(jnp.minimum(kb, qi) * tq, tq)
        k = k_ref[pl.ds(start, tq), :].astype(BF16)
        v = v_ref[pl.ds(start, tq), :].astype(BF16)
        s = _dot_nt(k, q) * ATTN_SCALE + cq - ck_ref[pl.ds(start, tq), :]
        return _online_update_t(carry, jnp.where(key_minus_query <= limit, s, NEG_INF), v)

    def body(j, carry):
        return tile(2 * j + 1, tile(2 * j, carry))

    init = (jnp.full((1, tq), NEG_INF, F32), jnp.zeros((1, tq), F32), jnp.zeros((HEAD_DIM, tq), F32))
    _, l, acc = lax.fori_loop(0, (qi + 2) // 2, body, init)
    o_ref[...] = (acc / jnp.maximum(l, 1e-30)).T


def _fox_prompt(q, k, v, c_col, c_row, n, s, tq):
    nq = s // tq
    kv_spec = pl.BlockSpec((s, HEAD_DIM), lambda b, hd, qi: (b, hd))
    return pl.pallas_call(
        functools.partial(_fox_prompt_kernel, tq=tq), grid=(n, FOX_HEADS, nq),
        in_specs=[pl.BlockSpec((tq, HEAD_DIM), lambda b, hd, qi: (b * nq + qi, hd)),
                  kv_spec, kv_spec,
                  pl.BlockSpec((None, None, 1, tq), lambda b, hd, qi: (b, hd, 0, qi)),
                  pl.BlockSpec((None, None, s, 1), lambda b, hd, qi: (b, hd, 0, 0))],
        out_specs=pl.BlockSpec((tq, HEAD_DIM), lambda b, hd, qi: (b * nq + qi, hd)),
        out_shape=jax.ShapeDtypeStruct((n * s, FOX_W), F32),
        compiler_params=_params(3), name="fox_prompt")(q, k, v, c_row, c_col)


def _nsa_bias_kernel(table_ref, tiles_ref, wtiles_ref, cmpb_ref, bk_ref, bw_ref, bc_ref, *, tq, past, n_win):
    d = pl.program_id(0)
    offs = _iota((tq, tq), 1) - _iota((tq, tq), 0)
    dist = d * tq + offs
    bucket = _rel_bucket(dist)
    dist_w = jnp.minimum(d, n_win - 1) * tq + offs
    bucket_w = _rel_bucket(dist_w)
    in_window = (dist_w >= 0) & (dist_w <= NSA_WINDOW)
    nb = cmpb_ref.shape[-1]
    dist_c = jnp.minimum(d, pl.num_programs(0) - 2) * tq + _iota((tq, nb), 0) \
        - (_iota((tq, nb), 1) * NSA_BLOCK + NSA_BLOCK - 1)
    bucket_c = _rel_bucket(dist_c)
    for h in range(NSA_HEADS):
        tiles_ref[h] = jnp.where(dist >= 0, _rel_bias(bucket, table_ref, h), NEG_INF)
        wtiles_ref[h] = jnp.where(in_window, _rel_bias(bucket_w, table_ref, h), NEG_INF)
        cmpb_ref[h] = _rel_bias(bucket_c, table_ref, h)

    @pl.when(d == 0)
    def _():
        win = bw_ref.shape[1] // NSA_GROUPS
        bucket_k = _rel_bucket(past - _shift_div(_iota((1, bk_ref.shape[1]), 1), NSA_GROUPS))
        bucket_w = _rel_bucket(win - _shift_div(_iota((1, bw_ref.shape[1]), 1), NSA_GROUPS))
        bucket_b = _rel_bucket(past - (_iota((1, LANES), 1) * NSA_BLOCK + NSA_BLOCK - 1))
        for h in range(NSA_HEADS):
            bk_ref[h:h + 1, :] = _rel_bias(bucket_k, table_ref, h)
            bw_ref[h:h + 1, :] = _rel_bias(bucket_w, table_ref, h)
            bc_ref[h:h + 1, :] = _rel_bias(bucket_b, table_ref, h)


def _nsa_bias_tables(rel_table, s, tq, past, win):
    nq = s // tq
    nb = LANES
    n_win = min(_window_tiles(tq), nq) + 1
    assert s // NSA_BLOCK <= LANES
    return pl.pallas_call(
        functools.partial(_nsa_bias_kernel, tq=tq, past=past, n_win=n_win),
        grid=(nq + 1,),
        in_specs=[pl.BlockSpec(memory_space=pltpu.SMEM)],
        out_specs=[pl.BlockSpec((None, NSA_HEADS, tq, tq), lambda d: (d, 0, 0, 0)),
                   pl.BlockSpec((None, NSA_HEADS, tq, tq), lambda d: (jnp.minimum(d, n_win - 1), 0, 0, 0)),
                   pl.BlockSpec((None, NSA_HEADS, tq, nb), lambda d: (jnp.minimum(d, nq - 1), 0, 0, 0)),
                   pl.BlockSpec((NSA_HEADS, NSA_GROUPS * past), lambda d: (0, 0)),
                   pl.BlockSpec((NSA_HEADS, NSA_GROUPS * win), lambda d: (0, 0)),
                   pl.BlockSpec((NSA_HEADS, LANES), lambda d: (0, 0))],
        out_shape=[jax.ShapeDtypeStruct((nq + 1, NSA_HEADS, tq, tq), F32),
                   jax.ShapeDtypeStruct((n_win, NSA_HEADS, tq, tq), F32),
                   jax.ShapeDtypeStruct((nq, NSA_HEADS, tq, nb), F32),
                   jax.ShapeDtypeStruct((NSA_HEADS, NSA_GROUPS * past), F32),
                   jax.ShapeDtypeStruct((NSA_HEADS, NSA_GROUPS * win), F32),
                   jax.ShapeDtypeStruct((NSA_HEADS, LANES), F32)],
        compiler_params=_params(1), name="nsa_bias")(rel_table)


def _window_tiles(tq):
    return (NSA_WINDOW + tq - 1) // tq


def _nsa_prompt_kernel(q_ref, kc_ref, vc_ref, ks_ref, vs_ref, kw_ref, vw_ref, gate_ref, tiles_ref,
                       wtiles_ref, cmpb_ref, o_ref, *, tq):
    qi = pl.program_id(2)
    s_len = kc_ref.shape[0]
    n_blk = s_len // NSA_BLOCK
    nb = LANES
    rows = NSA_HPG * tq
    q = jnp.concatenate([q_ref[:, h * HEAD_DIM:(h + 1) * HEAD_DIM] for h in range(NSA_HPG)],
                        axis=0).astype(BF16)

    def block_means(ref):
        m = jnp.mean(ref[...].reshape(n_blk, NSA_BLOCK, HEAD_DIM), axis=1)
        return jnp.concatenate([m, jnp.zeros((nb - n_blk, HEAD_DIM), F32)], axis=0).astype(BF16)

    qpos_r = qi * tq + (_iota((rows, nb), 0) & (tq - 1))
    blk_r = _iota((rows, nb), 1)
    bias_c = jnp.concatenate([cmpb_ref[h] for h in range(NSA_HPG)], axis=0)
    s_c = _dot_nt(q, block_means(kc_ref)) * ATTN_SCALE + bias_c
    e_c, den_c = _masked_softmax_parts(
        s_c, (blk_r < n_blk) & (qpos_r >= blk_r * NSA_BLOCK + NSA_BLOCK - 1))
    p_c = e_c / den_c
    o_c = _dot(p_c.astype(BF16), block_means(vc_ref))

    p_sum = p_c[0:tq]
    for h in range(1, NSA_HPG):
        p_sum = p_sum + p_c[h * tq:(h + 1) * tq]
    blk = _iota((tq, nb), 1)
    cur = _shift_div(qi * tq + _iota((tq, nb), 0), NSA_BLOCK)
    forced = (blk == 0) | (blk == cur) | (blk == cur - 1)
    score = jnp.where(forced, POS_INF, jnp.where(blk <= cur, p_sum, NEG_INF))
    sel_t = _top_blocks(score.T, min(NSA_TOPK, n_blk), axis=0).astype(BF16)

    init = (jnp.full((1, rows), NEG_INF, F32), jnp.zeros((1, rows), F32), jnp.zeros((HEAD_DIM, rows), F32))
    lane_tile = lambda ref, idx: jnp.concatenate([ref[idx, h] for h in range(NSA_HPG)], axis=1)

    def sel_tile(kt, carry):
        threshold = jnp.where(kt <= qi, 0.5, 2.0)
        start = pl.multiple_of(jnp.minimum(kt, qi) * tq, tq)
        k = ks_ref[pl.ds(start, tq), :].astype(BF16)
        v = vs_ref[pl.ds(start, tq), :].astype(BF16)
        s = _dot_nt(k, q) * ATTN_SCALE + lane_tile(tiles_ref, jnp.maximum(qi - kt, 0))
        key_blk = _shift_div(start + _iota((tq, nb), 0), NSA_BLOCK)
        expand = jnp.where(_iota((tq, nb), 1) == key_blk, 1.0, 0.0).astype(BF16)
        picked = _dot(expand, sel_t)
        picked = jnp.concatenate([picked] * NSA_HPG, axis=1)
        return _online_update_t(carry, jnp.where(picked > threshold, s, NEG_INF), v)

    def sel_body(j, carry):
        return sel_tile(2 * j + 1, sel_tile(2 * j, carry))

    _, l_s, acc_s = lax.fori_loop(0, (qi + 2) // 2, sel_body, init)
    o_s = acc_s / jnp.maximum(l_s, 1e-30)

    carry = init
    for delta in range(min(_window_tiles(tq), s_len // tq - 1), -1, -1):
        kt = qi - delta
        start = pl.multiple_of(jnp.maximum(kt, 0) * tq, tq)
        k = kw_ref[pl.ds(start, tq), :].astype(BF16)
        v = vw_ref[pl.ds(start, tq), :].astype(BF16)
        s = _dot_nt(k, q) * ATTN_SCALE + lane_tile(wtiles_ref, delta)
        carry = _online_update_t(carry, s if delta == 0 else jnp.where(kt >= 0, s, NEG_INF), v)
    _, l_w, acc_w = carry
    o_w = acc_w / jnp.maximum(l_w, 1e-30)

    gate = gate_ref[...]
    for h in range(NSA_HPG):
        r = slice(h * tq, (h + 1) * tq)
        o_ref[:, h * HEAD_DIM:(h + 1) * HEAD_DIM] = (
            gate[:, 3 * h:3 * h + 1] * o_c[r] + gate[:, 3 * h + 1:3 * h + 2] * o_s[:, r].T
            + gate[:, 3 * h + 2:3 * h + 3] * o_w[:, r].T)


def _nsa_prompt(q, kv, gates, tiles, wtiles, cmpb, n, s, tq):
    nq = s // tq
    grp_w = NSA_HPG * HEAD_DIM
    kv_spec = pl.BlockSpec((s, HEAD_DIM), lambda b, g, qi: (b, g))
    tile_spec = lambda a: pl.BlockSpec((a.shape[0], NSA_HPG, tq, tq), lambda b, g, qi: (0, g, 0, 0))
    return pl.pallas_call(
        functools.partial(_nsa_prompt_kernel, tq=tq), grid=(n, NSA_GROUPS, nq),
        in_specs=[pl.BlockSpec((tq, grp_w), lambda b, g, qi: (b * nq + qi, g))]
        + [kv_spec] * 6
        + [pl.BlockSpec((None, tq, 3 * NSA_HPG), lambda b, g, qi: (g, b * nq + qi, 0)),
           tile_spec(tiles), tile_spec(wtiles),
           pl.BlockSpec((None, NSA_HPG, tq, LANES), lambda b, g, qi: (qi, g, 0, 0))],
        out_specs=pl.BlockSpec((tq, grp_w), lambda b, g, qi: (b * nq + qi, g)),
        out_shape=jax.ShapeDtypeStruct((n * s, NSA_W), F32),
        compiler_params=_params(3), name="nsa_prompt")(q, *kv, gates, tiles, wtiles, cmpb)


def _softmax_step(scr, s_parts, v_refs):
    m_scr, l_scr, acc_scr = scr
    s = jnp.concatenate(s_parts, axis=1)
    m = m_scr[...]
    m_new = jnp.maximum(m, jnp.max(s, axis=-1, keepdims=True))
    m_safe = jnp.where(m_new == NEG_INF, 0.0, m_new)
    p = jnp.exp(s - m_safe)
    alpha = jnp.exp(m - m_safe)
    l_scr[...] = alpha * l_scr[...] + jnp.sum(p, axis=-1, keepdims=True)
    acc = alpha * acc_scr[...]
    width = s_parts[0].shape[1]
    for i, v_ref in enumerate(v_refs):
        acc = acc + _dot(p[:, i * width:(i + 1) * width].astype(BF16), v_ref[...].astype(BF16))
    acc_scr[...] = acc
    m_scr[...] = m_new


def _fox_sample_kernel(pt_ref, q_ref, kn_ref, vn_ref, lfn_ref, later_ref, *rest, pp):
    k_refs, v_refs, lf_refs = rest[:pp], rest[pp:2 * pp], rest[2 * pp:3 * pp]
    o_ref, m_scr, l_scr, acc_scr, suf_scr = rest[3 * pp:]
    step = pl.program_id(1)
    q = q_ref[...]

    @pl.when(step == 0)
    def _():
        m_scr[...] = jnp.sum(q * kn_ref[...], axis=-1, keepdims=True) * ATTN_SCALE
        l_scr[...] = jnp.ones(l_scr.shape, F32)
        acc_scr[...] = vn_ref[...].astype(BF16).astype(F32)
        suf_scr[...] = lfn_ref[...]

    cols = k_refs[0].shape[0]
    own = (_iota((FOX_HEADS, cols), 1) & (FOX_HEADS - 1)) == _iota((FOX_HEADS, cols), 0)
    q16 = q.astype(BF16)
    later = later_ref[...]
    lf_all = jnp.concatenate([lf_ref[...] for lf_ref in lf_refs], axis=0)
    lf_hi = lf_all.astype(BF16)
    lf_lo = (lf_all - lf_hi.astype(F32)).astype(BF16)
    within = _dot(lf_hi, later) + _dot(lf_lo, later)
    suf = suf_scr[...]
    s_parts = []
    for i, k_ref in enumerate(k_refs):
        rows = slice(i * FOX_HEADS, (i + 1) * FOX_HEADS)
        s = _dot_nt(q16, k_ref[...].astype(BF16)) * ATTN_SCALE + (within[rows] + suf)
        s_parts.append(jnp.where(own, s, NEG_INF))
        suf = suf + jnp.sum(lf_all[rows], axis=-1, keepdims=True)
    suf_scr[...] = suf
    _softmax_step((m_scr, l_scr, acc_scr), s_parts, v_refs)

    @pl.when(step == pl.num_programs(1) - 1)
    def _():
        o_ref[...] = acc_scr[...] / jnp.maximum(l_scr[...], 1e-30)


FOX_PAGES_PER_STEP = 8
NSA_PAGES_PER_STEP = 16


def _fox_sample(page_table, layer, q, k_new, v_new, logf_new, cache_k, cache_v, cache_logf_t):
    nb, n_pages = page_table.shape
    pp = FOX_PAGES_PER_STEP
    assert n_pages % pp == 0
    rows = PAGE_SIZE * FOX_HEADS
    later = np.repeat(np.tril(np.ones((PAGE_SIZE, PAGE_SIZE), np.float32), -1), FOX_HEADS, axis=1)
    per_b = lambda shape: pl.BlockSpec((None,) + shape, lambda b, p, pt: (b, 0, 0))
    page = lambda i: (lambda b, p, pt: (layer, pt[b, n_pages - 1 - (p * pp + i)], 0, 0))
    grid_spec = pltpu.PrefetchScalarGridSpec(
        num_scalar_prefetch=1, grid=(nb, n_pages // pp),
        in_specs=[per_b((FOX_HEADS, HEAD_DIM)), per_b((FOX_HEADS, HEAD_DIM)), per_b((FOX_HEADS, HEAD_DIM)),
                  per_b((FOX_HEADS, 1)),
                  pl.BlockSpec((PAGE_SIZE, rows), lambda b, p, pt: (0, 0))]
        + [pl.BlockSpec((None, None, rows, HEAD_DIM), page(i)) for i in range(pp)]
        + [pl.BlockSpec((None, None, rows, HEAD_DIM), page(i)) for i in range(pp)]
        + [pl.BlockSpec((None, None, FOX_HEADS, PAGE_SIZE), page(i)) for i in range(pp)],
        out_specs=per_b((FOX_HEADS, HEAD_DIM)),
        scratch_shapes=[pltpu.VMEM((FOX_HEADS, 1), F32), pltpu.VMEM((FOX_HEADS, 1), F32),
                        pltpu.VMEM((FOX_HEADS, HEAD_DIM), F32), pltpu.VMEM((FOX_HEADS, 1), F32)])
    return pl.pallas_call(
        functools.partial(_fox_sample_kernel, pp=pp), grid_spec=grid_spec,
        out_shape=jax.ShapeDtypeStruct((nb, FOX_HEADS, HEAD_DIM), F32),
        compiler_params=_params(2), name="fox_sample")(
            page_table, q, k_new, v_new, logf_new, jnp.asarray(later, BF16),
            *([cache_k] * pp), *([cache_v] * pp), *([cache_logf_t] * pp))


def _group_block_sums(x):
    sub = 8
    y = jnp.sum(x.reshape(x.shape[0] // sub, sub, x.shape[1]), axis=0)
    shift = NSA_GROUPS
    while shift < sub:
        y = y + pltpu.roll(y, shift, 0)
        shift *= 2
    return y


def _nsa_sample_cmp_kernel(pt_ref, q_ref, bc_ref, *rest, past, pp):
    k_refs, v_refs = rest[:pp], rest[pp:2 * pp]
    oc_ref, sel_ref, kcb_scr, vcb_scr = rest[2 * pp:]
    step = pl.program_id(1)
    n_blk = past // NSA_BLOCK
    per_page = PAGE_SIZE // NSA_BLOCK
    blk_rows = NSA_BLOCK * NSA_GROUPS

    @pl.when(step == 0)
    def _():
        kcb_scr[...] = jnp.zeros(kcb_scr.shape, F32)
        vcb_scr[...] = jnp.zeros(vcb_scr.shape, F32)

    row_id = _iota((LANES, HEAD_DIM), 0)
    for refs, scr in ((k_refs, kcb_scr), (v_refs, vcb_scr)):
        tabs = [scr[g] for g in range(NSA_GROUPS)]
        for i, ref in enumerate(refs):
            for bl in range(per_page):
                sums = _group_block_sums(ref[bl * blk_rows:(bl + 1) * blk_rows, :]) * (1.0 / NSA_BLOCK)
                blk_id = (step * pp + i) * per_page + bl
                for g in range(NSA_GROUPS):
                    tabs[g] = jnp.where(row_id == blk_id, sums[g:g + 1], tabs[g])
        for g in range(NSA_GROUPS):
            scr[g] = tabs[g]

    @pl.when(step == pl.num_programs(1) - 1)
    def _():
        q16 = q_ref[...].astype(BF16)
        blk = _iota((NSA_HEADS, LANES), 1)
        head_grp = _shift_div(_iota((NSA_HEADS, LANES), 0), NSA_HPG)
        by_group = lambda parts: functools.reduce(
            lambda acc, gp: jnp.where(head_grp == gp[0], gp[1], acc), enumerate(parts), jnp.zeros_like(parts[0]))
        s_c = by_group([_dot_nt(q16, kcb_scr[g].astype(BF16)) for g in range(NSA_GROUPS)])
        s_c = s_c * ATTN_SCALE + bc_ref[...]
        e_c, den_c = _masked_softmax_parts(s_c, (blk < n_blk) & (past >= blk * NSA_BLOCK + NSA_BLOCK - 1))
        p_c = e_c / den_c
        p16 = p_c.astype(BF16)
        oc_ref[...] = by_group([_dot(p16, vcb_scr[g].astype(BF16)) for g in range(NSA_GROUPS)])
        p_sum = jnp.zeros((NSA_HEADS, LANES), F32)
        for g in range(NSA_GROUPS):
            tot = jnp.sum(p_c[g * NSA_HPG:(g + 1) * NSA_HPG], axis=0, keepdims=True)
            p_sum = jnp.where(head_grp == g, tot, p_sum)
        cur = past // NSA_BLOCK
        forced = (blk == 0) | (blk == cur) | (blk == cur - 1)
        score = jnp.where(forced, POS_INF, jnp.where(blk <= cur, p_sum, NEG_INF))
        sel_ref[...] = _top_blocks(score, min(NSA_TOPK, cur + 1))


def _nsa_sample_cmp(page_table, layer, q, cache_k, cache_v, bc, past):
    assert past % NSA_BLOCK == 0 and past // NSA_BLOCK < LANES
    nb, n_pages = page_table.shape
    pp = NSA_PAGES_PER_STEP
    assert n_pages % pp == 0
    rows = PAGE_SIZE * NSA_GROUPS
    page = lambda i: (lambda b, p, pt: (layer, pt[b, p * pp + i], 0, 0))
    grid_spec = pltpu.PrefetchScalarGridSpec(
        num_scalar_prefetch=1, grid=(nb, n_pages // pp),
        in_specs=[pl.BlockSpec((None, NSA_HEADS, HEAD_DIM), lambda b, p, pt: (b, 0, 0)),
                  pl.BlockSpec((NSA_HEADS, LANES), lambda b, p, pt: (0, 0))]
        + [pl.BlockSpec((None, None, rows, HEAD_DIM), page(i)) for i in range(pp)] * 2,
        out_specs=[pl.BlockSpec((None, NSA_HEADS, HEAD_DIM), lambda b, p, pt: (b, 0, 0)),
                   pl.BlockSpec((None, NSA_HEADS, LANES), lambda b, p, pt: (b, 0, 0))],
        scratch_shapes=[pltpu.VMEM((NSA_GROUPS, LANES, HEAD_DIM), F32),
                        pltpu.VMEM((NSA_GROUPS, LANES, HEAD_DIM), F32)])
    return pl.pallas_call(
        functools.partial(_nsa_sample_cmp_kernel, past=past, pp=pp), grid_spec=grid_spec,
        out_shape=[jax.ShapeDtypeStruct((nb, NSA_HEADS, HEAD_DIM), F32),
                   jax.ShapeDtypeStruct((nb, NSA_HEADS, LANES), F32)],
        compiler_params=_params(2), name="nsa_sample_cmp")(
            page_table, q, bc, *([cache_k] * pp), *([cache_v] * pp))


def _per_head(new_ref):
    head_grp = _shift_div(_iota((NSA_HEADS, HEAD_DIM), 0), NSA_HPG)
    out = jnp.zeros((NSA_HEADS, HEAD_DIM), F32)
    for g in range(NSA_GROUPS):
        out = jnp.where(head_grp == g, new_ref[g:g + 1, :], out)
    return out


def _nsa_sample_sel_kernel(pt_ref, q_ref, ksn_ref, vsn_ref, kwn_ref, vwn_ref, sel_ref, oc_ref, gate_ref,
                           kw_ref, vw_ref, bk_ref, bw_ref, b0_ref, *rest, pp):
    ks_refs, vs_refs = rest[:pp], rest[pp:2 * pp]
    o_ref, m_scr, l_scr, acc_scr = rest[2 * pp:]
    step = pl.program_id(1)
    per_page = PAGE_SIZE // NSA_BLOCK
    q = q_ref[...]
    q16 = q.astype(BF16)
    b0 = b0_ref[...]
    own = lambda cols: ((_iota((NSA_HEADS, cols), 1) & (NSA_GROUPS - 1))
                        == _shift_div(_iota((NSA_HEADS, cols), 0), NSA_HPG))

    @pl.when(step == 0)
    def _():
        m_scr[...] = jnp.sum(q * _per_head(ksn_ref), axis=-1, keepdims=True) * ATTN_SCALE + b0
        l_scr[...] = jnp.ones(l_scr.shape, F32)
        acc_scr[...] = _per_head(vsn_ref).astype(BF16).astype(F32)

    cols = ks_refs[0].shape[0]
    sel16 = sel_ref[...].astype(BF16)
    key_blk = _shift_div(_iota((LANES, cols), 1), NSA_GROUPS * NSA_BLOCK)
    s_parts = []
    for i, ks_ref in enumerate(ks_refs):
        pg = step * pp + i
        expand = jnp.where(_iota((LANES, cols), 0) == pg * per_page + key_blk, 1.0, 0.0)
        picked = _dot(sel16, expand.astype(BF16))
        start = pl.multiple_of(pg * cols, cols)
        s = _dot_nt(q16, ks_ref[...].astype(BF16)) * ATTN_SCALE + bk_ref[:, pl.ds(start, cols)]
        s_parts.append(jnp.where(own(cols) & (picked > 0.5), s, NEG_INF))
    _softmax_step((m_scr, l_scr, acc_scr), s_parts, vs_refs)

    @pl.when(step == pl.num_programs(1) - 1)
    def _():
        o_s = acc_scr[...] / jnp.maximum(l_scr[...], 1e-30)
        wcols = kw_ref.shape[0]
        s_w = _dot_nt(q16, kw_ref[...].astype(BF16)) * ATTN_SCALE + bw_ref[...]
        s_w = jnp.where(own(wcols), s_w, NEG_INF)
        s_n = jnp.sum(q * _per_head(kwn_ref), axis=-1, keepdims=True) * ATTN_SCALE + b0
        m_w = jnp.maximum(jnp.max(s_w, axis=-1, keepdims=True), s_n)
        e_w = jnp.exp(s_w - m_w)
        e_n = jnp.exp(s_n - m_w)
        den = jnp.maximum(jnp.sum(e_w, axis=-1, keepdims=True) + e_n, 1e-30)
        p_w = (e_w / den).astype(BF16)
        p_n = (e_n / den).astype(BF16).astype(F32)
        o_w = _dot(p_w, vw_ref[...].astype(BF16)) + p_n * _per_head(vwn_ref).astype(BF16).astype(F32)
        gate = gate_ref[...]
        o_ref[...] = gate[:, 0:1] * oc_ref[...] + gate[:, 1:2] * o_s + gate[:, 2:3] * o_w


def _nsa_sample_sel(page_table, layer, q, ks_new, vs_new, kw_new, vw_new, sel, o_c, gates,
                    cache_ks, cache_vs, win_k, win_v, bk, bw, b0):
    nb, n_pages = page_table.shape
    pp = NSA_PAGES_PER_STEP
    assert n_pages % pp == 0
    wrows = win_k.shape[2]
    rows = PAGE_SIZE * NSA_GROUPS
    per_b = lambda shape: pl.BlockSpec((None,) + shape, lambda b, p, pt: (b, 0, 0))
    page = lambda i: (lambda b, p, pt: (layer, pt[b, p * pp + i], 0, 0))
    const = lambda shape: pl.BlockSpec(shape, lambda b, p, pt: (0, 0))
    new = per_b((NSA_GROUPS, HEAD_DIM))
    grid_spec = pltpu.PrefetchScalarGridSpec(
        num_scalar_prefetch=1, grid=(nb, n_pages // pp),
        in_specs=[per_b((NSA_HEADS, HEAD_DIM)), new, new, new, new,
                  per_b((NSA_HEADS, LANES)), per_b((NSA_HEADS, HEAD_DIM)), per_b((NSA_HEADS, 3)),
                  pl.BlockSpec((None, None, wrows, HEAD_DIM), lambda b, p, pt: (layer, b, 0, 0)),
                  pl.BlockSpec((None, None, wrows, HEAD_DIM), lambda b, p, pt: (layer, b, 0, 0)),
                  const((NSA_HEADS, n_pages * rows)), const((NSA_HEADS, wrows)), const((NSA_HEADS, 1))]
        + [pl.BlockSpec((None, None, rows, HEAD_DIM), page(i)) for i in range(pp)] * 2,
        out_specs=per_b((NSA_HEADS, HEAD_DIM)),
        scratch_shapes=[pltpu.VMEM((NSA_HEADS, 1), F32), pltpu.VMEM((NSA_HEADS, 1), F32),
                        pltpu.VMEM((NSA_HEADS, HEAD_DIM), F32)])
    return pl.pallas_call(
        functools.partial(_nsa_sample_sel_kernel, pp=pp), grid_spec=grid_spec,
        out_shape=jax.ShapeDtypeStruct((nb, NSA_HEADS, HEAD_DIM), F32),
        compiler_params=_params(2), name="nsa_sample_sel")(
            page_table, q, ks_new, vs_new, kw_new, vw_new, sel, o_c, gates,
            win_k, win_v, bk, bw, b0, *([cache_ks] * pp), *([cache_vs] * pp))


def _gla_gate_kernel(g_ref, w_ref, b_ref, o_ref):
    z = _dot(g_ref[...].astype(BF16), w_ref[...]) + b_ref[...]
    o_ref[...] = _log_sigmoid(z) * (1.0 / GLA_TAU)


def _gla_gate(hc, col_g, w_g2_pad, b_g, tm):
    t = hc.shape[0]
    n = w_g2_pad.shape[1]
    return pl.pallas_call(
        _gla_gate_kernel, grid=(t // tm,),
        in_specs=[pl.BlockSpec((tm, LANES), lambda i: (i, col_g)),
                  pl.BlockSpec((LANES, n), lambda i: (0, 0)),
                  pl.BlockSpec((1, n), lambda i: (0, 0))],
        out_specs=pl.BlockSpec((tm, n), lambda i: (i, 0)),
        out_shape=jax.ShapeDtypeStruct((t, n), F32),
        compiler_params=_params(1), name="gla_gate")(hc, w_g2_pad, b_g.reshape(1, n))


def _gla_out_gate(o, r, g_norm):
    o = o * lax.rsqrt(jnp.mean(o * o, axis=-1, keepdims=True) + LN_EPS) * g_norm
    return o * (r * _sigmoid(r))


def _gla_prompt_kernel(q_ref, k_ref, v_ref, r_ref, la_ref, gn_ref, o_ref, st_ref, stt_scr):
    c = pl.program_id(1)
    ch = q_ref.shape[0]

    @pl.when(c == 0)
    def _():
        stt_scr[...] = jnp.zeros(stt_scr.shape, F32)

    la = la_ref[...]
    row = _iota((ch, ch), 0)
    col = _iota((ch, ch), 1)
    b_all = _dot(jnp.where(row >= col, 1.0, 0.0), la, HIGHEST)
    sub_start = (row >> int(math.log2(GLA_SUB))) << int(math.log2(GLA_SUB))
    base_all = _dot(jnp.where(col < sub_start, 1.0, 0.0), la, HIGHEST)
    krow = _iota((ch, 1), 0)
    for h in range(GLA_HEADS):
        dk = slice(h * GLA_DK, (h + 1) * GLA_DK)
        dv = slice(h * GLA_DV, (h + 1) * GLA_DV)
        b, base = b_all[:, dk], base_all[:, dk]
        q = q_ref[:, dk] * (GLA_DK ** -0.5)
        k = k_ref[:, dk]
        v16 = v_ref[:, dv].astype(BF16)
        stt = stt_scr[h]

        o = _dot_nt((q * jnp.exp(b)).astype(BF16), stt.astype(BF16))
        q_in = (q * jnp.exp(b - base)).astype(BF16)
        att_rows = []
        for i in range(ch // GLA_SUB):
            lo, hi = i * GLA_SUB, (i + 1) * GLA_SUB
            expo = jnp.where(krow < hi, base[lo:lo + 1] - b, NEG_INF)
            k_in = (k * jnp.exp(expo)).astype(BF16)
            att_rows.append(_dot_nt(q_in[lo:hi], k_in))
        att = jnp.where(row >= col, jnp.concatenate(att_rows, axis=0), 0.0)
        o = o + _dot(att.astype(BF16), v16)

        b_last = b[ch - 1:ch]
        k_dec = (k * jnp.exp(b_last - b)).astype(BF16)
        stt = stt * jnp.exp(b_last) + lax.dot_general(v16, k_dec, (((0,), (0,)), ((), ())),
                                                       preferred_element_type=F32)
        stt_scr[h] = stt
        o_ref[:, dv] = _gla_out_gate(o, r_ref[:, dv], gn_ref[...])

    @pl.when(c == pl.num_programs(1) - 1)
    def _():
        for h in range(GLA_HEADS):
            st_ref[h] = stt_scr[h].T


def _gla_prompt(hc, la, g_norm, n, s):
    ch = GLA_CHUNK if s % GLA_CHUNK == 0 else s
    nc = s // ch
    rows = lambda b, c: b * nc + c
    qk_w, v_w = GLA_HEADS * GLA_DK, GLA_HEADS * GLA_DV
    assert v_w == 2 * qk_w
    return pl.pallas_call(
        _gla_prompt_kernel, grid=(n, nc),
        in_specs=[pl.BlockSpec((ch, qk_w), lambda b, c: (rows(b, c), 0)),
                  pl.BlockSpec((ch, qk_w), lambda b, c: (rows(b, c), 1)),
                  pl.BlockSpec((ch, v_w), lambda b, c: (rows(b, c), 1)),
                  pl.BlockSpec((ch, v_w), lambda b, c: (rows(b, c), 2)),
                  pl.BlockSpec((ch, qk_w), lambda b, c: (rows(b, c), 0)),
                  pl.BlockSpec((1, GLA_DV), lambda b, c: (0, 0))],
        out_specs=[pl.BlockSpec((ch, v_w), lambda b, c: (rows(b, c), 0)),
                   pl.BlockSpec((None, GLA_HEADS, GLA_DK, GLA_DV), lambda b, c: (b, 0, 0, 0))],
        out_shape=[jax.ShapeDtypeStruct((n * s, v_w), F32),
                   jax.ShapeDtypeStruct((n, GLA_HEADS, GLA_DK, GLA_DV), F32)],
        scratch_shapes=[pltpu.VMEM((GLA_HEADS, GLA_DV, GLA_DK), F32)],
        compiler_params=_params(2), name="gla_prompt")(hc, hc, hc, hc, la, g_norm.reshape(1, GLA_DV))


def _gla_sample_kernel(q_ref, k_ref, la_ref, v_ref, r_ref, gn_ref, s0_ref, o_ref, st_ref):
    for h in range(GLA_HEADS):
        st = jnp.exp(la_ref[h]) * s0_ref[h] + k_ref[h] * v_ref[h]
        st_ref[h] = st
        o = jnp.sum((q_ref[h] * (GLA_DK ** -0.5)) * st, axis=0, keepdims=True)
        o_ref[h] = _gla_out_gate(o, r_ref[h], gn_ref[...])


def _gla_sample(q, k, la, v, r, g_norm, s0):
    nb = q.shape[0]
    col = pl.BlockSpec((None, GLA_HEADS, GLA_DK, 1), lambda b: (b, 0, 0, 0))
    rowv = pl.BlockSpec((None, GLA_HEADS, 1, GLA_DV), lambda b: (b, 0, 0, 0))
    full = pl.BlockSpec((None, GLA_HEADS, GLA_DK, GLA_DV), lambda b: (b, 0, 0, 0))
    return pl.pallas_call(
        _gla_sample_kernel, grid=(nb,),
        in_specs=[col, col, col, rowv, rowv, pl.BlockSpec((1, GLA_DV), lambda b: (0, 0)), full],
        out_specs=[rowv, full],
        out_shape=[jax.ShapeDtypeStruct((nb, GLA_HEADS, 1, GLA_DV), F32),
                   jax.ShapeDtypeStruct((nb, GLA_HEADS, GLA_DK, GLA_DV), F32)],
        compiler_params=_params(1), name="gla_sample")(q, k, la, v, r, g_norm.reshape(1, GLA_DV), s0)


def _router_kernel(x_ref, w_ref, b_ref, idx_ref, gate_ref):
    logits = _dot(x_ref[...], w_ref[...], HIGHEST) + b_ref[...]
    lane = _iota(logits.shape, 1).astype(F32)
    logits = jnp.where(lane < N_EXPERTS, logits, NEG_INF)
    top_v = jnp.full(logits.shape, NEG_INF, F32)
    top_i = jnp.zeros(logits.shape, F32)
    for kk in range(TOP_K):
        mx = jnp.max(logits, axis=-1, keepdims=True)
        first = jnp.min(jnp.where(logits == mx, lane, 1e9), axis=-1, keepdims=True)
        top_v = jnp.where(lane == kk, mx, top_v)
        top_i = jnp.where(lane == kk, first, top_i)
        logits = jnp.where(lane == first, NEG_INF, logits)
    e = jnp.exp(top_v - jnp.max(top_v, axis=-1, keepdims=True))
    gate_ref[...] = e / jnp.sum(e, axis=-1, keepdims=True)
    idx_ref[...] = top_i.astype(jnp.int32)


def _router(x, w_pad, b_pad, tm):
    t, d = x.shape
    return pl.pallas_call(
        _router_kernel, grid=(t // tm,),
        in_specs=[pl.BlockSpec((tm, d), lambda i: (i, 0)),
                  pl.BlockSpec((d, LANES), lambda i: (0, 0)),
                  pl.BlockSpec((1, LANES), lambda i: (0, 0))],
        out_specs=[pl.BlockSpec((tm, LANES), lambda i: (i, 0))] * 2,
        out_shape=[jax.ShapeDtypeStruct((t, LANES), jnp.int32), jax.ShapeDtypeStruct((t, LANES), F32)],
        compiler_params=_params(1), name="router")(x, w_pad, b_pad)


def _moe_gate_up_kernel(te_ref, tf_ref, tv_ref, x_ref, wg_ref, wu_ref, bg_ref, bu_ref, o_ref, wg16, wu16):
    i = pl.program_id(1)

    @pl.when(tf_ref[i] == 1)
    def _():
        wg16[...] = wg_ref[...].astype(BF16)
        wu16[...] = wu_ref[...].astype(BF16)

    @pl.when(tv_ref[i] == 1)
    def _():
        x = x_ref[...].astype(BF16)
        for lo in range(0, o_ref.shape[1], MOE_ACT_CHUNK):
            cs = slice(lo, lo + MOE_ACT_CHUNK)
            g = jnp.minimum(_dot(x, wg16[:, cs]) + bg_ref[:, cs], SWIGLU_LIMIT)
            u = jnp.clip(_dot(x, wu16[:, cs]) + bu_ref[:, cs], -SWIGLU_LIMIT, SWIGLU_LIMIT)
            o_ref[:, cs] = ((u + 1.0) * g * _sigmoid(SWIGLU_ALPHA * g)).astype(BF16)

    @pl.when(tv_ref[i] == 0)
    def _():
        o_ref[...] = jnp.zeros(o_ref.shape, BF16)


def _moe_gate_up(te, tf, tv, xs, w_gu, b_gu, layer, tmm, tn):
    p_pad, d = xs.shape
    de = w_gu.shape[-1] // 2
    nj = de // tn
    grid_spec = pltpu.PrefetchScalarGridSpec(
        num_scalar_prefetch=3, grid=(nj, p_pad // tmm),
        in_specs=[pl.BlockSpec((tmm, d), lambda j, i, te, tf, tv: (i, 0)),
                  pl.BlockSpec((None, None, d, tn), lambda j, i, te, tf, tv: (layer, te[i], 0, j)),
                  pl.BlockSpec((None, None, d, tn), lambda j, i, te, tf, tv: (layer, te[i], 0, nj + j)),
                  pl.BlockSpec((None, None, 1, tn), lambda j, i, te, tf, tv: (layer, te[i], 0, j)),
                  pl.BlockSpec((None, None, 1, tn), lambda j, i, te, tf, tv: (layer, te[i], 0, nj + j))],
        out_specs=pl.BlockSpec((tmm, tn), lambda j, i, te, tf, tv: (i, j)),
        scratch_shapes=[pltpu.VMEM((d, tn), BF16), pltpu.VMEM((d, tn), BF16)])
    b4 = b_gu.reshape(b_gu.shape[0], b_gu.shape[1], 1, b_gu.shape[2])
    return pl.pallas_call(
        _moe_gate_up_kernel, grid_spec=grid_spec,
        out_shape=jax.ShapeDtypeStruct((p_pad, de), BF16),
        compiler_params=_params(2), name="moe_gate_up")(te, tf, tv, xs, w_gu, w_gu, b4, b4)


def _moe_down_kernel(te_ref, tf_ref, tv_ref, a_ref, w_ref, b_ref, o_ref, w16):
    i = pl.program_id(1)

    @pl.when(tf_ref[i] == 1)
    def _():
        w16[...] = w_ref[...].astype(BF16)

    @pl.when(tv_ref[i] == 1)
    def _():
        o_ref[...] = _dot(a_ref[...], w16[...]) + b_ref[...]

    @pl.when(tv_ref[i] == 0)
    def _():
        o_ref[...] = jnp.zeros(o_ref.shape, F32)


def _moe_down(te, tf, tv, a, w_d, b_d, layer, tmm, tn):
    p_pad, de = a.shape
    d = w_d.shape[-1]
    grid_spec = pltpu.PrefetchScalarGridSpec(
        num_scalar_prefetch=3, grid=(d // tn, p_pad // tmm),
        in_specs=[pl.BlockSpec((tmm, de), lambda j, i, te, tf, tv: (i, 0)),
                  pl.BlockSpec((None, None, de, tn), lambda j, i, te, tf, tv: (layer, te[i], 0, j)),
                  pl.BlockSpec((None, None, 1, tn), lambda j, i, te, tf, tv: (layer, te[i], 0, j))],
        out_specs=pl.BlockSpec((tmm, tn), lambda j, i, te, tf, tv: (i, j)),
        scratch_shapes=[pltpu.VMEM((de, tn), BF16)])
    b4 = b_d.reshape(b_d.shape[0], b_d.shape[1], 1, b_d.shape[2])
    return pl.pallas_call(
        _moe_down_kernel, grid_spec=grid_spec,
        out_shape=jax.ShapeDtypeStruct((p_pad, d), F32),
        compiler_params=_params(2), name="moe_down")(te, tf, tv, a, w_d, b4)


def _moe_combine_ln_kernel(*refs):
    y_refs = refs[:TOP_K]
    gate_ref, x_ref, g_ref, b_ref, o_ref, o16_ref = refs[TOP_K:]
    gate = gate_ref[...]
    out = gate[:, 0:1] * y_refs[0][...]
    for kk in range(1, TOP_K):
        out = out + gate[:, kk:kk + 1] * y_refs[kk][...]
    res = _layernorm_rows(DEEPNORM_ALPHA * x_ref[...] + out, g_ref[...], b_ref[...])
    o_ref[...] = res
    o16_ref[...] = res.astype(BF16)


def _moe_combine_ln(yk, gate, x, g, b, tm):
    t, d = x.shape
    nt = t // tm
    slot = lambda kk: pl.BlockSpec((tm, d), lambda i: (kk * nt + i, 0))
    return pl.pallas_call(
        _moe_combine_ln_kernel, grid=(nt,),
        in_specs=[slot(kk) for kk in range(TOP_K)]
        + [pl.BlockSpec((tm, LANES), lambda i: (i, 0)),
           pl.BlockSpec((tm, d), lambda i: (i, 0)),
           pl.BlockSpec((1, d), lambda i: (0, 0)), pl.BlockSpec((1, d), lambda i: (0, 0))],
        out_specs=[pl.BlockSpec((tm, d), lambda i: (i, 0))] * 2,
        out_shape=[jax.ShapeDtypeStruct((t, d), F32), jax.ShapeDtypeStruct((t, d), BF16)],
        compiler_params=_params(1), name="moe_combine_ln")(
            *([yk] * TOP_K), gate, x, g.reshape(1, d), b.reshape(1, d))


MOE_TILE_ROWS = 256
MOE_TILE_COLS = 1024
MOE_DOWN_TILE_COLS = 2048
MOE_ACT_CHUNK = 256
ROW_TILE = 128


def _moe(x, x16, layer, router_w, router_b, w_gu, b_gu, w_d, b_d, ln_g, ln_b):
    t, d = x.shape
    w_pad = jnp.pad(router_w[layer], ((0, 0), (0, LANES - N_EXPERTS)))
    b_pad = jnp.pad(router_b[layer], (0, LANES - N_EXPERTS)).reshape(1, LANES)
    top_i, gate = _router(x, w_pad, b_pad, ROW_TILE)

    n_pairs = t * TOP_K
    n_tiles = n_pairs // MOE_TILE_ROWS + N_EXPERTS
    e_flat = top_i[:, :TOP_K].reshape(n_pairs)
    onehot = (e_flat[:, None] == jnp.arange(N_EXPERTS, dtype=jnp.int32)[None, :]).astype(jnp.int32)
    csum = jnp.cumsum(onehot, axis=0)
    counts = csum[-1]
    rank = jnp.sum(onehot * csum, axis=1) - 1
    tiles_per = (counts + MOE_TILE_ROWS - 1) // MOE_TILE_ROWS
    tile_end = jnp.cumsum(tiles_per)
    dest = ((tile_end - tiles_per) * MOE_TILE_ROWS)[e_flat] + rank
    row_token = jnp.zeros((n_tiles * MOE_TILE_ROWS,), jnp.int32).at[dest].set(
        jnp.arange(n_pairs, dtype=jnp.int32) // TOP_K)
    tile_id = jnp.arange(n_tiles, dtype=jnp.int32)
    tv = (tile_id < tile_end[-1]).astype(jnp.int32)
    last_tile = jnp.minimum(tile_id, tile_end[-1] - 1)
    te = jnp.sum((tile_end[None, :] <= last_tile[:, None]).astype(jnp.int32), axis=1)
    tf = jnp.concatenate([jnp.ones((1,), jnp.int32), (te[1:] != te[:-1]).astype(jnp.int32)])

    xs = x[row_token]
    act = _moe_gate_up(te, tf, tv, xs, w_gu, b_gu, layer, MOE_TILE_ROWS, MOE_TILE_COLS)
    ys = _moe_down(te, tf, tv, act, w_d, b_d, layer, MOE_TILE_ROWS, MOE_DOWN_TILE_COLS)
    yk = ys[dest.reshape(t, TOP_K).T.reshape(n_pairs)]
    return _moe_combine_ln(yk, gate, x, ln_g, ln_b, ROW_TILE)


PROJ_ROW_TILE = 640
PROJ_A_ROW_TILE = 1024
PROJ_A_COL_TILE = 512
PROJ_C_COL_TILE = 896
OUT_ROW_TILE = 320
FOX_Q_TILE = 256
NSA_Q_TILE = 128


def _attention_layer(x, x16, i, n, s, nb, caches, page_table, w_in, b_f, w_out, rel_table, ln_g, ln_b):
    (cache_fox_k, cache_fox_v, cache_fox_logf, cache_cmp_k, cache_cmp_v, cache_sel_k, cache_sel_v,
     cache_win_k, cache_win_v) = caches
    tp = n * s
    past = page_table.shape[1] * PAGE_SIZE
    win = cache_win_k.shape[2]
    o_f = 3 * FOX_W
    o_n = o_f + FOX_HEADS
    o_g = o_n + NSA_W + 6 * NSA_KV_W
    w_main = jnp.concatenate([w_in[:, :o_f], w_in[:, o_n:o_g]], axis=1).astype(BF16)
    n_small = FOX_HEADS + 3 * NSA_HEADS
    w_small = jnp.pad(jnp.concatenate([w_in[:, o_f:o_n], w_in[:, o_g:]], axis=1), ((0, 0), (0, LANES - n_small)))
    b_small = jnp.pad(b_f, (0, LANES - FOX_HEADS))

    groups = [[FOX_W]] * 3 + [[NSA_W]] + [[NSA_KV_W, NSA_KV_W]] * 3
    qf_p, kf_p, vf_p, qn_p, *kv_p = _mm_split(x16, tp, w_main, groups, PROJ_A_ROW_TILE, PROJ_A_COL_TILE,
                                              "proj_a")
    hs = _mm(x16[tp:], w_main, nb, PROJ_A_COL_TILE, "proj_a_sample")
    logf_all, logf_t, sg_all = _gates_a(x, w_small, b_small, ROW_TILE)
    logf = logf_all[:, :FOX_HEADS]
    gates = sg_all[:, FOX_HEADS:n_small]

    c_row = _cumsum_seq(logf_t, n, s)
    o_fox_p = _fox_prompt(qf_p, kf_p, vf_p, c_row.reshape(n, FOX_HEADS, s, 1),
                          c_row.reshape(n, FOX_HEADS, 1, s), n, s, FOX_Q_TILE)
    tiles, wtiles, cmpb, bk, bw, bc = _nsa_bias_tables(rel_table, s, NSA_Q_TILE, past, win)
    gates_g = gates.reshape(-1, NSA_GROUPS, 3 * NSA_HPG).transpose(1, 0, 2)
    col_nsa = 3 * FOX_W
    o_nsa_p = _nsa_prompt(qn_p, kv_p, gates_g, tiles, wtiles, cmpb, n, s, NSA_Q_TILE)

    seg = lambda lo, w: hs[:, lo:lo + w]
    q_fox, k_fox, v_fox = seg(0, FOX_W), seg(FOX_W, FOX_W), seg(2 * FOX_W, FOX_W)
    q_nsa = seg(col_nsa, NSA_W).reshape(nb, NSA_HEADS, HEAD_DIM)
    kv = [seg(col_nsa + NSA_W + j * NSA_KV_W, NSA_KV_W).reshape(nb, NSA_GROUPS, HEAD_DIM) for j in range(6)]
    rows_view = lambda c: c.reshape(c.shape[0], c.shape[1], c.shape[2] * c.shape[3], HEAD_DIM)
    heads = lambda a: a.reshape(nb, FOX_HEADS, HEAD_DIM)
    o_fox_s = _fox_sample(page_table, i, heads(q_fox), heads(k_fox), heads(v_fox),
                          logf[tp:].reshape(nb, FOX_HEADS, 1),
                          rows_view(cache_fox_k), rows_view(cache_fox_v),
                          jnp.swapaxes(cache_fox_logf, 2, 3))
    o_c, sel = _nsa_sample_cmp(page_table, i, q_nsa, rows_view(cache_cmp_k), rows_view(cache_cmp_v), bc, past)
    o_nsa_s = _nsa_sample_sel(
        page_table, i, q_nsa, kv[2], kv[3], kv[4], kv[5], sel, o_c,
        gates[tp:].reshape(nb, NSA_HEADS, 3), rows_view(cache_sel_k), rows_view(cache_sel_v),
        rows_view(cache_win_k), rows_view(cache_win_v), bk, bw, rel_table[0].reshape(NSA_HEADS, 1))

    o_fox = jnp.concatenate([o_fox_p, o_fox_s.reshape(nb, FOX_W)], axis=0)
    o_nsa = jnp.concatenate([o_nsa_p, o_nsa_s.reshape(nb, NSA_W)], axis=0)
    w_out16 = w_out.astype(BF16)
    x_new, x_new16 = _mm_res_ln([o_fox, o_nsa], [w_out16[:FOX_W], w_out16[FOX_W:]], x, ln_g, ln_b,
                                OUT_ROW_TILE, "attn_out_ln")

    wb = min(NSA_WINDOW, s)

    def state(p, lo, heads, keep=s):
        p = p.reshape(n, s, heads, HEAD_DIM)[:, s - keep:]
        return p, hs[:, lo:lo + heads * HEAD_DIM].reshape(nb, 1, heads, HEAD_DIM)

    kv_lo = col_nsa + NSA_W
    states = [state(kf_p, FOX_W, FOX_HEADS), state(vf_p, 2 * FOX_W, FOX_HEADS),
              (logf[:tp].reshape(n, s, FOX_HEADS), logf[tp:].reshape(nb, 1, FOX_HEADS))]
    states += [state(kv_p[j], kv_lo + j * NSA_KV_W, NSA_GROUPS) for j in range(4)]
    states += [state(kv_p[j], kv_lo + j * NSA_KV_W, NSA_GROUPS, wb) for j in (4, 5)]
    return x_new, x_new16, states


def _gla_layer(x, x16, i, n, s, nb, state_gla, w_in, w_g2, b_g, g_norm, w_out, ln_g, ln_b):
    tp = n * s
    qk_w = GLA_HEADS * GLA_DK
    v_w = GLA_HEADS * GLA_DV
    o_g = 2 * qk_w + v_w
    o_r = o_g + GLA_GATE_RANK
    w_main = jnp.concatenate([w_in[:, :o_g], w_in[:, o_r:], w_in[:, o_g:o_r],
                              jnp.zeros((w_in.shape[0], LANES - GLA_GATE_RANK), w_in.dtype)], axis=1).astype(BF16)
    hc = _mm(x16, w_main, PROJ_ROW_TILE, PROJ_C_COL_TILE, "proj_c")
    w_g2_pad = jnp.pad(w_g2, ((0, LANES - GLA_GATE_RANK), (0, 0))).astype(BF16)
    la = _gla_gate(hc, (o_g + v_w) // LANES, w_g2_pad, b_g, PROJ_ROW_TILE)

    og_p, st_p = _gla_prompt(hc, la, g_norm, n, s)
    hs = hc[tp:]
    col = lambda a: a.reshape(nb, GLA_HEADS, GLA_DK, 1)
    rowv = lambda a: a.reshape(nb, GLA_HEADS, 1, GLA_DV)
    og_s, st_s = _gla_sample(col(hs[:, :qk_w]), col(hs[:, qk_w:2 * qk_w]), col(la[tp:]),
                             rowv(hs[:, 2 * qk_w:o_g]), rowv(hs[:, o_g:o_g + v_w]), g_norm, state_gla[i])
    og = jnp.concatenate([og_p, og_s.reshape(nb, v_w)], axis=0)
    x_new, x_new16 = _mm_res_ln([og], [w_out.astype(BF16)], x, ln_g, ln_b, OUT_ROW_TILE, "gla_out_ln")
    return x_new, x_new16, st_p, st_s


def kernel(x_prompt, x_sample, cache_fox_k, cache_fox_v, cache_fox_logf, cache_nsa_cmp_k, cache_nsa_cmp_v,
           cache_nsa_sel_k, cache_nsa_sel_v, cache_nsa_win_k, cache_nsa_win_v, state_gla, page_table,
           w_in_a, b_forget, w_out_a, rel_bias_table, w_in_c, w_gla_gate2, b_gla_gate, gla_norm_g, w_out_c,
           ln_g, ln_b, router_w, router_b, w_gate_up, b_gate_up, w_down, b_down):
    n, s, d = x_prompt.shape
    nb = x_sample.shape[0]
    assert x_sample.shape[1] == 1
    tp = n * s
    x = jnp.concatenate([x_prompt.reshape(tp, d), x_sample.reshape(nb, d)], axis=0)
    x16 = x.astype(BF16)
    caches = (cache_fox_k, cache_fox_v, cache_fox_logf, cache_nsa_cmp_k, cache_nsa_cmp_v,
              cache_nsa_sel_k, cache_nsa_sel_v, cache_nsa_win_k, cache_nsa_win_v)
    attn_states, gla_p, gla_s = [], [], []
    for layer in range(DEPTH):
        i = layer // 2
        if layer % 2 == 0:
            x, x16, st = _attention_layer(x, x16, i, n, s, nb, caches, page_table, w_in_a[i], b_forget[i],
                                          w_out_a[i], rel_bias_table, ln_g[layer, 0], ln_b[layer, 0])
            attn_states.append(st)
        else:
            x, x16, st_p, st_s = _gla_layer(x, x16, i, n, s, nb, state_gla, w_in_c[i], w_gla_gate2[i],
                                            b_gla_gate[i], gla_norm_g[i], w_out_c[i],
                                            ln_g[layer, 0], ln_b[layer, 0])
            gla_p.append(st_p)
            gla_s.append(st_s)
        x, x16 = _moe(x, x16, layer, router_w, router_b, w_gate_up, b_gate_up, w_down, b_down,
                      ln_g[layer, 1], ln_b[layer, 1])
    outs = [x[:tp].reshape(n, s, d), x[tp:].reshape(nb, 1, d)]
    for j in range(9):
        outs.append(jnp.stack([st[j][0] for st in attn_states]))
        outs.append(jnp.stack([st[j][1] for st in attn_states]))
    outs.append(jnp.stack(gla_p))
    outs.append(jnp.stack(gla_s))
    return tuple(outs)
```
